```python
import jax
import jax.numpy as jnp
from jax import lax
import numpy as np

D_MODEL = 1024
BATCH = 4
SEQ = 4096
DEPTH = 4

CTX_LEN = 256
GRID_W = 64
GLA_HEADS = 4
GLA_DK = 128
GLA_DV = 256
GLA_RANK = 16
GLA_TAU = 16.0
GLA_CHUNK = 64
NAT_HEADS = 8
NAT_DH = 64
WIN_H = 8
WIN_W = 16
NAT_QB = 16
NAT_BAND = NAT_QB + WIN_W
D_FF = 3584
N_EXPERTS = 8
TOP_K = 2
ROPE_BASE = 10000.0
EPS = 1e-6
GLA_QK = GLA_HEADS * GLA_DK
GLA_V = GLA_HEADS * GLA_DV
NAT_W = NAT_HEADS * NAT_DH
SPLITS = (GLA_QK, GLA_QK, GLA_V, GLA_V, GLA_RANK, GLA_RANK, NAT_W, NAT_W, NAT_W, D_MODEL, D_MODEL)
D_IN = sum(SPLITS)
N_DENSE = (DEPTH + 1) // 2
N_MOE = DEPTH // 2

kernel_name = 'hybrid_gla_natten_moe_dit_trunk'


def _rmsnorm(x, g):
    xf = x.astype(jnp.float32)
    y = xf * lax.rsqrt(jnp.mean(xf * xf, axis=-1, keepdims=True) + EPS)
    return (y * g.astype(jnp.float32)).astype(x.dtype)


def _modulate(x, shift, scale):
    return x * (1.0 + scale[:, None, :]) + shift[:, None, :]


def _heads(x, n_heads):
    b, t, w = x.shape
    return x.reshape(b, t, n_heads, w // n_heads).transpose(0, 2, 1, 3)


def _merge_heads(x):
    b, h, t, d = x.shape
    return x.transpose(0, 2, 1, 3).reshape(b, t, h * d)


def _axial_rope(x, row_pos, col_pos):
    half = x.shape[-1] // 2
    nf = half // 2
    inv = ROPE_BASE ** (-jnp.arange(nf, dtype=jnp.float32) / nf)

    def rot(xp, pos):
        ang = pos.astype(jnp.float32)[:, None] * inv[None, :]
        cos, sin = jnp.cos(ang), jnp.sin(ang)
        x1, x2 = xp[..., :nf], xp[..., nf:]
        return jnp.concatenate([x1 * cos - x2 * sin, x1 * sin + x2 * cos], axis=-1)

    return jnp.concatenate([rot(x[..., :half], row_pos), rot(x[..., half:], col_pos)], axis=-1).astype(x.dtype)


def _gla_log_gate(a, w2, b):
    z = jnp.einsum('btr,rk->btk', a.astype(jnp.float32), w2.astype(jnp.float32)) + b.astype(jnp.float32)
    return _heads(jax.nn.log_sigmoid(z) / GLA_TAU, GLA_HEADS)


def _gla_scan(q, k, v, lg, s0):
    bsz, h, t, dk = q.shape
    dv = v.shape[-1]
    n = t // GLA_CHUNK
    f32 = jnp.float32
    q = q.astype(f32).reshape(bsz, h, n, GLA_CHUNK, dk)
    k = k.astype(f32).reshape(bsz, h, n, GLA_CHUNK, dk)
    v = v.astype(f32).reshape(bsz, h, n, GLA_CHUNK, dv)
    b = jnp.cumsum(lg.reshape(bsz, h, n, GLA_CHUNK, dk), axis=3)
    b_last = b[:, :, :, -1:, :]
    q_e = q * jnp.exp(b)
    k_e = k * jnp.exp(-b)
    k_d = k * jnp.exp(b_last - b)
    tri = np.tril(np.ones((GLA_CHUNK, GLA_CHUNK), dtype=bool))
    a = jnp.where(tri, jnp.einsum('bhncd,bhnsd->bhncs', q_e, k_e), 0.0)
    o_intra = jnp.einsum('bhncs,bhnsv->bhncv', a, v)
    decay = jnp.exp(b_last[:, :, :, 0, :])

    def step(state, xs):
        qe, kd, vv, dec = xs
        o = jnp.einsum('bhcd,bhdv->bhcv', qe, state)
        state = dec[..., None] * state + jnp.einsum('bhcd,bhcv->bhdv', kd, vv)
        return state, o

    mv = lambda arr: jnp.moveaxis(arr, 2, 0)
    s_fin, o_inter = lax.scan(step, s0.astype(f32), (mv(q_e), mv(k_d), mv(v), mv(decay)))
    o = o_intra + jnp.moveaxis(o_inter, 0, 2)
    return o.reshape(bsz, h, t, dv), s_fin


def _gla_bidir(q, k, v, lg_f, lg_b, s0_f, s0_b):
    o_f, s_f = _gla_scan(q, k, v, lg_f, s0_f)
    rev = lambda arr: jnp.flip(arr, axis=2)
    o_b, s_b = _gla_scan(rev(q), rev(k), rev(v), rev(lg_b), s0_b)
    return o_f + rev(o_b), s_f, s_b


def _gla_out(o, r, gain):
    on = o * lax.rsqrt(jnp.mean(o * o, axis=-1, keepdims=True) + EPS)
    return (_merge_heads(on) * gain.astype(jnp.float32)).astype(r.dtype) * jax.nn.silu(r)


def _natten_latent(q, k, v, k_ctx, v_ctx, rpb):
    bsz, h, s, d = q.shape
    rows = s // GRID_W
    kh = min(WIN_H, rows)
    nb = GRID_W // NAT_QB
    scale = d ** -0.5
    r = np.arange(rows)
    row_start = np.clip(r - kh // 2, 0, rows - kh)
    row_off = row_start[:, None] + np.arange(kh)[None, :] - r[:, None] + (WIN_H - 1)
    qcol = np.arange(GRID_W).reshape(nb, NAT_QB)
    col_start = np.clip(qcol - WIN_W // 2, 0, GRID_W - WIN_W)
    band_start = np.clip(np.arange(nb) * NAT_QB - WIN_W // 2, 0, GRID_W - NAT_BAND)
    kcol = band_start[:, None] + np.arange(NAT_BAND)[None, :]
    col_in = (kcol[:, None, :] >= col_start[:, :, None]) & (kcol[:, None, :] < col_start[:, :, None] + WIN_W)
    col_off = np.clip(kcol[:, None, :] - qcol[:, :, None], 1 - WIN_W, WIN_W - 1) + (WIN_W - 1)
    bias = rpb.astype(jnp.float32)[:, row_off][..., col_off]
    bias = jnp.where(col_in[:, :, None, :], bias.transpose(1, 0, 3, 4, 2, 5), -jnp.inf)
    kb = k.reshape(bsz, h, rows, GRID_W, d)[:, :, :, kcol]
    vb = v.reshape(bsz, h, rows, GRID_W, d)[:, :, :, kcol]
    qr = q.reshape(bsz, h, rows, nb, NAT_QB, d).transpose(2, 0, 1, 3, 4, 5)
    n_lat = kh * NAT_BAND

    def one_row(args):
        q_r, rs, bias_r = args
        k_r = lax.dynamic_slice_in_dim(kb, rs, kh, axis=2)
        v_r = lax.dynamic_slice_in_dim(vb, rs, kh, axis=2)
        s_lat = jnp.einsum('bhnqd,bhknjd->bhnqkj', q_r, k_r).astype(jnp.float32) * scale + bias_r
        s_ctx = jnp.einsum('bhnqd,bhld->bhnql', q_r, k_ctx).astype(jnp.float32) * scale
        sc = jnp.concatenate([s_lat.reshape(bsz, h, nb, NAT_QB, n_lat), s_ctx], axis=-1)
        p = jax.nn.softmax(sc, axis=-1).astype(v.dtype)
        p_lat = p[..., :n_lat].reshape(bsz, h, nb, NAT_QB, kh, NAT_BAND)
        return (jnp.einsum('bhnqkj,bhknjd->bhnqd', p_lat, v_r)
                + jnp.einsum('bhnql,bhld->bhnqd', p[..., n_lat:], v_ctx))

    out = lax.map(one_row, (qr, jnp.asarray(row_start, jnp.int32), bias))
    return out.transpose(1, 2, 0, 3, 4, 5).reshape(bsz, h, s, d)


def _ctx_attn(q, k, v):
    s = jnp.einsum('bhld,bhmd->bhlm', q, k).astype(jnp.float32) * (q.shape[-1] ** -0.5)
    p = jax.nn.softmax(s, axis=-1).astype(v.dtype)
    return jnp.einsum('bhlm,bhmd->bhld', p, v)


def _merge_branches(y_gla, y_nat, ga, gb, w_bg, w_bn, w_o):
    m = jax.nn.sigmoid(ga) * (y_gla @ w_bg) + jax.nn.sigmoid(gb) * (y_nat @ w_bn)
    return m @ w_o


def _token_mixer(h_c, h_l, w_in, gate_w2, gate_b, gla_g, rpb, w_bg, w_bn, w_o, need_ctx):
    bsz, s, _ = h_l.shape
    idx = np.cumsum(SPLITS)[:-1].tolist()
    gq_c, gk_c, gv_c, gr_c, af_c, ab_c, nq_c, nk_c, nv_c, ga_c, gb_c = jnp.split(h_c @ w_in, idx, axis=-1)
    gq_l, gk_l, gv_l, gr_l, af_l, ab_l, nq_l, nk_l, nv_l, ga_l, gb_l = jnp.split(h_l @ w_in, idx, axis=-1)
    t = jnp.arange(s, dtype=jnp.int32)
    row_pos, col_pos = t // GRID_W, t % GRID_W
    qscale = GLA_DK ** -0.5
    zero = jnp.zeros((bsz, GLA_HEADS, GLA_DK, GLA_DV), jnp.float32)
    o_gc, s_f, s_b = _gla_bidir(_heads(gq_c, GLA_HEADS) * qscale, _heads(gk_c, GLA_HEADS), _heads(gv_c, GLA_HEADS),
                                _gla_log_gate(af_c, gate_w2[0], gate_b[0]), _gla_log_gate(ab_c, gate_w2[1], gate_b[1]),
                                zero, zero)
    q_l = _axial_rope(_heads(gq_l, GLA_HEADS), row_pos, col_pos) * qscale
    k_l = _axial_rope(_heads(gk_l, GLA_HEADS), row_pos, col_pos)
    o_gl, _, _ = _gla_bidir(q_l, k_l, _heads(gv_l, GLA_HEADS),
                            _gla_log_gate(af_l, gate_w2[0], gate_b[0]), _gla_log_gate(ab_l, gate_w2[1], gate_b[1]),
                            s_f, s_b)
    nk_ch, nv_ch = _heads(nk_c, NAT_HEADS), _heads(nv_c, NAT_HEADS)
    o_nl = _natten_latent(_heads(nq_l, NAT_HEADS), _heads(nk_l, NAT_HEADS), _heads(nv_l, NAT_HEADS), nk_ch, nv_ch, rpb)
    y_l = _merge_branches(_gla_out(o_gl, gr_l, gla_g), _merge_heads(o_nl), ga_l, gb_l, w_bg, w_bn, w_o)
    if not need_ctx:
        return y_l, None
    o_nc = _ctx_attn(_heads(nq_c, NAT_HEADS), nk_ch, nv_ch)
    y_c = _merge_branches(_gla_out(o_gc, gr_c, gla_g), _merge_heads(o_nc), ga_c, gb_c, w_bg, w_bn, w_o)
    return y_l, y_c


def _swiglu(x, w1, w3, w2):
    return (jax.nn.silu(x @ w1) * (x @ w3)) @ w2


def _moe(x, w_r, w1, w3, w2):
    logits = jnp.einsum('btd,de->bte', x, w_r).astype(jnp.float32)
    top_v, top_i = lax.top_k(logits, TOP_K)
    wts = jax.nn.softmax(top_v, axis=-1)
    gates = jnp.sum(jax.nn.one_hot(top_i, N_EXPERTS, dtype=jnp.float32) * wts[..., None], axis=-2)
    y = jnp.zeros_like(x)
    for e in range(N_EXPERTS):
        y = y + gates[..., e:e + 1].astype(x.dtype) * _swiglu(x, w1[e], w3[e], w2[e])
    return y


def _channel_mixer(h, i, ffn_w1, ffn_w3, ffn_w2, moe_router, moe_w1, moe_w3, moe_w2):
    j = i // 2
    if i % 2 == 0:
        return _swiglu(h, ffn_w1[j], ffn_w3[j], ffn_w2[j])
    return _moe(h, moe_router[j], moe_w1[j], moe_w3[j], moe_w2[j])


def setup_inputs(seed: int = 0) -> dict:
    key = jax.random.key(seed)
    ks = jax.random.split(key, 32)
    D = D_MODEL

    def nrm(k, shape, s):
        return jax.random.normal(k, shape, jnp.float32) * s

    return {
        'x': nrm(ks[0], (BATCH, SEQ, D), 1.0),
        'c': nrm(ks[1], (BATCH, D), 1.0),
        'ctx': nrm(ks[2], (BATCH, CTX_LEN, D), 1.0),
        'c_ctx': nrm(ks[3], (D,), 1.0),
        'w_mod': nrm(ks[4], (DEPTH, D, 6 * D), 0.5 * D ** -0.5),
        'b_mod': nrm(ks[5], (DEPTH, 6 * D), 0.01),
        'norm_mix_pre': 1.0 + nrm(ks[6], (DEPTH, D), 0.02),
        'norm_mix_post': 1.0 + nrm(ks[7], (DEPTH, D), 0.02),
        'w_in': nrm(ks[8], (DEPTH, D, D_IN), D ** -0.5),
        'gla_gate_w2': nrm(ks[9], (DEPTH, 2, GLA_RANK, GLA_QK), GLA_RANK ** -0.5),
        'gla_gate_b': nrm(ks[10], (DEPTH, 2, GLA_QK), 0.1),
        'gla_norm': 1.0 + nrm(ks[11], (DEPTH, GLA_V), 0.02),
        'nat_rpb': nrm(ks[12], (DEPTH, NAT_HEADS, 2 * WIN_H - 1, 2 * WIN_W - 1), 0.02),
        'w_branch_gla': nrm(ks[13], (DEPTH, GLA_V, D), GLA_V ** -0.5),
        'w_branch_nat': nrm(ks[14], (DEPTH, NAT_W, D), NAT_W ** -0.5),
        'w_out': nrm(ks[15], (DEPTH, D, D), D ** -0.5),
        'norm_ffn_pre': 1.0 + nrm(ks[16], (DEPTH, D), 0.02),
        'norm_ffn_post': 1.0 + nrm(ks[17], (DEPTH, D), 0.02),
        'ffn_w1': nrm(ks[18], (N_DENSE, D, D_FF), D ** -0.5),
        'ffn_w3': nrm(ks[19], (N_DENSE, D, D_FF), D ** -0.5),
        'ffn_w2': nrm(ks[20], (N_DENSE, D_FF, D), D_FF ** -0.5),
        'moe_router': nrm(ks[21], (N_MOE, D, N_EXPERTS), D ** -0.5),
        'moe_w1': nrm(ks[22], (N_MOE, N_EXPERTS, D, D_FF), D ** -0.5),
        'moe_w3': nrm(ks[23], (N_MOE, N_EXPERTS, D, D_FF), D ** -0.5),
        'moe_w2': nrm(ks[24], (N_MOE, N_EXPERTS, D_FF, D), D_FF ** -0.5),
    }


def reference(x, c, ctx, c_ctx, w_mod, b_mod, norm_mix_pre, norm_mix_post, w_in, gla_gate_w2, gla_gate_b,
              gla_norm, nat_rpb, w_branch_gla, w_branch_nat, w_out, norm_ffn_pre, norm_ffn_post,
              ffn_w1, ffn_w3, ffn_w2, moe_router, moe_w1, moe_w3, moe_w2):
    x_l, x_c = x, ctx
    s_lat = jax.nn.silu(c)
    s_ctx = jax.nn.silu(c_ctx)[None]
    for i in range(DEPTH):
        need_ctx = i < DEPTH - 1
        m_l = jnp.split(s_lat @ w_mod[i] + b_mod[i], 6, axis=-1)
        m_c = jnp.split(s_ctx @ w_mod[i] + b_mod[i], 6, axis=-1)
        h_l = _modulate(_rmsnorm(x_l, norm_mix_pre[i]), m_l[0], m_l[1])
        h_c = _modulate(_rmsnorm(x_c, norm_mix_pre[i]), m_c[0], m_c[1])
        y_l, y_c = _token_mixer(h_c, h_l, w_in[i], gla_gate_w2[i], gla_gate_b[i], gla_norm[i], nat_rpb[i],
                                w_branch_gla[i], w_branch_nat[i], w_out[i], need_ctx)
        x_l = x_l + m_l[2][:, None, :] * _rmsnorm(y_l, norm_mix_post[i])
        f_l = _channel_mixer(_modulate(_rmsnorm(x_l, norm_ffn_pre[i]), m_l[3], m_l[4]), i,
                             ffn_w1, ffn_w3, ffn_w2, moe_router, moe_w1, moe_w3, moe_w2)
        x_l = x_l + m_l[5][:, None, :] * _rmsnorm(f_l, norm_ffn_post[i])
        if need_ctx:
            x_c = x_c + m_c[2][:, None, :] * _rmsnorm(y_c, norm_mix_post[i])
            f_c = _channel_mixer(_modulate(_rmsnorm(x_c, norm_ffn_pre[i]), m_c[3], m_c[4]), i,
                                 ffn_w1, ffn_w3, ffn_w2, moe_router, moe_w1, moe_w3, moe_w2)
            x_c = x_c + m_c[5][:, None, :] * _rmsnorm(f_c, norm_ffn_post[i])
    return x_l
```

```python
import functools

import numpy as np
import jax
import jax.numpy as jnp
from jax import lax
from jax.experimental import pallas as pl
from jax.experimental.pallas import tpu as pltpu

EPS = 1e-6
GRID_W = 64
TILE = 256
TILE_ROWS = TILE // GRID_W
GLA_HEADS, GLA_DK, GLA_DV, GLA_RANK, GLA_TAU, GLA_CHUNK = 4, 128, 256, 16, 16.0, 64
NAT_HEADS, NAT_DH = 8, 64
WIN_H, WIN_W = 8, 16
N_EXPERTS = 8
ROPE_BASE = 10000.0
GLA_QK = GLA_HEADS * GLA_DK
GLA_V = GLA_HEADS * GLA_DV
NAT_W = NAT_HEADS * NAT_DH
LANES = 128
COL_GQ, COL_GK, COL_GV, COL_GR, COL_GA, COL_GB = 0, 512, 1024, 2048, 3072, 4096
COL_NQ, COL_NK, COL_NV, COL_CODE = 5120, 5632, 6144, 6656
N_IN = COL_CODE + LANES
VMEM_LIMIT = 56 * 1024 * 1024
BF16 = jnp.bfloat16
F32 = jnp.float32


def _dot(a, b):
    return jnp.dot(a, b, preferred_element_type=F32)


def _dot_nt(a, b):
    return lax.dot_general(a, b, (((1,), (1,)), ((), ())), preferred_element_type=F32)


def _dot_tn(a, b):
    return lax.dot_general(a, b, (((0,), (0,)), ((), ())), preferred_element_type=F32)


def _rms(x, g):
    return x * lax.rsqrt(jnp.mean(x * x, axis=-1, keepdims=True) + EPS) * g


def _split_dot(a, b_hi, b_lo):
    a_hi = a.astype(BF16)
    a_lo = (a - a_hi.astype(F32)).astype(BF16)
    return _dot(a_hi, b_hi) + (_dot(a_lo, b_hi) + _dot(a_hi, b_lo))


def _params(sem):
    return pltpu.CompilerParams(dimension_semantics=sem, vmem_limit_bytes=VMEM_LIMIT)


def _mod_kernel(c_ref, w_ref, b_ref, o_ref):
    c = c_ref[...]
    s = c * jax.nn.sigmoid(c)
    o_ref[0] = _dot(s.astype(BF16), w_ref[0].astype(BF16)) + b_ref[0]


def _modulation(c_rows, w_mod, b_mod):
    depth, d, n = w_mod.shape
    rows = c_rows.shape[0]
    tn = 1536
    return pl.pallas_call(
        _mod_kernel,
        grid=(depth, n // tn),
        in_specs=[pl.BlockSpec((rows, d), lambda l, j: (0, 0)),
                  pl.BlockSpec((1, d, tn), lambda l, j: (l, 0, j)),
                  pl.BlockSpec((1, 1, tn), lambda l, j: (l, 0, j))],
        out_specs=pl.BlockSpec((1, rows, tn), lambda l, j: (l, 0, j)),
        out_shape=jax.ShapeDtypeStruct((depth, rows, n), F32),
        compiler_params=_params(("arbitrary", "arbitrary")),
        name="modulation",
    )(c_rows, w_mod, b_mod.reshape(depth, 1, n))


def _inproj_kernel(x_ref, g_ref, sh_ref, sc_ref, w_ref, o_ref):
    h = _rms(x_ref[...], g_ref[...]) * (1.0 + sc_ref[...]) + sh_ref[...]
    hb = h.astype(BF16)
    n = o_ref.shape[1]
    step = 1024
    for j in range(0, n, step):
        w = min(step, n - j)
        o_ref[:, j:j + w] = _dot(hb, w_ref[:, j:j + w]).astype(o_ref.dtype)


def _mod_spec(comp, row_of_tile):
    return pl.BlockSpec((None, None, 1, None), lambda i: (row_of_tile(i), comp, 0, 0))


def _in_projection(x, g, mod, w, nb, nt):
    t, d = x.shape
    n = w.shape[1]
    row_of_tile = lambda i: jnp.where(i < nb, nb, (i - nb) // nt)
    dm = mod.shape[-1]
    mspec = lambda comp: pl.BlockSpec((None, None, 1, dm), lambda i: (row_of_tile(i), comp, 0, 0))
    return pl.pallas_call(
        _inproj_kernel,
        grid=(t // TILE,),
        in_specs=[pl.BlockSpec((TILE, d), lambda i: (i, 0)),
                  pl.BlockSpec((1, d), lambda i: (0, 0)),
                  mspec(0), mspec(1),
                  pl.BlockSpec((d, n), lambda i: (0, 0))],
        out_specs=pl.BlockSpec((TILE, n), lambda i: (i, 0)),
        out_shape=jax.ShapeDtypeStruct((t, n), BF16),
        compiler_params=_params(("arbitrary",)),
        name="in_projection",
    )(x, g, mod, mod, w)


def _gla_direction(d, q_ref, k_ref, v_ref, code_ref, cos_ref, sin_ref, w2hi_ref, w2lo_ref, gb_ref,
                   cum_ref, tot_ref, st_ref, o_ref, qe_ref, ke_ref, kd_ref, dec_ref):
    lane = lax.broadcasted_iota(jnp.int32, (TILE, GLA_QK), 1)
    first_half = (lane % 64) < 32
    cos = jnp.concatenate([cos_ref[...]] * GLA_HEADS, axis=1)
    sin = jnp.concatenate([sin_ref[...]] * GLA_HEADS, axis=1)

    def rope(x):
        partner = jnp.where(first_half, pltpu.roll(x, GLA_QK - 32, 1), pltpu.roll(x, 32, 1))
        return x * cos + partner * sin

    z = _dot(code_ref[...], w2hi_ref[d]) + _dot(code_ref[...], w2lo_ref[d]) + gb_ref[d]
    lg = (jnp.minimum(z, 0.0) - jnp.log(1.0 + jnp.exp(-jnp.abs(z)))) * (1.0 / GLA_TAU)
    lg_hi = lg.astype(BF16)
    lg_lo = (lg - lg_hi.astype(F32)).astype(BF16)
    cum = _dot(cum_ref[d], lg_hi) + _dot(cum_ref[d], lg_lo)
    tot = _dot(tot_ref[...], lg_hi) + _dot(tot_ref[...], lg_lo)
    q = rope(q_ref[...].astype(F32)) * (GLA_DK ** -0.5)
    k = rope(k_ref[...].astype(F32))
    qe_ref[...] = (q * jnp.exp(cum)).astype(BF16)
    ke_ref[...] = (k * jnp.exp(-cum)).astype(BF16)
    kd_ref[...] = (k * jnp.exp(tot - cum)).astype(BF16)
    dec_ref[...] = jnp.exp(tot)

    r = lax.broadcasted_iota(jnp.int32, (GLA_CHUNK, GLA_CHUNK), 0)
    s = lax.broadcasted_iota(jnp.int32, (GLA_CHUNK, GLA_CHUNK), 1)
    causal = (s <= r) if d == 0 else (s >= r)
    n_chunks = TILE // GLA_CHUNK

    def chunk(ci, carry):
        c = ci if d == 0 else n_chunks - 1 - ci
        rows = pl.ds(pl.multiple_of(c * GLA_CHUNK, GLA_CHUNK), GLA_CHUNK)
        for h in range(GLA_HEADS):
            kcols = slice(h * GLA_DK, (h + 1) * GLA_DK)
            vcols = slice(h * GLA_DV, (h + 1) * GLA_DV)
            qe = qe_ref[rows, kcols]
            v = v_ref[rows, vcols]
            a = jnp.where(causal, _dot_nt(qe, ke_ref[rows, kcols]), 0.0)
            st = st_ref[d, h]
            o = _dot(a.astype(BF16), v) + _dot_nt(qe, st.astype(BF16))
            o_ref[rows, vcols] = o.astype(o_ref.dtype)
            dec = dec_ref[rows, kcols][0:1, :]
            st_ref[d, h] = st * dec + _dot_tn(v, kd_ref[rows, kcols])
        return carry

    lax.fori_loop(0, n_chunks, chunk, 0)


def _gla_kernel(qf, kf, vf, cf, cosf, sinf, qb, kb, vb, cb, cosb, sinb, w2hi, w2lo, gb, cum, tot,
                of, ob, st_ref, qe_ref, ke_ref, kd_ref, dec_ref):
    @pl.when(pl.program_id(1) == 0)
    def _():
        st_ref[...] = jnp.zeros_like(st_ref)

    _gla_direction(0, qf, kf, vf, cf, cosf, sinf, w2hi, w2lo, gb, cum, tot, st_ref, of,
                   qe_ref, ke_ref, kd_ref, dec_ref)
    _gla_direction(1, qb, kb, vb, cb, cosb, sinb, w2hi, w2lo, gb, cum, tot, st_ref, ob,
                   qe_ref, ke_ref, kd_ref, dec_ref)


def _gla(y, rope_cos, rope_sin, w2hi, w2lo, gbias, cum_m, tot_m, nb, nt):
    t = y.shape[0]
    fwd = lambda b, j: jnp.where(j == 0, b, nb + b * nt + j - 1)
    bwd = lambda b, j: jnp.where(j == 0, b, nb + b * nt + nt - j)
    fwd_loc = lambda b, j: j
    bwd_loc = lambda b, j: jnp.where(j == 0, 0, nt + 1 - j)

    def ysl(width, col, tile):
        return pl.BlockSpec((TILE, width), lambda b, j: (tile(b, j), col // width))

    def direction(tile, loc):
        return [ysl(GLA_QK, COL_GQ, tile), ysl(GLA_QK, COL_GK, tile), ysl(GLA_V, COL_GV, tile),
                ysl(LANES, COL_CODE, tile),
                pl.BlockSpec((TILE, LANES), lambda b, j: (loc(b, j), 0)),
                pl.BlockSpec((TILE, LANES), lambda b, j: (loc(b, j), 0))]

    const = lambda shape: pl.BlockSpec(shape, lambda b, j: (0,) * len(shape))
    in_specs = (direction(fwd, fwd_loc) + direction(bwd, bwd_loc)
                + [const(w2hi.shape), const(w2lo.shape), const(gbias.shape), const(cum_m.shape), const(tot_m.shape)])
    args = [y, y, y, y, rope_cos, rope_sin] * 2 + [w2hi, w2lo, gbias, cum_m, tot_m]
    return pl.pallas_call(
        _gla_kernel,
        grid=(nb, nt + 1),
        in_specs=in_specs,
        out_specs=[pl.BlockSpec((TILE, GLA_V), lambda b, j: (fwd(b, j), 0)),
                   pl.BlockSpec((TILE, GLA_V), lambda b, j: (bwd(b, j), 0))],
        out_shape=[jax.ShapeDtypeStruct((t, GLA_V), BF16)] * 2,
        scratch_shapes=[pltpu.VMEM((2, GLA_HEADS, GLA_DV, GLA_DK), F32),
                        pltpu.VMEM((TILE, GLA_QK), BF16), pltpu.VMEM((TILE, GLA_QK), BF16),
                        pltpu.VMEM((TILE, GLA_QK), BF16), pltpu.VMEM((TILE, GLA_QK), F32)],
        compiler_params=_params(("arbitrary", "arbitrary")),
        name="gla_scan",
    )(*args)


def _nat_kernel(q_ref, k0, k1, k2, kc, v0, v1, v2, vc, bias_ref, o_ref):
    lane = lax.broadcasted_iota(jnp.int32, (TILE, LANES), 1)
    low = lane < NAT_DH
    for p in range(NAT_HEADS // 2):
        cols = slice(p * LANES, (p + 1) * LANES)
        qp = q_ref[:, cols] * (NAT_DH ** -0.5)
        ks = [r[:, cols] for r in (k0, k1, k2, kc)]
        vs = [r[:, cols] for r in (v0, v1, v2, vc)]
        outs = []
        for half in range(2):
            qh = jnp.where(low if half == 0 else ~low, qp, jnp.zeros_like(qp))
            s_lat = jnp.concatenate([_dot_nt(qh, kk) for kk in ks[:3]], axis=1) + bias_ref[2 * p + half]
            s_ctx = _dot_nt(qh, ks[3])
            m = jnp.maximum(jnp.max(s_lat, axis=-1, keepdims=True), jnp.max(s_ctx, axis=-1, keepdims=True))
            e_lat = jnp.exp(s_lat - m)
            e_ctx = jnp.exp(s_ctx - m)
            den = jnp.sum(e_lat, axis=-1, keepdims=True) + jnp.sum(e_ctx, axis=-1, keepdims=True)
            e_lat = e_lat.astype(BF16)
            acc = _dot(e_ctx.astype(BF16), vs[3])
            for i in range(3):
                acc += _dot(e_lat[:, i * TILE:(i + 1) * TILE], vs[i])
            outs.append(acc / den)
        o_ref[:, cols] = jnp.where(low, outs[0], outs[1]).astype(o_ref.dtype)


def _nat(y, bias, nb, nt):
    t = y.shape[0]
    lat = lambda b, tt: nb + b * nt + jnp.clip(tt, 0, nt - 1)
    qtile = lambda b, j: jnp.where(j == 0, b, nb + b * nt + j - 1)
    pattern = lambda b, j: jnp.where(j == 0, 3, jnp.where(j == 1, 0, jnp.where(j == nt, 2, 1)))

    def ysl(col, tile):
        return pl.BlockSpec((TILE, NAT_W), lambda b, j: (tile(b, j), col // NAT_W))

    slab = [lambda b, j: lat(b, j - 2), lambda b, j: lat(b, j - 1), lambda b, j: lat(b, j), lambda b, j: b]
    in_specs = ([ysl(COL_NQ, qtile)] + [ysl(COL_NK, s) for s in slab] + [ysl(COL_NV, s) for s in slab]
                + [pl.BlockSpec((None, NAT_HEADS, TILE, 3 * TILE), lambda b, j: (pattern(b, j), 0, 0, 0))])
    return pl.pallas_call(
        _nat_kernel,
        grid=(nb, nt + 1),
        in_specs=in_specs,
        out_specs=pl.BlockSpec((TILE, NAT_W), lambda b, j: (qtile(b, j), 0)),
        out_shape=jax.ShapeDtypeStruct((t, NAT_W), BF16),
        compiler_params=_params(("arbitrary", "arbitrary")),
        name="nat_attention",
    )(*([y] * 9), bias)


def _nat_bias_table(rpb, nt):
    rows = nt * TILE_ROWS
    kh = min(WIN_H, rows)
    qi = np.arange(TILE) // GRID_W
    qc = np.arange(TILE) % GRID_W
    kj = np.arange(3 * TILE) // GRID_W
    kc = np.arange(3 * TILE) % GRID_W
    col_start = np.clip(qc - WIN_W // 2, 0, GRID_W - WIN_W)
    col_ok = (kc[None, :] >= col_start[:, None]) & (kc[None, :] < col_start[:, None] + WIN_W)
    col_off = np.clip(kc[None, :] - qc[:, None], 1 - WIN_W, WIN_W - 1) + (WIN_W - 1)
    tabs = []
    for t in (0, min(1, nt - 1), nt - 1):
        r = TILE_ROWS * t + qi
        row_start = np.clip(r - kh // 2, 0, rows - kh)
        kr = TILE_ROWS * (t - 1) + kj
        tile_of = kr // TILE_ROWS
        valid = (tile_of >= 0) & (tile_of < nt)
        row_ok = (kr[None, :] >= row_start[:, None]) & (kr[None, :] < row_start[:, None] + kh) & valid[None, :]
        row_off = np.clip(kr[None, :] - r[:, None] + (WIN_H - 1), 0, 2 * WIN_H - 2)
        ok = row_ok & col_ok
        b = rpb.astype(F32)[:, row_off, col_off]
        tabs.append(jnp.where(ok[None], b, -jnp.inf))
    tabs.append(jnp.full_like(tabs[0], -jnp.inf))
    return jnp.stack(tabs)


def _merge_kernel(of_ref, ob_ref, gr_ref, ga_ref, gb_ref, on_ref, x_ref, gain_ref, wbg_ref, wbn_ref, wo_ref,
                  gpost_ref, m2_ref, gpre_ref, m3_ref, m4_ref, *rest, with_router):
    if with_router:
        wrhi_ref, wrlo_ref, x_out, h_out, gate_out = rest
    else:
        x_out, h_out = rest
    o = of_ref[...].astype(F32) + ob_ref[...].astype(F32)
    parts = []
    for h in range(GLA_HEADS):
        oh = o[:, h * GLA_DV:(h + 1) * GLA_DV]
        parts.append(oh * lax.rsqrt(jnp.mean(oh * oh, axis=-1, keepdims=True) + EPS))
    r = gr_ref[...].astype(F32)
    yg = jnp.concatenate(parts, axis=1) * gain_ref[...] * (r * jax.nn.sigmoid(r))
    m = (jax.nn.sigmoid(ga_ref[...].astype(F32)) * _dot(yg.astype(BF16), wbg_ref[...])
         + jax.nn.sigmoid(gb_ref[...].astype(F32)) * _dot(on_ref[...], wbn_ref[...]))
    y = _dot(m.astype(BF16), wo_ref[...])
    x1 = x_ref[...] + m2_ref[...] * _rms(y, gpost_ref[...])
    x_out[...] = x1
    h = _rms(x1, gpre_ref[...]) * (1.0 + m4_ref[...]) + m3_ref[...]
    h_out[...] = h.astype(h_out.dtype)
    if with_router:
        logits = _split_dot(h, wrhi_ref[...], wrlo_ref[...])
        lane = lax.broadcasted_iota(jnp.int32, logits.shape, 1)
        lg = jnp.where(lane < N_EXPERTS, logits, -jnp.inf)
        v1 = jnp.max(lg, axis=-1, keepdims=True)
        i1 = jnp.min(jnp.where(lg == v1, lane, LANES), axis=-1, keepdims=True)
        lg2 = jnp.where(lane == i1, -jnp.inf, lg)
        v2 = jnp.max(lg2, axis=-1, keepdims=True)
        i2 = jnp.min(jnp.where(lg2 == v2, lane, LANES), axis=-1, keepdims=True)
        e2 = jnp.exp(v2 - v1)
        w1 = 1.0 / (1.0 + e2)
        w2 = e2 / (1.0 + e2)
        gate_out[...] = jnp.where(lane == i1, w1, 0.0) + jnp.where(lane == i2, w2, 0.0)


def _merge(y, of, ob, on, x, gain, wbg, wbn, wo, gpost, gpre, mod, router, nb, nt):
    t, d = x.shape
    row_of_tile = lambda i: jnp.where(i < nb, nb, (i - nb) // nt)
    mspec = lambda comp: pl.BlockSpec((None, None, 1, d), lambda i: (row_of_tile(i), comp, 0, 0))
    rowblk = lambda w: pl.BlockSpec((TILE, w), lambda i: (i, 0))
    ycol = lambda w, col: pl.BlockSpec((TILE, w), lambda i: (i, col // w))
    const = lambda a: pl.BlockSpec(a.shape, lambda i: (0,) * a.ndim)
    in_specs = [rowblk(GLA_V), rowblk(GLA_V), ycol(GLA_V, COL_GR), ycol(d, COL_GA), ycol(d, COL_GB), rowblk(NAT_W),
                rowblk(d), const(gain), const(wbg), const(wbn), const(wo), const(gpost), mspec(2), const(gpre),
                mspec(3), mspec(4)]
    args = [of, ob, y, y, y, on, x, gain, wbg, wbn, wo, gpost, mod, gpre, mod, mod]
    out_specs = [rowblk(d), rowblk(d)]
    out_shape = [jax.ShapeDtypeStruct((t, d), F32), jax.ShapeDtypeStruct((t, d), BF16)]
    if router is not None:
        in_specs += [const(router[0]), const(router[1])]
        args += list(router)
        out_specs.append(rowblk(LANES))
        out_shape.append(jax.ShapeDtypeStruct((t, LANES), F32))
    return pl.pallas_call(
        functools.partial(_merge_kernel, with_router=router is not None),
        grid=(t // TILE,),
        in_specs=in_specs,
        out_specs=out_specs,
        out_shape=out_shape,
        compiler_params=_params(("arbitrary",)),
        name="merge_router" if router is not None else "merge",
    )(*args)


def _ffn_kernel(h_ref, w1_ref, w3_ref, w2_ref, x_ref, gpost_ref, m5_ref, o_ref, acc_ref):
    f = pl.program_id(1)

    @pl.when(f == 0)
    def _():
        acc_ref[...] = jnp.zeros_like(acc_ref)

    h = h_ref[...]
    a = _dot(h, w1_ref[...])
    u = (a * jax.nn.sigmoid(a)) * _dot(h, w3_ref[...])
    acc_ref[...] += _dot(u.astype(BF16), w2_ref[...])

    @pl.when(f == pl.num_programs(1) - 1)
    def _():
        o_ref[...] = x_ref[...] + m5_ref[...] * _rms(acc_ref[...], gpost_ref[...])


def _moe_dense_kernel(h_ref, g_ref, w1_ref, w3_ref, w2_ref, x_ref, gpost_ref, m5_ref, o_ref, acc_ref):
    e = pl.program_id(1)
    f = pl.program_id(2)

    @pl.when((e == 0) & (f == 0))
    def _():
        acc_ref[...] = jnp.zeros_like(acc_ref)

    h = h_ref[...]
    a = _dot(h, w1_ref[0])
    u = (a * jax.nn.sigmoid(a)) * _dot(h, w3_ref[0])
    lane = lax.broadcasted_iota(jnp.int32, g_ref.shape, 1)
    ge = jnp.sum(jnp.where(lane == e, g_ref[...], 0.0), axis=-1, keepdims=True)
    acc_ref[...] += ge * _dot(u.astype(BF16), w2_ref[0])

    @pl.when((e == pl.num_programs(1) - 1) & (f == pl.num_programs(2) - 1))
    def _():
        o_ref[...] = x_ref[...] + m5_ref[...] * _rms(acc_ref[...], gpost_ref[...])


def _ffn_tiles(t, nb, nt):
    tm = 1024 if (nb * TILE) % 1024 == 0 and (nt * TILE) % 1024 == 0 else TILE
    per = tm // TILE
    row_of_tile = lambda i: jnp.where(i * per < nb, nb, (i * per - nb) // nt)
    return tm, row_of_tile


def _ffn(h, w1, w3, w2, x, gpost, mod, nb, nt):
    t, d = x.shape
    dff = w1.shape[1]
    tm, row_of_tile = _ffn_tiles(t, nb, nt)
    tf = 512
    return pl.pallas_call(
        _ffn_kernel,
        grid=(t // tm, dff // tf),
        in_specs=[pl.BlockSpec((tm, d), lambda i, f: (i, 0)),
                  pl.BlockSpec((d, tf), lambda i, f: (0, f)),
                  pl.BlockSpec((d, tf), lambda i, f: (0, f)),
                  pl.BlockSpec((tf, d), lambda i, f: (f, 0)),
                  pl.BlockSpec((tm, d), lambda i, f: (i, 0)),
                  pl.BlockSpec((1, d), lambda i, f: (0, 0)),
                  pl.BlockSpec((None, None, 1, d), lambda i, f: (row_of_tile(i), 5, 0, 0))],
        out_specs=pl.BlockSpec((tm, d), lambda i, f: (i, 0)),
        out_shape=jax.ShapeDtypeStruct((t, d), F32),
        scratch_shapes=[pltpu.VMEM((tm, d), F32)],
        compiler_params=_params(("arbitrary", "arbitrary")),
        name="ffn_dense",
    )(h, w1, w3, w2, x, gpost, mod)


def _moe_dense(h, gates, w1, w3, w2, x, gpost, mod, nb, nt):
    t, d = x.shape
    ne, _, dff = w1.shape
    tm, row_of_tile = _ffn_tiles(t, nb, nt)
    tf = 512
    return pl.pallas_call(
        _moe_dense_kernel,
        grid=(t // tm, ne, dff // tf),
        in_specs=[pl.BlockSpec((tm, d), lambda i, e, f: (i, 0)),
                  pl.BlockSpec((tm, LANES), lambda i, e, f: (i, 0)),
                  pl.BlockSpec((1, d, tf), lambda i, e, f: (e, 0, f)),
                  pl.BlockSpec((1, d, tf), lambda i, e, f: (e, 0, f)),
                  pl.BlockSpec((1, tf, d), lambda i, e, f: (e, f, 0)),
                  pl.BlockSpec((tm, d), lambda i, e, f: (i, 0)),
                  pl.BlockSpec((1, d), lambda i, e, f: (0, 0)),
                  pl.BlockSpec((None, None, 1, d), lambda i, e, f: (row_of_tile(i), 5, 0, 0))],
        out_specs=pl.BlockSpec((tm, d), lambda i, e, f: (i, 0)),
        out_shape=jax.ShapeDtypeStruct((t, d), F32),
        scratch_shapes=[pltpu.VMEM((tm, d), F32)],
        compiler_params=_params(("arbitrary", "arbitrary", "arbitrary")),
        name="moe_dense",
    )(h, gates, w1, w3, w2, x, gpost, mod)


def _reorder_w_in(w):
    d = w.shape[0]
    a = GLA_QK * 2 + GLA_V * 2
    c0 = a + 2 * GLA_RANK
    g0 = c0 + 3 * NAT_W
    pad = jnp.zeros((d, LANES - 2 * GLA_RANK), w.dtype)
    return jnp.concatenate([w[:, :a], w[:, g0:], w[:, c0:g0], w[:, a:c0], pad], axis=1).astype(BF16)


def _rope_tables(nt):
    t = np.arange(nt * TILE)
    pos = np.stack([t // GRID_W, t % GRID_W], axis=1).astype(np.float32)
    nf = GLA_DK // 4
    inv = (ROPE_BASE ** (-jnp.arange(nf, dtype=F32) / nf))
    ang = jnp.asarray(pos)[:, :, None] * inv[None, None, :]
    cos = jnp.concatenate([jnp.cos(ang), jnp.cos(ang)], axis=-1).reshape(nt * TILE, GLA_DK)
    sin = jnp.concatenate([-jnp.sin(ang), jnp.sin(ang)], axis=-1).reshape(nt * TILE, GLA_DK)
    ident = jnp.ones((TILE, GLA_DK), F32)
    return (jnp.concatenate([ident, cos], axis=0), jnp.concatenate([jnp.zeros_like(ident), sin], axis=0))


def _chunk_matrices():
    i = np.arange(TILE)
    same = (i[:, None] // GLA_CHUNK) == (i[None, :] // GLA_CHUNK)
    lower = same & (i[None, :] <= i[:, None])
    upper = same & (i[None, :] >= i[:, None])
    cum = jnp.asarray(np.stack([lower, upper]).astype(np.float32), BF16)
    return cum, jnp.asarray(same.astype(np.float32), BF16)


def _hi_lo(w):
    hi = w.astype(BF16)
    return hi, (w - hi.astype(F32)).astype(BF16)


def kernel(x, c, ctx, c_ctx, w_mod, b_mod, norm_mix_pre, norm_mix_post, w_in, gla_gate_w2, gla_gate_b, gla_norm,
           nat_rpb, w_branch_gla, w_branch_nat, w_out, norm_ffn_pre, norm_ffn_post, ffn_w1, ffn_w3, ffn_w2,
           moe_router, moe_w1, moe_w3, moe_w2):
    nb, seq, d = x.shape
    depth = w_mod.shape[0]
    assert ctx.shape[1] == TILE and seq % TILE == 0 and d % LANES == 0
    nt = seq // TILE
    mod_rows = -(-(nb + 1) // 8) * 8
    c_rows = jnp.concatenate([c, c_ctx[None], jnp.zeros((mod_rows - nb - 1, d), c.dtype)], axis=0)
    mod_all = _modulation(c_rows, w_mod, b_mod).reshape(depth, mod_rows, 6, 1, d)
    xs = jnp.concatenate([ctx.reshape(nb * TILE, d), x.reshape(nb * seq, d)], axis=0)
    rope_cos, rope_sin = _rope_tables(nt)
    cum_m, tot_m = _chunk_matrices()
    row = lambda v: v.reshape(1, -1)
    for i in range(depth):
        mod = mod_all[i]
        y = _in_projection(xs, row(norm_mix_pre[i]), mod, _reorder_w_in(w_in[i]), nb, nt)
        w2p = jnp.zeros((2, LANES, GLA_QK), F32)
        w2p = w2p.at[0, :GLA_RANK].set(gla_gate_w2[i, 0]).at[1, GLA_RANK:2 * GLA_RANK].set(gla_gate_w2[i, 1])
        w2hi, w2lo = _hi_lo(w2p)
        of, ob = _gla(y, rope_cos, rope_sin, w2hi, w2lo, gla_gate_b[i].reshape(2, 1, GLA_QK), cum_m, tot_m, nb, nt)
        on = _nat(y, _nat_bias_table(nat_rpb[i], nt), nb, nt)
        is_moe = i % 2 == 1
        j = i // 2
        router = None
        if is_moe:
            wr = jnp.zeros((d, LANES), F32).at[:, :N_EXPERTS].set(moe_router[j])
            router = _hi_lo(wr)
        outs = _merge(y, of, ob, on, xs, row(gla_norm[i]), w_branch_gla[i].astype(BF16), w_branch_nat[i].astype(BF16),
                      w_out[i].astype(BF16), row(norm_mix_post[i]), row(norm_ffn_pre[i]), mod, router, nb, nt)
        if is_moe:
            xs, h, gates = outs
            xs = _moe_dense(h, gates, moe_w1[j].astype(BF16), moe_w3[j].astype(BF16), moe_w2[j].astype(BF16),
                            xs, row(norm_ffn_post[i]), mod, nb, nt)
        else:
            xs, h = outs
            xs = _ffn(h, ffn_w1[j].astype(BF16), ffn_w3[j].astype(BF16), ffn_w2[j].astype(BF16),
                      xs, row(norm_ffn_post[i]), mod, nb, nt)
    return xs[nb * TILE:].reshape(nb, seq, d)
```

```python
import functools

import numpy as np
import jax
import jax.numpy as jnp
from jax import lax
from jax.experimental import pallas as pl
from jax.experimental.pallas import tpu as pltpu

EPS = 1e-6
GRID_W = 64
TILE = 256
TILE_ROWS = TILE // GRID_W
GLA_HEADS, GLA_DK, GLA_DV, GLA_RANK, GLA_TAU, GLA_CHUNK = 4, 128, 256, 16, 16.0, 64
NAT_HEADS, NAT_DH = 8, 64
WIN_H, WIN_W = 8, 16
N_EXPERTS = 8
ROPE_BASE = 10000.0
GLA_QK = GLA_HEADS * GLA_DK
GLA_V = GLA_HEADS * GLA_DV
NAT_W = NAT_HEADS * NAT_DH
LANES = 128
COL_GQ, COL_GK, COL_GV, COL_GR, COL_GA, COL_GB = 0, 512, 1024, 2048, 3072, 4096
COL_NQ, COL_NK, COL_NV, COL_CODE = 5120, 5632, 6144, 6656
N_IN = COL_CODE + LANES
VMEM_LIMIT = 56 * 1024 * 1024
BF16 = jnp.bfloat16
F32 = jnp.float32


def _dot(a, b):
    return jnp.dot(a, b, preferred_element_type=F32)


def _dot_nt(a, b):
    return lax.dot_general(a, b, (((1,), (1,)), ((), ())), preferred_element_type=F32)


def _dot_tn(a, b):
    return lax.dot_general(a, b, (((0,), (0,)), ((), ())), preferred_element_type=F32)


def _rms(x, g):
    return x * lax.rsqrt(jnp.mean(x * x, axis=-1, keepdims=True) + EPS) * g


def _split_dot(a, b_hi, b_lo):
    a_hi = a.astype(BF16)
    a_lo = (a - a_hi.astype(F32)).astype(BF16)
    return _dot(a_hi, b_hi) + (_dot(a_lo, b_hi) + _dot(a_hi, b_lo))


def _params(sem):
    return pltpu.CompilerParams(dimension_semantics=sem, vmem_limit_bytes=VMEM_LIMIT)


def _mod_kernel(c_ref, w_ref, b_ref, o_ref):
    c = c_ref[...]
    s = c * jax.nn.sigmoid(c)
    o_ref[0] = _dot(s.astype(BF16), w_ref[0].astype(BF16)) + b_ref[0]


def _modulation(c_rows, w_mod, b_mod):
    depth, d, n = w_mod.shape
    rows = c_rows.shape[0]
    tn = 1536
    return pl.pallas_call(
        _mod_kernel,
        grid=(depth, n // tn),
        in_specs=[pl.BlockSpec((rows, d), lambda l, j: (0, 0)),
                  pl.BlockSpec((1, d, tn), lambda l, j: (l, 0, j)),
                  pl.BlockSpec((1, 1, tn), lambda l, j: (l, 0, j))],
        out_specs=pl.BlockSpec((1, rows, tn), lambda l, j: (l, 0, j)),
        out_shape=jax.ShapeDtypeStruct((depth, rows, n), F32),
        compiler_params=_params(("arbitrary", "arbitrary")),
        name="modulation",
    )(c_rows, w_mod, b_mod.reshape(depth, 1, n))


def _inproj_kernel(x_ref, g_ref, sh_ref, sc_ref, w_ref, o_ref):
    h = _rms(x_ref[...], g_ref[...]) * (1.0 + sc_ref[...]) + sh_ref[...]
    hb = h.astype(BF16)
    n = o_ref.shape[1]
    step = 1024
    for j in range(0, n, step):
        w = min(step, n - j)
        o_ref[:, j:j + w] = _dot(hb, w_ref[:, j:j + w]).astype(o_ref.dtype)


def _in_projection(x, g, mod, w, nb, nt):
    t, d = x.shape
    n = w.shape[1]
    row_of_tile = lambda i: jnp.where(i < nb, nb, (i - nb) // nt)
    dm = mod.shape[-1]
    mspec = lambda comp: pl.BlockSpec((None, None, 1, dm), lambda i: (row_of_tile(i), comp, 0, 0))
    return pl.pallas_call(
        _inproj_kernel,
        grid=(t // TILE,),
        in_specs=[pl.BlockSpec((TILE, d), lambda i: (i, 0)),
                  pl.BlockSpec((1, d), lambda i: (0, 0)),
                  mspec(0), mspec(1),
                  pl.BlockSpec((d, n), lambda i: (0, 0))],
        out_specs=pl.BlockSpec((TILE, n), lambda i: (i, 0)),
        out_shape=jax.ShapeDtypeStruct((t, n), BF16),
        compiler_params=_params(("arbitrary",)),
        name="in_projection",
    )(x, g, mod, mod, w)


def _gla_direction(d, q_ref, k_ref, v_ref, code_ref, cos_ref, sin_ref, w2hi_ref, w2lo_ref, gb_ref,
                   cum_ref, tot_ref, st_ref, o_ref, qe_ref, ke_ref, kd_ref, dec_ref):
    lane = lax.broadcasted_iota(jnp.int32, (TILE, GLA_QK), 1)
    first_half = (lane % 64) < 32
    cos = jnp.concatenate([cos_ref[...]] * GLA_HEADS, axis=1)
    sin = jnp.concatenate([sin_ref[...]] * GLA_HEADS, axis=1)

    def rope(x):
        partner = jnp.where(first_half, pltpu.roll(x, GLA_QK - 32, 1), pltpu.roll(x, 32, 1))
        return x * cos + partner * sin

    z = _dot(code_ref[...], w2hi_ref[d]) + _dot(code_ref[...], w2lo_ref[d]) + gb_ref[d]
    lg = (jnp.minimum(z, 0.0) - jnp.log(1.0 + jnp.exp(-jnp.abs(z)))) * (1.0 / GLA_TAU)
    lg_hi = lg.astype(BF16)
    lg_lo = (lg - lg_hi.astype(F32)).astype(BF16)
    cum = _dot(cum_ref[d], lg_hi) + _dot(cum_ref[d], lg_lo)
    tot = _dot(tot_ref[...], lg_hi) + _dot(tot_ref[...], lg_lo)
    q = rope(q_ref[...].astype(F32)) * (GLA_DK ** -0.5)
    k = rope(k_ref[...].astype(F32))
    qe_ref[...] = (q * jnp.exp(cum)).astype(BF16)
    ke_ref[...] = (k * jnp.exp(-cum)).astype(BF16)
    kd_ref[...] = (k * jnp.exp(tot - cum)).astype(BF16)
    dec_ref[...] = jnp.exp(tot)

    r = lax.broadcasted_iota(jnp.int32, (GLA_CHUNK, GLA_CHUNK), 0)
    s = lax.broadcasted_iota(jnp.int32, (GLA_CHUNK, GLA_CHUNK), 1)
    causal = (s <= r) if d == 0 else (s >= r)
    n_chunks = TILE // GLA_CHUNK

    def chunk(ci, carry):
        c = ci if d == 0 else n_chunks - 1 - ci
        rows = pl.ds(pl.multiple_of(c * GLA_CHUNK, GLA_CHUNK), GLA_CHUNK)
        for h in range(GLA_HEADS):
            kcols = slice(h * GLA_DK, (h + 1) * GLA_DK)
            vcols = slice(h * GLA_DV, (h + 1) * GLA_DV)
            qe = qe_ref[rows, kcols]
            v = v_ref[rows, vcols]
            a = jnp.where(causal, _dot_nt(qe, ke_ref[rows, kcols]), 0.0)
            st = st_ref[d, h]
            o = _dot(a.astype(BF16), v) + _dot_nt(qe, st.astype(BF16))
            o_ref[rows, vcols] = o.astype(o_ref.dtype)
            dec = dec_ref[rows, kcols][0:1, :]
            st_ref[d, h] = st * dec + _dot_tn(v, kd_ref[rows, kcols])
        return carry

    lax.fori_loop(0, n_chunks, chunk, 0)


def _gla_kernel(qf, kf, vf, cf, cosf, sinf, qb, kb, vb, cb, cosb, sinb, w2hi, w2lo, gb, cum, tot,
                of, ob, st_ref, qe_ref, ke_ref, kd_ref, dec_ref):
    @pl.when(pl.program_id(1) == 0)
    def _():
        st_ref[...] = jnp.zeros_like(st_ref)

    _gla_direction(0, qf, kf, vf, cf, cosf, sinf, w2hi, w2lo, gb, cum, tot, st_ref, of,
                   qe_ref, ke_ref, kd_ref, dec_ref)
    _gla_direction(1, qb, kb, vb, cb, cosb, sinb, w2hi, w2lo, gb, cum, tot, st_ref, ob,
                   qe_ref, ke_ref, kd_ref, dec_ref)


def _gla(y, rope_cos, rope_sin, w2hi, w2lo, gbias, cum_m, tot_m, nb, nt):
    t = y.shape[0]
    fwd = lambda b, j: jnp.where(j == 0, b, nb + b * nt + j - 1)
    bwd = lambda b, j: jnp.where(j == 0, b, nb + b * nt + nt - j)
    fwd_loc = lambda b, j: j
    bwd_loc = lambda b, j: jnp.where(j == 0, 0, nt + 1 - j)

    def ysl(width, col, tile):
        return pl.BlockSpec((TILE, width), lambda b, j: (tile(b, j), col // width))

    def direction(tile, loc):
        return [ysl(GLA_QK, COL_GQ, tile), ysl(GLA_QK, COL_GK, tile), ysl(GLA_V, COL_GV, tile),
                ysl(LANES, COL_CODE, tile),
                pl.BlockSpec((TILE, LANES), lambda b, j: (loc(b, j), 0)),
                pl.BlockSpec((TILE, LANES), lambda b, j: (loc(b, j), 0))]

    const = lambda shape: pl.BlockSpec(shape, lambda b, j: (0,) * len(shape))
    in_specs = (direction(fwd, fwd_loc) + direction(bwd, bwd_loc)
                + [const(w2hi.shape), const(w2lo.shape), const(gbias.shape), const(cum_m.shape), const(tot_m.shape)])
    args = [y, y, y, y, rope_cos, rope_sin] * 2 + [w2hi, w2lo, gbias, cum_m, tot_m]
    return pl.pallas_call(
        _gla_kernel,
        grid=(nb, nt + 1),
        in_specs=in_specs,
        out_specs=[pl.BlockSpec((TILE, GLA_V), lambda b, j: (fwd(b, j), 0)),
                   pl.BlockSpec((TILE, GLA_V), lambda b, j: (bwd(b, j), 0))],
        out_shape=[jax.ShapeDtypeStruct((t, GLA_V), BF16)] * 2,
        scratch_shapes=[pltpu.VMEM((2, GLA_HEADS, GLA_DV, GLA_DK), F32),
                        pltpu.VMEM((TILE, GLA_QK), BF16), pltpu.VMEM((TILE, GLA_QK), BF16),
                        pltpu.VMEM((TILE, GLA_QK), BF16), pltpu.VMEM((TILE, GLA_QK), F32)],
        compiler_params=_params(("arbitrary", "arbitrary")),
        name="gla_scan",
    )(*args)


def _nat_kernel(q_ref, k0, k1, k2, kc, v0, v1, v2, vc, bias_ref, o_ref):
    lane = lax.broadcasted_iota(jnp.int32, (TILE, LANES), 1)
    low = lane < NAT_DH
    for p in range(NAT_HEADS // 2):
        cols = slice(p * LANES, (p + 1) * LANES)
        qp = q_ref[:, cols] * (NAT_DH ** -0.5)
        ks = [r[:, cols] for r in (k0, k1, k2, kc)]
        vs = [r[:, cols] for r in (v0, v1, v2, vc)]
        outs = []
        for half in range(2):
            qh = jnp.where(low if half == 0 else ~low, qp, jnp.zeros_like(qp))
            s_lat = jnp.concatenate([_dot_nt(qh, kk) for kk in ks[:3]], axis=1) + bias_ref[2 * p + half]
            s_ctx = _dot_nt(qh, ks[3])
            m = jnp.maximum(jnp.max(s_lat, axis=-1, keepdims=True), jnp.max(s_ctx, axis=-1, keepdims=True))
            e_lat = jnp.exp(s_lat - m)
            e_ctx = jnp.exp(s_ctx - m)
            den = jnp.sum(e_lat, axis=-1, keepdims=True) + jnp.sum(e_ctx, axis=-1, keepdims=True)
            e_lat = e_lat.astype(BF16)
            acc = _dot(e_ctx.astype(BF16), vs[3])
            for i in range(3):
                acc += _dot(e_lat[:, i * TILE:(i + 1) * TILE], vs[i])
            outs.append(acc / den)
        o_ref[:, cols] = jnp.where(low, outs[0], outs[1]).astype(o_ref.dtype)


def _nat(y, bias, nb, nt):
    t = y.shape[0]
    lat = lambda b, tt: nb + b * nt + jnp.clip(tt, 0, nt - 1)
    qtile = lambda b, j: jnp.where(j == 0, b, nb + b * nt + j - 1)
    pattern = lambda b, j: jnp.where(j == 0, 3, jnp.where(j == 1, 0, jnp.where(j == nt, 2, 1)))

    def ysl(col, tile):
        return pl.BlockSpec((TILE, NAT_W), lambda b, j: (tile(b, j), col // NAT_W))

    slab = [lambda b, j: lat(b, j - 2), lambda b, j: lat(b, j - 1), lambda b, j: lat(b, j), lambda b, j: b]
    in_specs = ([ysl(COL_NQ, qtile)] + [ysl(COL_NK, s) for s in slab] + [ysl(COL_NV, s) for s in slab]
                + [pl.BlockSpec((None, NAT_HEADS, TILE, 3 * TILE), lambda b, j: (pattern(b, j), 0, 0, 0))])
    return pl.pallas_call(
        _nat_kernel,
        grid=(nb, nt + 1),
        in_specs=in_specs,
        out_specs=pl.BlockSpec((TILE, NAT_W), lambda b, j: (qtile(b, j), 0)),
        out_shape=jax.ShapeDtypeStruct((t, NAT_W), BF16),
        compiler_params=_params(("arbitrary", "arbitrary")),
        name="nat_attention",
    )(*([y] * 9), bias)


def _nat_bias_table(rpb, nt):
    rows = nt * TILE_ROWS
    kh = min(WIN_H, rows)
    nr, nc = 2 * WIN_H - 1, 2 * WIN_W - 1
    qi = np.arange(TILE_ROWS)
    kj = np.arange(3 * TILE_ROWS)
    qc = np.arange(GRID_W)
    kc = np.arange(GRID_W)
    col_start = np.clip(qc - WIN_W // 2, 0, GRID_W - WIN_W)
    col_ok = (kc[None, :] >= col_start[:, None]) & (kc[None, :] < col_start[:, None] + WIN_W)
    col_off = np.clip(kc[None, :] - qc[:, None], 1 - WIN_W, WIN_W - 1) + (WIN_W - 1)
    col_sel = ((col_off[None] == np.arange(nc)[:, None, None]) & col_ok[None]).astype(np.float32)
    col_mask = jnp.where(jnp.asarray(col_ok), 0.0, -jnp.inf)[None, None, :, None, :]
    tabs = []
    for t in (0, min(1, nt - 1), nt - 1):
        r = TILE_ROWS * t + qi
        row_start = np.clip(r - kh // 2, 0, rows - kh)
        kr = TILE_ROWS * (t - 1) + kj
        tile_of = kr // TILE_ROWS
        valid = (tile_of >= 0) & (tile_of < nt)
        row_ok = (kr[None, :] >= row_start[:, None]) & (kr[None, :] < row_start[:, None] + kh) & valid[None, :]
        row_off = kr[None, :] - r[:, None] + (WIN_H - 1)
        row_sel = ((row_off[None] == np.arange(nr)[:, None, None]) & row_ok[None]).astype(np.float32)
        b = jnp.einsum("hab,aik,bcl->hickl", rpb.astype(F32), row_sel, col_sel, precision=lax.Precision.HIGHEST)
        b = b + col_mask + jnp.where(jnp.asarray(row_ok), 0.0, -jnp.inf)[None, :, None, :, None]
        tabs.append(b.reshape(NAT_HEADS, TILE, 3 * TILE))
    tabs.append(jnp.full_like(tabs[0], -jnp.inf))
    return jnp.stack(tabs)


def _merge_kernel(of_ref, ob_ref, gr_ref, ga_ref, gb_ref, on_ref, x_ref, gain_ref, wbg_ref, wbn_ref, wo_ref,
                  gpost_ref, m2_ref, gpre_ref, m3_ref, m4_ref, *rest, with_router):
    if with_router:
        wrhi_ref, wrlo_ref, x_out, h_out, gate_out, count_out, count_ref = rest
    else:
        x_out, h_out = rest
    o = of_ref[...].astype(F32) + ob_ref[...].astype(F32)
    parts = []
    for h in range(GLA_HEADS):
        oh = o[:, h * GLA_DV:(h + 1) * GLA_DV]
        parts.append(oh * lax.rsqrt(jnp.mean(oh * oh, axis=-1, keepdims=True) + EPS))
    r = gr_ref[...].astype(F32)
    yg = jnp.concatenate(parts, axis=1) * gain_ref[...] * (r * jax.nn.sigmoid(r))
    m = (jax.nn.sigmoid(ga_ref[...].astype(F32)) * _dot(yg.astype(BF16), wbg_ref[...])
         + jax.nn.sigmoid(gb_ref[...].astype(F32)) * _dot(on_ref[...], wbn_ref[...]))
    y = _dot(m.astype(BF16), wo_ref[...])
    x1 = x_ref[...] + m2_ref[...] * _rms(y, gpost_ref[...])
    x_out[...] = x1
    h = _rms(x1, gpre_ref[...]) * (1.0 + m4_ref[...]) + m3_ref[...]
    h_out[...] = h.astype(h_out.dtype)
    if with_router:
        logits = _split_dot(h, wrhi_ref[...], wrlo_ref[...])
        lane = lax.broadcasted_iota(jnp.int32, logits.shape, 1)
        lg = jnp.where(lane < N_EXPERTS, logits, -jnp.inf)
        v1 = jnp.max(lg, axis=-1, keepdims=True)
        i1 = jnp.min(jnp.where(lg == v1, lane, LANES), axis=-1, keepdims=True)
        lg2 = jnp.where(lane == i1, -jnp.inf, lg)
        v2 = jnp.max(lg2, axis=-1, keepdims=True)
        i2 = jnp.min(jnp.where(lg2 == v2, lane, LANES), axis=-1, keepdims=True)
        e2 = jnp.exp(v2 - v1)
        w1 = 1.0 / (1.0 + e2)
        w2 = e2 / (1.0 + e2)

        @pl.when(pl.program_id(0) == 0)
        def _():
            count_ref[...] = jnp.zeros_like(count_ref)

        oh1 = (lane == i1).astype(F32)
        oh2 = (lane == i2).astype(F32)
        rr = lax.broadcasted_iota(jnp.int32, (TILE, TILE), 0)
        ss = lax.broadcasted_iota(jnp.int32, (TILE, TILE), 1)
        before = (ss < rr).astype(BF16)
        carry = count_ref[...]
        cnt1 = jnp.sum(oh1, axis=0, keepdims=True)
        rank1 = jnp.sum(oh1 * (_dot(before, oh1.astype(BF16)) + carry), axis=-1, keepdims=True)
        rank2 = jnp.sum(oh2 * (_dot(before, oh2.astype(BF16)) + (carry + cnt1)), axis=-1, keepdims=True)
        total = carry + cnt1 + jnp.sum(oh2, axis=0, keepdims=True)
        count_ref[...] = total
        count_out[...] = jnp.broadcast_to(total, count_out.shape)
        packed = jnp.zeros_like(logits)
        for ln, val in ((8, i1.astype(F32)), (9, i2.astype(F32)), (10, w1), (11, w2), (12, rank1), (13, rank2)):
            packed = jnp.where(lane == ln, val, packed)
        gate_out[...] = packed


def _merge(y, of, ob, on, x, gain, wbg, wbn, wo, gpost, gpre, mod, router, nb, nt):
    t, d = x.shape
    row_of_tile = lambda i: jnp.where(i < nb, nb, (i - nb) // nt)
    mspec = lambda comp: pl.BlockSpec((None, None, 1, d), lambda i: (row_of_tile(i), comp, 0, 0))
    rowblk = lambda w: pl.BlockSpec((TILE, w), lambda i: (i, 0))
    ycol = lambda w, col: pl.BlockSpec((TILE, w), lambda i: (i, col // w))
    const = lambda a: pl.BlockSpec(a.shape, lambda i: (0,) * a.ndim)
    in_specs = [rowblk(GLA_V), rowblk(GLA_V), ycol(GLA_V, COL_GR), ycol(d, COL_GA), ycol(d, COL_GB), rowblk(NAT_W),
                rowblk(d), const(gain), const(wbg), const(wbn), const(wo), const(gpost), mspec(2), const(gpre),
                mspec(3), mspec(4)]
    args = [of, ob, y, y, y, on, x, gain, wbg, wbn, wo, gpost, mod, gpre, mod, mod]
    out_specs = [rowblk(d), rowblk(d)]
    out_shape = [jax.ShapeDtypeStruct((t, d), F32), jax.ShapeDtypeStruct((t, d), BF16)]
    scratch = []
    if router is not None:
        in_specs += [const(router[0]), const(router[1])]
        args += list(router)
        out_specs += [rowblk(LANES), pl.BlockSpec((8, LANES), lambda i: (0, 0))]
        out_shape += [jax.ShapeDtypeStruct((t, LANES), F32), jax.ShapeDtypeStruct((8, LANES), F32)]
        out_shape[1] = jax.ShapeDtypeStruct((t, d), F32)
        scratch = [pltpu.VMEM((1, LANES), F32)]
    return pl.pallas_call(
        functools.partial(_merge_kernel, with_router=router is not None),
        grid=(t // TILE,),
        in_specs=in_specs,
        out_specs=out_specs,
        out_shape=out_shape,
        scratch_shapes=scratch,
        compiler_params=_params(("arbitrary",)),
        name="merge_router" if router is not None else "merge",
    )(*args)


def _ffn_kernel(h_ref, w1_ref, w3_ref, w2_ref, x_ref, gpost_ref, m5_ref, o_ref, acc_ref):
    f = pl.program_id(1)

    @pl.when(f == 0)
    def _():
        acc_ref[...] = jnp.zeros_like(acc_ref)

    h = h_ref[...]
    a = _dot(h, w1_ref[...])
    u = (a * jax.nn.sigmoid(a)) * _dot(h, w3_ref[...])
    acc_ref[...] += _dot(u.astype(BF16), w2_ref[...])

    @pl.when(f == pl.num_programs(1) - 1)
    def _():
        o_ref[...] = x_ref[...] + m5_ref[...] * _rms(acc_ref[...], gpost_ref[...])


def _dispatch_kernel(pos_ref, h_ref, xs_in, xs_out, sem):
    del xs_in
    t = pl.num_programs(0) * TILE
    base = pl.program_id(0) * TILE

    def copies(r):
        src = h_ref.at[pl.ds(r, 1)]
        return (pltpu.make_async_copy(src, xs_out.at[pl.ds(pos_ref[base + r], 1)], sem),
                pltpu.make_async_copy(src, xs_out.at[pl.ds(pos_ref[t + base + r], 1)], sem))

    def start(r, carry):
        for cp in copies(r):
            cp.start()
        return carry

    def wait(r, carry):
        for cp in copies(r):
            cp.wait()
        return carry

    lax.fori_loop(0, TILE, start, 0, unroll=8)
    lax.fori_loop(0, TILE, wait, 0, unroll=8)


def _moe_ffn_kernel(te_ref, nu_ref, x_ref, w1_ref, w3_ref, w2_ref, o_ref, xb_ref):
    del te_ref
    i = pl.program_id(0)
    f = pl.program_id(1)

    @pl.when(i < nu_ref[0])
    def _():
        @pl.when(f == 0)
        def _():
            xb_ref[...] = x_ref[...].astype(BF16)

        h = xb_ref[...]
        a = _dot(h, w1_ref[0])
        u = (a * jax.nn.sigmoid(a)) * _dot(h, w3_ref[0])
        y = _dot(u.astype(BF16), w2_ref[0])

        @pl.when(f == 0)
        def _():
            o_ref[...] = y

        @pl.when(f > 0)
        def _():
            o_ref[...] += y

    @pl.when((i >= nu_ref[0]) & (f == 0))
    def _():
        o_ref[...] = jnp.zeros_like(o_ref)


def _combine_kernel(pos_ref, ys_hbm, g_ref, x_ref, gpost_ref, m5_ref, o_ref, buf_ref, sem):
    i = pl.program_id(0)
    n = pl.num_programs(0)
    t = n * TILE

    def copies(step, slot, r):
        base = step * TILE
        return (pltpu.make_async_copy(ys_hbm.at[pl.ds(pos_ref[base + r], 1)],
                                      buf_ref.at[slot, 0, pl.ds(r, 1)], sem.at[slot]),
                pltpu.make_async_copy(ys_hbm.at[pl.ds(pos_ref[t + base + r], 1)],
                                      buf_ref.at[slot, 1, pl.ds(r, 1)], sem.at[slot]))

    def issue(step, slot):
        def body(r, carry):
            for cp in copies(step, slot, r):
                cp.start()
            return carry
        lax.fori_loop(0, TILE, body, 0, unroll=8)

    @pl.when(i == 0)
    def _():
        issue(0, 0)

    @pl.when(i + 1 < n)
    def _():
        issue(i + 1, (i + 1) % 2)

    slot = i % 2

    def wait(r, carry):
        for cp in copies(i, slot, r):
            cp.wait()
        return carry

    lax.fori_loop(0, TILE, wait, 0, unroll=8)
    g = g_ref[...]
    y = g[:, 10:11] * buf_ref[slot, 0] + g[:, 11:12] * buf_ref[slot, 1]
    o_ref[...] = x_ref[...] + m5_ref[...] * _rms(y, gpost_ref[...])


def _ffn_tiles(t, nb, nt):
    tm = 1024 if (nb * TILE) % 1024 == 0 and (nt * TILE) % 1024 == 0 else TILE
    per = tm // TILE
    row_of_tile = lambda i: jnp.where(i * per < nb, nb, (i * per - nb) // nt)
    return tm, row_of_tile


def _ffn(h, w1, w3, w2, x, gpost, mod, nb, nt):
    t, d = x.shape
    dff = w1.shape[1]
    tm, row_of_tile = _ffn_tiles(t, nb, nt)
    tf = 512
    return pl.pallas_call(
        _ffn_kernel,
        grid=(t // tm, dff // tf),
        in_specs=[pl.BlockSpec((tm, d), lambda i, f: (i, 0)),
                  pl.BlockSpec((d, tf), lambda i, f: (0, f)),
                  pl.BlockSpec((d, tf), lambda i, f: (0, f)),
                  pl.BlockSpec((tf, d), lambda i, f: (f, 0)),
                  pl.BlockSpec((tm, d), lambda i, f: (i, 0)),
                  pl.BlockSpec((1, d), lambda i, f: (0, 0)),
                  pl.BlockSpec((None, None, 1, d), lambda i, f: (row_of_tile(i), 5, 0, 0))],
        out_specs=pl.BlockSpec((tm, d), lambda i, f: (i, 0)),
        out_shape=jax.ShapeDtypeStruct((t, d), F32),
        scratch_shapes=[pltpu.VMEM((tm, d), F32)],
        compiler_params=_params(("arbitrary", "arbitrary")),
        name="ffn_dense",
    )(h, w1, w3, w2, x, gpost, mod)


MOE_TM = 512
MOE_TF = 896


def _moe_routed(h, pack, counts, w1, w3, w2, x, gpost, mod, nb, nt):
    t, d = x.shape
    ne, _, dff = w1.shape
    tm = MOE_TM
    tf = MOE_TF if dff % MOE_TF == 0 else 512
    n_tiles = -(-(2 * t) // tm) + ne
    p_rows = n_tiles * tm
    cnt = counts[0, :ne].astype(jnp.int32)
    padded = (cnt + tm - 1) // tm * tm
    ends = jnp.cumsum(padded)
    offs = ends - padded
    e1 = pack[:, 8].astype(jnp.int32)
    e2 = pack[:, 9].astype(jnp.int32)
    pos = jnp.concatenate([offs[e1] + pack[:, 12].astype(jnp.int32), offs[e2] + pack[:, 13].astype(jnp.int32)])
    n_used = (ends[-1] // tm).astype(jnp.int32).reshape(1)
    tile_start = jnp.arange(n_tiles, dtype=jnp.int32) * tm
    tile_start = jnp.minimum(tile_start, ends[-1] - tm)
    tile_expert = jnp.minimum(jnp.sum(tile_start[:, None] >= ends[None, :], axis=1), ne - 1).astype(jnp.int32)

    xs = pl.pallas_call(
        _dispatch_kernel,
        grid_spec=pltpu.PrefetchScalarGridSpec(
            num_scalar_prefetch=1, grid=(t // TILE,),
            in_specs=[pl.BlockSpec((TILE, d), lambda i, pos: (i, 0)), pl.BlockSpec(memory_space=pl.ANY)],
            out_specs=pl.BlockSpec(memory_space=pl.ANY),
            scratch_shapes=[pltpu.SemaphoreType.DMA(())]),
        out_shape=jax.ShapeDtypeStruct((p_rows, d), F32),
        input_output_aliases={2: 0},
        compiler_params=_params(("arbitrary",)),
        name="moe_dispatch",
    )(pos, h, jnp.zeros((p_rows, d), F32))

    nf = dff // tf
    row = lambda i, nu: jnp.minimum(i, nu[0] - 1)
    fcol = lambda i, f, nu: jnp.where(i < nu[0], f, nf - 1)
    ys = pl.pallas_call(
        _moe_ffn_kernel,
        grid_spec=pltpu.PrefetchScalarGridSpec(
            num_scalar_prefetch=2, grid=(n_tiles, nf),
            in_specs=[pl.BlockSpec((tm, d), lambda i, f, te, nu: (row(i, nu), 0)),
                      pl.BlockSpec((1, d, tf), lambda i, f, te, nu: (te[i], 0, fcol(i, f, nu))),
                      pl.BlockSpec((1, d, tf), lambda i, f, te, nu: (te[i], 0, fcol(i, f, nu))),
                      pl.BlockSpec((1, tf, d), lambda i, f, te, nu: (te[i], fcol(i, f, nu), 0))],
            out_specs=pl.BlockSpec((tm, d), lambda i, f, te, nu: (i, 0)),
            scratch_shapes=[pltpu.VMEM((tm, d), BF16)]),
        out_shape=jax.ShapeDtypeStruct((p_rows, d), F32),
        compiler_params=_params(("arbitrary", "arbitrary")),
        name="moe_ffn",
    )(tile_expert, n_used, xs, w1, w3, w2)

    row_of_tile = lambda i: jnp.where(i < nb, nb, (i - nb) // nt)
    return pl.pallas_call(
        _combine_kernel,
        grid_spec=pltpu.PrefetchScalarGridSpec(
            num_scalar_prefetch=1, grid=(t // TILE,),
            in_specs=[pl.BlockSpec(memory_space=pl.ANY),
                      pl.BlockSpec((TILE, LANES), lambda i, pos: (i, 0)),
                      pl.BlockSpec((TILE, d), lambda i, pos: (i, 0)),
                      pl.BlockSpec((1, d), lambda i, pos: (0, 0)),
                      pl.BlockSpec((None, None, 1, d), lambda i, pos: (row_of_tile(i), 5, 0, 0))],
            out_specs=pl.BlockSpec((TILE, d), lambda i, pos: (i, 0)),
            scratch_shapes=[pltpu.VMEM((2, 2, TILE, d), F32), pltpu.SemaphoreType.DMA((2,))]),
        out_shape=jax.ShapeDtypeStruct((t, d), F32),
        compiler_params=_params(("arbitrary",)),
        name="moe_combine",
    )(pos, ys, pack, x, gpost, mod)


def _reorder_w_in(w):
    d = w.shape[0]
    a = GLA_QK * 2 + GLA_V * 2
    c0 = a + 2 * GLA_RANK
    g0 = c0 + 3 * NAT_W
    pad = jnp.zeros((d, LANES - 2 * GLA_RANK), w.dtype)
    return jnp.concatenate([w[:, :a], w[:, g0:], w[:, c0:g0], w[:, a:c0], pad], axis=1).astype(BF16)


def _rope_tables(nt):
    t = np.arange(nt * TILE)
    pos = np.stack([t // GRID_W, t % GRID_W], axis=1).astype(np.float32)
    nf = GLA_DK // 4
    inv = (ROPE_BASE ** (-jnp.arange(nf, dtype=F32) / nf))
    ang = jnp.asarray(pos)[:, :, None] * inv[None, None, :]
    cos = jnp.concatenate([jnp.cos(ang), jnp.cos(ang)], axis=-1).reshape(nt * TILE, GLA_DK)
    sin = jnp.concatenate([-jnp.sin(ang), jnp.sin(ang)], axis=-1).reshape(nt * TILE, GLA_DK)
    ident = jnp.ones((TILE, GLA_DK), F32)
    return (jnp.concatenate([ident, cos], axis=0), jnp.concatenate([jnp.zeros_like(ident), sin], axis=0))


def _chunk_matrices():
    i = np.arange(TILE)
    same = (i[:, None] // GLA_CHUNK) == (i[None, :] // GLA_CHUNK)
    lower = same & (i[None, :] <= i[:, None])
    upper = same & (i[None, :] >= i[:, None])
    cum = jnp.asarray(np.stack([lower, upper]).astype(np.float32), BF16)
    return cum, jnp.asarray(same.astype(np.float32), BF16)


def _hi_lo(w):
    hi = w.astype(BF16)
    return hi, (w - hi.astype(F32)).astype(BF16)


def kernel(x, c, ctx, c_ctx, w_mod, b_mod, norm_mix_pre, norm_mix_post, w_in, gla_gate_w2, gla_gate_b, gla_norm,
           nat_rpb, w_branch_gla, w_branch_nat, w_out, norm_ffn_pre, norm_ffn_post, ffn_w1, ffn_w3, ffn_w2,
           moe_router, moe_w1, moe_w3, moe_w2):
    nb, seq, d = x.shape
    depth = w_mod.shape[0]
    assert ctx.shape[1] == TILE and seq % TILE == 0 and d % LANES == 0
    nt = seq // TILE
    mod_rows = -(-(nb + 1) // 8) * 8
    c_rows = jnp.concatenate([c, c_ctx[None], jnp.zeros((mod_rows - nb - 1, d), c.dtype)], axis=0)
    mod_all = _modulation(c_rows, w_mod, b_mod).reshape(depth, mod_rows, 6, 1, d)
    xs = jnp.concatenate([ctx.reshape(nb * TILE, d), x.reshape(nb * seq, d)], axis=0)
    rope_cos, rope_sin = _rope_tables(nt)
    cum_m, tot_m = _chunk_matrices()
    row = lambda v: v.reshape(1, -1)
    for i in range(depth):
        mod = mod_all[i]
        y = _in_projection(xs, row(norm_mix_pre[i]), mod, _reorder_w_in(w_in[i]), nb, nt)
        w2p = jnp.zeros((2, LANES, GLA_QK), F32)
        w2p = w2p.at[0, :GLA_RANK].set(gla_gate_w2[i, 0]).at[1, GLA_RANK:2 * GLA_RANK].set(gla_gate_w2[i, 1])
        w2hi, w2lo = _hi_lo(w2p)
        of, ob = _gla(y, rope_cos, rope_sin, w2hi, w2lo, gla_gate_b[i].reshape(2, 1, GLA_QK), cum_m, tot_m, nb, nt)
        on = _nat(y, _nat_bias_table(nat_rpb[i], nt), nb, nt)
        is_moe = i % 2 == 1
        j = i // 2
        router = None
        if is_moe:
            wr = jnp.zeros((d, LANES), F32).at[:, :N_EXPERTS].set(moe_router[j])
            router = _hi_lo(wr)
        outs = _merge(y, of, ob, on, xs, row(gla_norm[i]), w_branch_gla[i].astype(BF16), w_branch_nat[i].astype(BF16),
                      w_out[i].astype(BF16), row(norm_mix_post[i]), row(norm_ffn_pre[i]), mod, router, nb, nt)
        if is_moe:
            xs, h, pack, counts = outs
            xs = _moe_routed(h, pack, counts, moe_w1[j].astype(BF16), moe_w3[j].astype(BF16),
                             moe_w2[j].astype(BF16), xs, row(norm_ffn_post[i]), mod, nb, nt)
        else:
            xs, h = outs
            xs = _ffn(h, ffn_w1[j].astype(BF16), ffn_w3[j].astype(BF16), ffn_w2[j].astype(BF16),
                      xs, row(norm_ffn_post[i]), mod, nb, nt)
    return xs[nb * TILE:].reshape(nb, seq, d)
```

```python
import functools

import numpy as np
import jax
import jax.numpy as jnp
from jax import lax
from jax.experimental import pallas as pl
from jax.experimental.pallas import tpu as pltpu

EPS = 1e-6
GRID_W = 64
TILE = 256
TILE_ROWS = TILE // GRID_W
GLA_HEADS, GLA_DK, GLA_DV, GLA_RANK, GLA_TAU, GLA_CHUNK = 4, 128, 256, 16, 16.0, 64
NAT_HEADS, NAT_DH = 8, 64
WIN_H, WIN_W = 8, 16
N_EXPERTS = 8
ROPE_BASE = 10000.0
GLA_QK = GLA_HEADS * GLA_DK
GLA_V = GLA_HEADS * GLA_DV
NAT_W = NAT_HEADS * NAT_DH
LANES = 128
COL_GQ, COL_GK, COL_GV, COL_GR, COL_GA, COL_GB = 0, 512, 1024, 2048, 3072, 4096
COL_NQ, COL_NK, COL_NV, COL_CODE = 5120, 5632, 6144, 6656
N_IN = COL_CODE + LANES
VMEM_LIMIT = 56 * 1024 * 1024
BF16 = jnp.bfloat16
F32 = jnp.float32


def _dot(a, b):
    return jnp.dot(a, b, preferred_element_type=F32)


def _dot_nt(a, b):
    return lax.dot_general(a, b, (((1,), (1,)), ((), ())), preferred_element_type=F32)


def _dot_tn(a, b):
    return lax.dot_general(a, b, (((0,), (0,)), ((), ())), preferred_element_type=F32)


def _rms(x, g):
    return x * lax.rsqrt(jnp.mean(x * x, axis=-1, keepdims=True) + EPS) * g


def _split_dot(a, b_hi, b_lo):
    a_hi = a.astype(BF16)
    a_lo = (a - a_hi.astype(F32)).astype(BF16)
    return _dot(a_hi, b_hi) + (_dot(a_lo, b_hi) + _dot(a_hi, b_lo))


def _params(sem):
    return pltpu.CompilerParams(dimension_semantics=sem, vmem_limit_bytes=VMEM_LIMIT)


def _mod_kernel(c_ref, w_ref, b_ref, o_ref):
    c = c_ref[...]
    s = c * jax.nn.sigmoid(c)
    o_ref[0] = _dot(s.astype(BF16), w_ref[0].astype(BF16)) + b_ref[0]


def _modulation(c_rows, w_mod, b_mod):
    depth, d, n = w_mod.shape
    rows = c_rows.shape[0]
    tn = 1536
    return pl.pallas_call(
        _mod_kernel,
        grid=(depth, n // tn),
        in_specs=[pl.BlockSpec((rows, d), lambda l, j: (0, 0)),
                  pl.BlockSpec((1, d, tn), lambda l, j: (l, 0, j)),
                  pl.BlockSpec((1, 1, tn), lambda l, j: (l, 0, j))],
        out_specs=pl.BlockSpec((1, rows, tn), lambda l, j: (l, 0, j)),
        out_shape=jax.ShapeDtypeStruct((depth, rows, n), F32),
        compiler_params=_params(("arbitrary", "arbitrary")),
        name="modulation",
    )(c_rows, w_mod, b_mod.reshape(depth, 1, n))


def _inproj_kernel(x_ref, g_ref, sh_ref, sc_ref, cos_ref, sin_ref, w_ref, o_ref):
    h = _rms(x_ref[...], g_ref[...]) * (1.0 + sc_ref[...]) + sh_ref[...]
    hb = h.astype(BF16)
    n = o_ref.shape[1]
    qk = _dot(hb, w_ref[:, :2 * GLA_QK])
    lane = lax.broadcasted_iota(jnp.int32, qk.shape, 1)
    reps = 2 * GLA_HEADS
    cos = jnp.concatenate([cos_ref[...]] * reps, axis=1)
    sin = jnp.concatenate([sin_ref[...]] * reps, axis=1)
    partner = jnp.where((lane % 64) < 32, pltpu.roll(qk, 2 * GLA_QK - 32, 1), pltpu.roll(qk, 32, 1))
    qk = qk * cos + partner * sin
    qk = jnp.where(lane < GLA_QK, qk * (GLA_DK ** -0.5), qk)
    o_ref[:, :2 * GLA_QK] = qk.astype(o_ref.dtype)
    step = 1024
    for j in range(2 * GLA_QK, n, step):
        w = min(step, n - j)
        o_ref[:, j:j + w] = _dot(hb, w_ref[:, j:j + w]).astype(o_ref.dtype)


def _in_projection(x, g, mod, rope_cos, rope_sin, w, nb, nt):
    t, d = x.shape
    n = w.shape[1]
    row_of_tile = lambda i: jnp.where(i < nb, nb, (i - nb) // nt)
    rope_tile = lambda i: jnp.where(i < nb, 0, 1 + (i - nb) % nt)
    dm = mod.shape[-1]
    mspec = lambda comp: pl.BlockSpec((None, None, 1, dm), lambda i: (row_of_tile(i), comp, 0, 0))
    return pl.pallas_call(
        _inproj_kernel,
        grid=(t // TILE,),
        in_specs=[pl.BlockSpec((TILE, d), lambda i: (i, 0)),
                  pl.BlockSpec((1, d), lambda i: (0, 0)),
                  mspec(0), mspec(1),
                  pl.BlockSpec((TILE, LANES), lambda i: (rope_tile(i), 0)),
                  pl.BlockSpec((TILE, LANES), lambda i: (rope_tile(i), 0)),
                  pl.BlockSpec((d, n), lambda i: (0, 0))],
        out_specs=pl.BlockSpec((TILE, n), lambda i: (i, 0)),
        out_shape=jax.ShapeDtypeStruct((t, n), BF16),
        compiler_params=_params(("arbitrary",)),
        name="in_projection",
    )(x, g, mod, mod, rope_cos, rope_sin, w)


def _gla_prepare(d, q_ref, k_ref, code_ref, w2hi_ref, w2lo_ref, gb_ref, cum_ref, tot_ref, ops_ref):
    z = _dot(code_ref[...], w2hi_ref[d]) + _dot(code_ref[...], w2lo_ref[d]) + gb_ref[d]
    lg = (jnp.minimum(z, 0.0) - jnp.log(1.0 + jnp.exp(-jnp.abs(z)))) * (1.0 / GLA_TAU)
    lg_hi = lg.astype(BF16)
    lg_lo = (lg - lg_hi.astype(F32)).astype(BF16)
    cum = _dot(cum_ref[d], lg_hi) + _dot(cum_ref[d], lg_lo)
    tot = _dot(tot_ref[...], lg_hi) + _dot(tot_ref[...], lg_lo)
    k = k_ref[...].astype(F32)
    qe = q_ref[...].astype(F32) * jnp.exp(cum)
    kd = k * jnp.exp(tot - cum)
    ops_ref[d, OP_QE] = qe.astype(BF16)
    ops_ref[d, OP_KE] = (k * jnp.exp(-cum)).astype(BF16)
    ops_ref[d, OP_KD] = kd.astype(BF16)
    n = TILE // GLA_CHUNK
    order = list(range(n)) if d == 0 else list(range(n - 1, -1, -1))
    tc = [tot[c * GLA_CHUNK:c * GLA_CHUNK + 1, :] for c in order]
    zero = jnp.zeros_like(tc[0])
    for p, c in enumerate(order):
        rows = slice(c * GLA_CHUNK, (c + 1) * GLA_CHUNK)
        before = sum(tc[:p], zero)
        after = sum(tc[p + 1:], zero)
        gap1 = tc[p - 1] if p >= 1 else zero
        gap2 = tc[p - 1] + tc[p - 2] if p >= 2 else zero
        ops_ref[d, OP_Q1, rows] = (qe[rows] * jnp.exp(gap1)).astype(BF16)
        ops_ref[d, OP_Q2, rows] = (qe[rows] * jnp.exp(gap2)).astype(BF16)
        ops_ref[d, OP_QP, rows] = (qe[rows] * jnp.exp(before)).astype(BF16)
        ops_ref[d, OP_KS, rows] = (kd[rows] * jnp.exp(after)).astype(BF16)
    return jnp.exp(sum(tc, zero))


OP_QE, OP_KE, OP_KD, OP_Q1, OP_Q2, OP_QP, OP_KS = range(7)


def _gla_kernel(qf, kf, vf, cf, qb, kb, vb, cb, w2hi, w2lo, gb, cum, tot, of, ob, st_ref, ops_ref):
    @pl.when(pl.program_id(1) == 0)
    def _():
        st_ref[...] = jnp.zeros_like(st_ref)

    decay = [_gla_prepare(0, qf, kf, cf, w2hi, w2lo, gb, cum, tot, ops_ref),
             _gla_prepare(1, qb, kb, cb, w2hi, w2lo, gb, cum, tot, ops_ref)]
    r = lax.broadcasted_iota(jnp.int32, (TILE, TILE), 0)
    s = lax.broadcasted_iota(jnp.int32, (TILE, TILE), 1)
    for d, (v_ref, o_ref) in enumerate(((vf, of), (vb, ob))):
        gap = (r // GLA_CHUNK - s // GLA_CHUNK) * (1 if d == 0 else -1)
        diag = (gap == 0) & ((s <= r) if d == 0 else (s >= r))
        for h in range(GLA_HEADS):
            kcols = slice(h * GLA_DK, (h + 1) * GLA_DK)
            vcols = slice(h * GLA_DV, (h + 1) * GLA_DV)
            op = lambda which: ops_ref[d, which, :, kcols]
            kd = op(OP_KD)
            a = jnp.where(diag, _dot_nt(op(OP_QE), op(OP_KE)),
                          jnp.where(gap == 1, _dot_nt(op(OP_QE), kd),
                                    jnp.where(gap == 2, _dot_nt(op(OP_Q1), kd),
                                              jnp.where(gap == 3, _dot_nt(op(OP_Q2), kd), 0.0))))
            v = v_ref[:, vcols]
            st = st_ref[d, h]
            o = _dot(a.astype(BF16), v) + _dot_nt(op(OP_QP), st.astype(BF16))
            o_ref[:, vcols] = o.astype(o_ref.dtype)
            st_ref[d, h] = st * decay[d][:, kcols] + _dot_tn(v, op(OP_KS))


def _gla(y, w2hi, w2lo, gbias, cum_m, tot_m, nb, nt):
    t = y.shape[0]
    fwd = lambda b, j: jnp.where(j == 0, b, nb + b * nt + j - 1)
    bwd = lambda b, j: jnp.where(j == 0, b, nb + b * nt + nt - j)

    def ysl(width, col, tile):
        return pl.BlockSpec((TILE, width), lambda b, j: (tile(b, j), col // width))

    def direction(tile):
        return [ysl(GLA_QK, COL_GQ, tile), ysl(GLA_QK, COL_GK, tile), ysl(GLA_V, COL_GV, tile),
                ysl(LANES, COL_CODE, tile)]

    const = lambda shape: pl.BlockSpec(shape, lambda b, j: (0,) * len(shape))
    in_specs = (direction(fwd) + direction(bwd)
                + [const(w2hi.shape), const(w2lo.shape), const(gbias.shape), const(cum_m.shape), const(tot_m.shape)])
    args = [y] * 8 + [w2hi, w2lo, gbias, cum_m, tot_m]
    return pl.pallas_call(
        _gla_kernel,
        grid=(nb, nt + 1),
        in_specs=in_specs,
        out_specs=[pl.BlockSpec((TILE, GLA_V), lambda b, j: (fwd(b, j), 0)),
                   pl.BlockSpec((TILE, GLA_V), lambda b, j: (bwd(b, j), 0))],
        out_shape=[jax.ShapeDtypeStruct((t, GLA_V), BF16)] * 2,
        scratch_shapes=[pltpu.VMEM((2, GLA_HEADS, GLA_DV, GLA_DK), F32),
                        pltpu.VMEM((2, 7, TILE, GLA_QK), BF16)],
        compiler_params=_params(("arbitrary", "arbitrary")),
        name="gla_scan",
    )(*args)


def _nat_kernel(q_ref, k0, k1, k2, kc, v0, v1, v2, vc, bias_ref, o_ref):
    lane = lax.broadcasted_iota(jnp.int32, (TILE, LANES), 1)
    low = lane < NAT_DH
    for p in range(NAT_HEADS // 2):
        cols = slice(p * LANES, (p + 1) * LANES)
        qp = q_ref[:, cols] * (NAT_DH ** -0.5)
        ks = [r[:, cols] for r in (k0, k1, k2, kc)]
        vs = [r[:, cols] for r in (v0, v1, v2, vc)]
        outs = []
        for half in range(2):
            qh = jnp.where(low if half == 0 else ~low, qp, jnp.zeros_like(qp))
            s_lat = jnp.concatenate([_dot_nt(qh, kk) for kk in ks[:3]], axis=1) + bias_ref[2 * p + half]
            s_ctx = _dot_nt(qh, ks[3])
            m = jnp.maximum(jnp.max(s_lat, axis=-1, keepdims=True), jnp.max(s_ctx, axis=-1, keepdims=True))
            e_lat = jnp.exp(s_lat - m)
            e_ctx = jnp.exp(s_ctx - m)
            den = jnp.sum(e_lat, axis=-1, keepdims=True) + jnp.sum(e_ctx, axis=-1, keepdims=True)
            e_lat = e_lat.astype(BF16)
            acc = _dot(e_ctx.astype(BF16), vs[3])
            for i in range(3):
                acc += _dot(e_lat[:, i * TILE:(i + 1) * TILE], vs[i])
            outs.append(acc / den)
        o_ref[:, cols] = jnp.where(low, outs[0], outs[1]).astype(o_ref.dtype)


def _nat(y, bias, nb, nt):
    t = y.shape[0]
    lat = lambda b, tt: nb + b * nt + jnp.clip(tt, 0, nt - 1)
    qtile = lambda b, j: jnp.where(j == 0, b, nb + b * nt + j - 1)
    pattern = lambda b, j: jnp.where(j == 0, 3, jnp.where(j == 1, 0, jnp.where(j == nt, 2, 1)))

    def ysl(col, tile):
        return pl.BlockSpec((TILE, NAT_W), lambda b, j: (tile(b, j), col // NAT_W))

    slab = [lambda b, j: lat(b, j - 2), lambda b, j: lat(b, j - 1), lambda b, j: lat(b, j), lambda b, j: b]
    in_specs = ([ysl(COL_NQ, qtile)] + [ysl(COL_NK, s) for s in slab] + [ysl(COL_NV, s) for s in slab]
                + [pl.BlockSpec((None, NAT_HEADS, TILE, 3 * TILE), lambda b, j: (pattern(b, j), 0, 0, 0))])
    return pl.pallas_call(
        _nat_kernel,
        grid=(nb, nt + 1),
        in_specs=in_specs,
        out_specs=pl.BlockSpec((TILE, NAT_W), lambda b, j: (qtile(b, j), 0)),
        out_shape=jax.ShapeDtypeStruct((t, NAT_W), BF16),
        compiler_params=_params(("arbitrary", "arbitrary")),
        name="nat_attention",
    )(*([y] * 9), bias)


def _nat_bias_table(rpb, nt):
    rows = nt * TILE_ROWS
    kh = min(WIN_H, rows)
    nr, nc = 2 * WIN_H - 1, 2 * WIN_W - 1
    qi = np.arange(TILE_ROWS)
    kj = np.arange(3 * TILE_ROWS)
    qc = np.arange(GRID_W)
    kc = np.arange(GRID_W)
    col_start = np.clip(qc - WIN_W // 2, 0, GRID_W - WIN_W)
    col_ok = (kc[None, :] >= col_start[:, None]) & (kc[None, :] < col_start[:, None] + WIN_W)
    col_off = np.clip(kc[None, :] - qc[:, None], 1 - WIN_W, WIN_W - 1) + (WIN_W - 1)
    col_sel = ((col_off[None] == np.arange(nc)[:, None, None]) & col_ok[None]).astype(np.float32)
    col_mask = jnp.where(jnp.asarray(col_ok), 0.0, -jnp.inf)[None, None, :, None, :]
    tabs = []
    for t in (0, min(1, nt - 1), nt - 1):
        r = TILE_ROWS * t + qi
        row_start = np.clip(r - kh // 2, 0, rows - kh)
        kr = TILE_ROWS * (t - 1) + kj
        tile_of = kr // TILE_ROWS
        valid = (tile_of >= 0) & (tile_of < nt)
        row_ok = (kr[None, :] >= row_start[:, None]) & (kr[None, :] < row_start[:, None] + kh) & valid[None, :]
        row_off = kr[None, :] - r[:, None] + (WIN_H - 1)
        row_sel = ((row_off[None] == np.arange(nr)[:, None, None]) & row_ok[None]).astype(np.float32)
        b = jnp.einsum("hab,aik,bcl->hickl", rpb.astype(F32), row_sel, col_sel, precision=lax.Precision.HIGHEST)
        b = b + col_mask + jnp.where(jnp.asarray(row_ok), 0.0, -jnp.inf)[None, :, None, :, None]
        tabs.append(b.reshape(NAT_HEADS, TILE, 3 * TILE))
    tabs.append(jnp.full_like(tabs[0], -jnp.inf))
    return jnp.stack(tabs)


def _merge_kernel(of_ref, ob_ref, gr_ref, ga_ref, gb_ref, on_ref, x_ref, gain_ref, wbg_ref, wbn_ref, wo_ref,
                  gpost_ref, m2_ref, gpre_ref, m3_ref, m4_ref, *rest, with_router):
    if with_router:
        wrhi_ref, wrlo_ref, x_out, h_out, gate_out, count_out, count_ref = rest
    else:
        x_out, h_out = rest
    o = of_ref[...].astype(F32) + ob_ref[...].astype(F32)
    parts = []
    for h in range(GLA_HEADS):
        oh = o[:, h * GLA_DV:(h + 1) * GLA_DV]
        parts.append(oh * lax.rsqrt(jnp.mean(oh * oh, axis=-1, keepdims=True) + EPS))
    r = gr_ref[...].astype(F32)
    yg = jnp.concatenate(parts, axis=1) * gain_ref[...] * (r * jax.nn.sigmoid(r))
    m = (jax.nn.sigmoid(ga_ref[...].astype(F32)) * _dot(yg.astype(BF16), wbg_ref[...])
         + jax.nn.sigmoid(gb_ref[...].astype(F32)) * _dot(on_ref[...], wbn_ref[...]))
    y = _dot(m.astype(BF16), wo_ref[...])
    x1 = x_ref[...] + m2_ref[...] * _rms(y, gpost_ref[...])
    x_out[...] = x1
    h = _rms(x1, gpre_ref[...]) * (1.0 + m4_ref[...]) + m3_ref[...]
    h_out[...] = h.astype(h_out.dtype)
    if with_router:
        logits = _split_dot(h, wrhi_ref[...], wrlo_ref[...])
        lane = lax.broadcasted_iota(jnp.int32, logits.shape, 1)
        lg = jnp.where(lane < N_EXPERTS, logits, -jnp.inf)
        v1 = jnp.max(lg, axis=-1, keepdims=True)
        i1 = jnp.min(jnp.where(lg == v1, lane, LANES), axis=-1, keepdims=True)
        lg2 = jnp.where(lane == i1, -jnp.inf, lg)
        v2 = jnp.max(lg2, axis=-1, keepdims=True)
        i2 = jnp.min(jnp.where(lg2 == v2, lane, LANES), axis=-1, keepdims=True)
        e2 = jnp.exp(v2 - v1)
        w1 = 1.0 / (1.0 + e2)
        w2 = e2 / (1.0 + e2)

        @pl.when(pl.program_id(0) == 0)
        def _():
            count_ref[...] = jnp.zeros_like(count_ref)

        oh1 = (lane == i1).astype(F32)
        oh2 = (lane == i2).astype(F32)
        rr = lax.broadcasted_iota(jnp.int32, (TILE, TILE), 0)
        ss = lax.broadcasted_iota(jnp.int32, (TILE, TILE), 1)
        before = (ss < rr).astype(BF16)
        carry = count_ref[...]
        cnt1 = jnp.sum(oh1, axis=0, keepdims=True)
        rank1 = jnp.sum(oh1 * (_dot(before, oh1.astype(BF16)) + carry), axis=-1, keepdims=True)
        rank2 = jnp.sum(oh2 * (_dot(before, oh2.astype(BF16)) + (carry + cnt1)), axis=-1, keepdims=True)
        total = carry + cnt1 + jnp.sum(oh2, axis=0, keepdims=True)
        count_ref[...] = total
        count_out[...] = jnp.broadcast_to(total, count_out.shape)
        packed = jnp.zeros_like(logits)
        for ln, val in ((8, i1.astype(F32)), (9, i2.astype(F32)), (10, w1), (11, w2), (12, rank1), (13, rank2)):
            packed = jnp.where(lane == ln, val, packed)
        gate_out[...] = packed


def _merge(y, of, ob, on, x, gain, wbg, wbn, wo, gpost, gpre, mod, router, nb, nt):
    t, d = x.shape
    row_of_tile = lambda i: jnp.where(i < nb, nb, (i - nb) // nt)
    mspec = lambda comp: pl.BlockSpec((None, None, 1, d), lambda i: (row_of_tile(i), comp, 0, 0))
    rowblk = lambda w: pl.BlockSpec((TILE, w), lambda i: (i, 0))
    ycol = lambda w, col: pl.BlockSpec((TILE, w), lambda i: (i, col // w))
    const = lambda a: pl.BlockSpec(a.shape, lambda i: (0,) * a.ndim)
    in_specs = [rowblk(GLA_V), rowblk(GLA_V), ycol(GLA_V, COL_GR), ycol(d, COL_GA), ycol(d, COL_GB), rowblk(NAT_W),
                rowblk(d), const(gain), const(wbg), const(wbn), const(wo), const(gpost), mspec(2), const(gpre),
                mspec(3), mspec(4)]
    args = [of, ob, y, y, y, on, x, gain, wbg, wbn, wo, gpost, mod, gpre, mod, mod]
    out_specs = [rowblk(d), rowblk(d)]
    out_shape = [jax.ShapeDtypeStruct((t, d), F32), jax.ShapeDtypeStruct((t, d), BF16)]
    scratch = []
    if router is not None:
        in_specs += [const(router[0]), const(router[1])]
        args += list(router)
        out_specs += [rowblk(LANES), pl.BlockSpec((8, LANES), lambda i: (0, 0))]
        out_shape += [jax.ShapeDtypeStruct((t, LANES), F32), jax.ShapeDtypeStruct((8, LANES), F32)]
        out_shape[1] = jax.ShapeDtypeStruct((t, d), F32)
        scratch = [pltpu.VMEM((1, LANES), F32)]
    return pl.pallas_call(
        functools.partial(_merge_kernel, with_router=router is not None),
        grid=(t // TILE,),
        in_specs=in_specs,
        out_specs=out_specs,
        out_shape=out_shape,
        scratch_shapes=scratch,
        compiler_params=_params(("arbitrary",)),
        name="merge_router" if router is not None else "merge",
    )(*args)


def _ffn_kernel(h_ref, w1_ref, w3_ref, w2_ref, x_ref, gpost_ref, m5_ref, o_ref, acc_ref):
    f = pl.program_id(1)

    @pl.when(f == 0)
    def _():
        acc_ref[...] = jnp.zeros_like(acc_ref)

    h = h_ref[...]
    a = _dot(h, w1_ref[...])
    u = (a * jax.nn.sigmoid(a)) * _dot(h, w3_ref[...])
    acc_ref[...] += _dot(u.astype(BF16), w2_ref[...])

    @pl.when(f == pl.num_programs(1) - 1)
    def _():
        o_ref[...] = x_ref[...] + m5_ref[...] * _rms(acc_ref[...], gpost_ref[...])


def _moe_ffn_kernel(te_ref, nu_ref, src_ref, h_hbm, w1_ref, w3_ref, w2_ref, o_ref, xg_ref, xb_ref, sem, *, nf):
    del te_ref
    i = pl.program_id(0)
    f = pl.program_id(1)
    tm = xb_ref.shape[0]
    per = tm // nf
    nu = nu_ref[0]

    def row_copy(tile, slot, r):
        return pltpu.make_async_copy(h_hbm.at[pl.ds(src_ref[tile * tm + r], 1)], xg_ref.at[slot, pl.ds(r, 1)],
                                     sem.at[slot])

    @pl.when((i == 0) & (f == 0))
    def _():
        def body(r, carry):
            row_copy(0, 0, r).start()
            return carry
        lax.fori_loop(0, tm, body, 0, unroll=8)

    @pl.when((f == 0) & (i <= nu))
    def _():
        slot = i % 2
        pltpu.make_async_copy(h_hbm.at[pl.ds(0, tm)], xg_ref.at[slot], sem.at[slot]).wait()

    @pl.when(i < nu)
    def _():
        @pl.when(f == 0)
        def _():
            xb_ref[...] = xg_ref[i % 2].astype(BF16)

        for r in range(per):
            row_copy(i + 1, (i + 1) % 2, f * per + r).start()
        h = xb_ref[...]
        a = _dot(h, w1_ref[0])
        u = (a * jax.nn.sigmoid(a)) * _dot(h, w3_ref[0])
        y = _dot(u.astype(BF16), w2_ref[0])

        @pl.when(f == 0)
        def _():
            o_ref[...] = y

        @pl.when(f > 0)
        def _():
            o_ref[...] += y

    @pl.when((i >= nu_ref[0]) & (f == 0))
    def _():
        o_ref[...] = jnp.zeros_like(o_ref)


def _combine_kernel(pos_ref, ys_hbm, g_ref, x_ref, gpost_ref, m5_ref, o_ref, buf_ref, sem):
    i = pl.program_id(0)
    n = pl.num_programs(0)
    t = n * TILE

    def copies(step, slot, r):
        base = step * TILE
        return (pltpu.make_async_copy(ys_hbm.at[pl.ds(pos_ref[base + r], 1)],
                                      buf_ref.at[slot, 0, pl.ds(r, 1)], sem.at[slot]),
                pltpu.make_async_copy(ys_hbm.at[pl.ds(pos_ref[t + base + r], 1)],
                                      buf_ref.at[slot, 1, pl.ds(r, 1)], sem.at[slot]))

    def issue(step, slot):
        def body(r, carry):
            for cp in copies(step, slot, r):
                cp.start()
            return carry
        lax.fori_loop(0, TILE, body, 0, unroll=8)

    @pl.when(i == 0)
    def _():
        issue(0, 0)

    @pl.when(i + 1 < n)
    def _():
        issue(i + 1, (i + 1) % 2)

    slot = i % 2

    def wait(r, carry):
        for cp in copies(i, slot, r):
            cp.wait()
        return carry

    lax.fori_loop(0, TILE, wait, 0, unroll=8)
    g = g_ref[...]
    y = g[:, 10:11] * buf_ref[slot, 0] + g[:, 11:12] * buf_ref[slot, 1]
    o_ref[...] = x_ref[...] + m5_ref[...] * _rms(y, gpost_ref[...])


def _ffn_tiles(t, nb, nt):
    tm = 1024 if (nb * TILE) % 1024 == 0 and (nt * TILE) % 1024 == 0 else TILE
    per = tm // TILE
    row_of_tile = lambda i: jnp.where(i * per < nb, nb, (i * per - nb) // nt)
    return tm, row_of_tile


def _ffn(h, w1, w3, w2, x, gpost, mod, nb, nt):
    t, d = x.shape
    dff = w1.shape[1]
    tm, row_of_tile = _ffn_tiles(t, nb, nt)
    tf = 512
    return pl.pallas_call(
        _ffn_kernel,
        grid=(t // tm, dff // tf),
        in_specs=[pl.BlockSpec((tm, d), lambda i, f: (i, 0)),
                  pl.BlockSpec((d, tf), lambda i, f: (0, f)),
                  pl.BlockSpec((d, tf), lambda i, f: (0, f)),
                  pl.BlockSpec((tf, d), lambda i, f: (f, 0)),
                  pl.BlockSpec((tm, d), lambda i, f: (i, 0)),
                  pl.BlockSpec((1, d), lambda i, f: (0, 0)),
                  pl.BlockSpec((None, None, 1, d), lambda i, f: (row_of_tile(i), 5, 0, 0))],
        out_specs=pl.BlockSpec((tm, d), lambda i, f: (i, 0)),
        out_shape=jax.ShapeDtypeStruct((t, d), F32),
        scratch_shapes=[pltpu.VMEM((tm, d), F32)],
        compiler_params=_params(("arbitrary", "arbitrary")),
        name="ffn_dense",
    )(h, w1, w3, w2, x, gpost, mod)


MOE_TM = 512
MOE_TF = 896


def _moe_routed(h, pack, counts, w1, w3, w2, x, gpost, mod, nb, nt):
    t, d = x.shape
    ne, _, dff = w1.shape
    tm = MOE_TM
    tf = MOE_TF if dff % MOE_TF == 0 else 512
    n_tiles = -(-(2 * t) // tm) + ne
    p_rows = n_tiles * tm
    cnt = counts[0, :ne].astype(jnp.int32)
    padded = (cnt + tm - 1) // tm * tm
    ends = jnp.cumsum(padded)
    offs = ends - padded
    e1 = pack[:, 8].astype(jnp.int32)
    e2 = pack[:, 9].astype(jnp.int32)
    pos = jnp.concatenate([offs[e1] + pack[:, 12].astype(jnp.int32), offs[e2] + pack[:, 13].astype(jnp.int32)])
    n_used = (ends[-1] // tm).astype(jnp.int32).reshape(1)
    tile_start = jnp.arange(n_tiles, dtype=jnp.int32) * tm
    tile_start = jnp.minimum(tile_start, ends[-1] - tm)
    tile_expert = jnp.minimum(jnp.sum(tile_start[:, None] >= ends[None, :], axis=1), ne - 1).astype(jnp.int32)

    tok = jnp.arange(t, dtype=jnp.int32)
    src = jnp.zeros((p_rows,), jnp.int32).at[pos].set(jnp.concatenate([tok, tok]))

    nf = dff // tf
    assert tm % nf == 0
    fcol = lambda i, f, nu: jnp.where(i < nu[0], f, nf - 1)
    ys = pl.pallas_call(
        functools.partial(_moe_ffn_kernel, nf=nf),
        grid_spec=pltpu.PrefetchScalarGridSpec(
            num_scalar_prefetch=3, grid=(n_tiles, nf),
            in_specs=[pl.BlockSpec(memory_space=pl.ANY),
                      pl.BlockSpec((1, d, tf), lambda i, f, te, nu, src: (te[i], 0, fcol(i, f, nu))),
                      pl.BlockSpec((1, d, tf), lambda i, f, te, nu, src: (te[i], 0, fcol(i, f, nu))),
                      pl.BlockSpec((1, tf, d), lambda i, f, te, nu, src: (te[i], fcol(i, f, nu), 0))],
            out_specs=pl.BlockSpec((tm, d), lambda i, f, te, nu, src: (i, 0)),
            scratch_shapes=[pltpu.VMEM((2, tm, d), F32), pltpu.VMEM((tm, d), BF16), pltpu.SemaphoreType.DMA((2,))]),
        out_shape=jax.ShapeDtypeStruct((p_rows, d), F32),
        compiler_params=_params(("arbitrary", "arbitrary")),
        name="moe_ffn",
    )(tile_expert, n_used, src, h, w1, w3, w2)

    row_of_tile = lambda i: jnp.where(i < nb, nb, (i - nb) // nt)
    return pl.pallas_call(
        _combine_kernel,
        grid_spec=pltpu.PrefetchScalarGridSpec(
            num_scalar_prefetch=1, grid=(t // TILE,),
            in_specs=[pl.BlockSpec(memory_space=pl.ANY),
                      pl.BlockSpec((TILE, LANES), lambda i, pos: (i, 0)),
                      pl.BlockSpec((TILE, d), lambda i, pos: (i, 0)),
                      pl.BlockSpec((1, d), lambda i, pos: (0, 0)),
                      pl.BlockSpec((None, None, 1, d), lambda i, pos: (row_of_tile(i), 5, 0, 0))],
            out_specs=pl.BlockSpec((TILE, d), lambda i, pos: (i, 0)),
            scratch_shapes=[pltpu.VMEM((2, 2, TILE, d), F32), pltpu.SemaphoreType.DMA((2,))]),
        out_shape=jax.ShapeDtypeStruct((t, d), F32),
        compiler_params=_params(("arbitrary",)),
        name="moe_combine",
    )(pos, ys, pack, x, gpost, mod)


def _reorder_w_in(w):
    d = w.shape[0]
    a = GLA_QK * 2 + GLA_V * 2
    c0 = a + 2 * GLA_RANK
    g0 = c0 + 3 * NAT_W
    pad = jnp.zeros((d, LANES - 2 * GLA_RANK), w.dtype)
    return jnp.concatenate([w[:, :a], w[:, g0:], w[:, c0:g0], w[:, a:c0], pad], axis=1).astype(BF16)


def _rope_tables(nt):
    t = np.arange(nt * TILE)
    pos = np.stack([t // GRID_W, t % GRID_W], axis=1).astype(np.float32)
    nf = GLA_DK // 4
    inv = (ROPE_BASE ** (-jnp.arange(nf, dtype=F32) / nf))
    ang = jnp.asarray(pos)[:, :, None] * inv[None, None, :]
    cos = jnp.concatenate([jnp.cos(ang), jnp.cos(ang)], axis=-1).reshape(nt * TILE, GLA_DK)
    sin = jnp.concatenate([-jnp.sin(ang), jnp.sin(ang)], axis=-1).reshape(nt * TILE, GLA_DK)
    ident = jnp.ones((TILE, GLA_DK), F32)
    return (jnp.concatenate([ident, cos], axis=0), jnp.concatenate([jnp.zeros_like(ident), sin], axis=0))


def _chunk_matrices():
    i = np.arange(TILE)
    same = (i[:, None] // GLA_CHUNK) == (i[None, :] // GLA_CHUNK)
    lower = same & (i[None, :] <= i[:, None])
    upper = same & (i[None, :] >= i[:, None])
    cum = jnp.asarray(np.stack([lower, upper]).astype(np.float32), BF16)
    return cum, jnp.asarray(same.astype(np.float32), BF16)


def _hi_lo(w):
    hi = w.astype(BF16)
    return hi, (w - hi.astype(F32)).astype(BF16)


def kernel(x, c, ctx, c_ctx, w_mod, b_mod, norm_mix_pre, norm_mix_post, w_in, gla_gate_w2, gla_gate_b, gla_norm,
           nat_rpb, w_branch_gla, w_branch_nat, w_out, norm_ffn_pre, norm_ffn_post, ffn_w1, ffn_w3, ffn_w2,
           moe_router, moe_w1, moe_w3, moe_w2):
    nb, seq, d = x.shape
    depth = w_mod.shape[0]
    assert ctx.shape[1] == TILE and seq % TILE == 0 and d % LANES == 0
    nt = seq // TILE
    mod_rows = -(-(nb + 1) // 8) * 8
    c_rows = jnp.concatenate([c, c_ctx[None], jnp.zeros((mod_rows - nb - 1, d), c.dtype)], axis=0)
    mod_all = _modulation(c_rows, w_mod, b_mod).reshape(depth, mod_rows, 6, 1, d)
    xs = jnp.concatenate([ctx.reshape(nb * TILE, d), x.reshape(nb * seq, d)], axis=0)
    rope_cos, rope_sin = _rope_tables(nt)
    cum_m, tot_m = _chunk_matrices()
    row = lambda v: v.reshape(1, -1)
    for i in range(depth):
        mod = mod_all[i]
        y = _in_projection(xs, row(norm_mix_pre[i]), mod, rope_cos, rope_sin, _reorder_w_in(w_in[i]), nb, nt)
        w2p = jnp.zeros((2, LANES, GLA_QK), F32)
        w2p = w2p.at[0, :GLA_RANK].set(gla_gate_w2[i, 0]).at[1, GLA_RANK:2 * GLA_RANK].set(gla_gate_w2[i, 1])
        w2hi, w2lo = _hi_lo(w2p)
        of, ob = _gla(y, w2hi, w2lo, gla_gate_b[i].reshape(2, 1, GLA_QK), cum_m, tot_m, nb, nt)
        on = _nat(y, _nat_bias_table(nat_rpb[i], nt), nb, nt)
        is_moe = i % 2 == 1
        j = i // 2
        router = None
        if is_moe:
            wr = jnp.zeros((d, LANES), F32).at[:, :N_EXPERTS].set(moe_router[j])
            router = _hi_lo(wr)
        outs = _merge(y, of, ob, on, xs, row(gla_norm[i]), w_branch_gla[i].astype(BF16), w_branch_nat[i].astype(BF16),
                      w_out[i].astype(BF16), row(norm_mix_post[i]), row(norm_ffn_pre[i]), mod, router, nb, nt)
        if is_moe:
            xs, h, pack, counts = outs
            xs = _moe_routed(h, pack, counts, moe_w1[j].astype(BF16), moe_w3[j].astype(BF16),
                             moe_w2[j].astype(BF16), xs, row(norm_ffn_post[i]), mod, nb, nt)
        else:
            xs, h = outs
            xs = _ffn(h, ffn_w1[j].astype(BF16), ffn_w3[j].astype(BF16), ffn_w2[j].astype(BF16),
                      xs, row(norm_ffn_post[i]), mod, nb, nt)
    return xs[nb * TILE:].reshape(nb, seq, d)
```

```python
import functools

import numpy as np
import jax
import jax.numpy as jnp
from jax import lax
from jax.experimental import pallas as pl
from jax.experimental.pallas import tpu as pltpu

EPS = 1e-6
GRID_W = 64
TILE = 256
TILE_ROWS = TILE // GRID_W
GLA_HEADS, GLA_DK, GLA_DV, GLA_RANK, GLA_TAU, GLA_CHUNK = 4, 128, 256, 16, 16.0, 64
NAT_HEADS, NAT_DH = 8, 64
WIN_H, WIN_W = 8, 16
N_EXPERTS = 8
ROPE_BASE = 10000.0
GLA_QK = GLA_HEADS * GLA_DK
GLA_V = GLA_HEADS * GLA_DV
NAT_W = NAT_HEADS * NAT_DH
LANES = 128
COL_GQ, COL_GK, COL_GV, COL_GR, COL_GA, COL_GB = 0, 512, 1024, 2048, 3072, 4096
COL_NQ, COL_NK, COL_NV, COL_CODE = 5120, 5632, 6144, 6656
N_IN = COL_CODE + LANES
VMEM_LIMIT = 56 * 1024 * 1024
BF16 = jnp.bfloat16
F32 = jnp.float32


def _dot(a, b):
    return jnp.dot(a, b, preferred_element_type=F32)


def _dot_nt(a, b):
    return lax.dot_general(a, b, (((1,), (1,)), ((), ())), preferred_element_type=F32)


def _dot_tn(a, b):
    return lax.dot_general(a, b, (((0,), (0,)), ((), ())), preferred_element_type=F32)


def _rms(x, g):
    return x * lax.rsqrt(jnp.mean(x * x, axis=-1, keepdims=True) + EPS) * g


def _split_dot(a, b_hi, b_lo):
    a_hi = a.astype(BF16)
    a_lo = (a - a_hi.astype(F32)).astype(BF16)
    return _dot(a_hi, b_hi) + (_dot(a_lo, b_hi) + _dot(a_hi, b_lo))


def _params(sem):
    return pltpu.CompilerParams(dimension_semantics=sem, vmem_limit_bytes=VMEM_LIMIT)


def _mod_kernel(c_ref, w_ref, b_ref, o_ref):
    c = c_ref[...]
    s = c * jax.nn.sigmoid(c)
    o_ref[0] = _dot(s.astype(BF16), w_ref[0].astype(BF16)) + b_ref[0]


def _modulation(c_rows, w_mod, b_mod):
    depth, d, n = w_mod.shape
    rows = c_rows.shape[0]
    tn = 1536
    return pl.pallas_call(
        _mod_kernel,
        grid=(depth, n // tn),
        in_specs=[pl.BlockSpec((rows, d), lambda l, j: (0, 0)),
                  pl.BlockSpec((1, d, tn), lambda l, j: (l, 0, j)),
                  pl.BlockSpec((1, 1, tn), lambda l, j: (l, 0, j))],
        out_specs=pl.BlockSpec((1, rows, tn), lambda l, j: (l, 0, j)),
        out_shape=jax.ShapeDtypeStruct((depth, rows, n), F32),
        compiler_params=_params(("arbitrary", "arbitrary")),
        name="modulation",
    )(c_rows, w_mod, b_mod.reshape(depth, 1, n))


def _inproj_kernel(x_ref, g_ref, sh_ref, sc_ref, cos_ref, sin_ref, w_ref, o_ref):
    h = _rms(x_ref[...], g_ref[...]) * (1.0 + sc_ref[...]) + sh_ref[...]
    hb = h.astype(BF16)
    n = o_ref.shape[1]
    qk = _dot(hb, w_ref[:, :2 * GLA_QK])
    lane = lax.broadcasted_iota(jnp.int32, qk.shape, 1)
    reps = 2 * GLA_HEADS
    cos = jnp.concatenate([cos_ref[...]] * reps, axis=1)
    sin = jnp.concatenate([sin_ref[...]] * reps, axis=1)
    partner = jnp.where((lane % 64) < 32, pltpu.roll(qk, 2 * GLA_QK - 32, 1), pltpu.roll(qk, 32, 1))
    qk = qk * cos + partner * sin
    qk = jnp.where(lane < GLA_QK, qk * (GLA_DK ** -0.5), qk)
    o_ref[:, :2 * GLA_QK] = qk.astype(o_ref.dtype)
    step = 1024
    for j in range(2 * GLA_QK, n, step):
        w = min(step, n - j)
        o_ref[:, j:j + w] = _dot(hb, w_ref[:, j:j + w]).astype(o_ref.dtype)


def _in_projection(x, g, mod, rope_cos, rope_sin, w, layer, nb, nt):
    t, d = x.shape
    n = w.shape[2]
    row_of_tile = lambda i: jnp.where(i < nb, nb, (i - nb) // nt)
    rope_tile = lambda i: jnp.where(i < nb, 0, 1 + (i - nb) % nt)
    dm = mod.shape[-1]
    mspec = lambda comp: pl.BlockSpec((None, None, 1, dm), lambda i: (row_of_tile(i), comp, 0, 0))
    return pl.pallas_call(
        _inproj_kernel,
        grid=(t // TILE,),
        in_specs=[pl.BlockSpec((TILE, d), lambda i: (i, 0)),
                  pl.BlockSpec((1, d), lambda i: (0, 0)),
                  mspec(0), mspec(1),
                  pl.BlockSpec((TILE, LANES), lambda i: (rope_tile(i), 0)),
                  pl.BlockSpec((TILE, LANES), lambda i: (rope_tile(i), 0)),
                  pl.BlockSpec((None, d, n), lambda i: (layer, 0, 0))],
        out_specs=pl.BlockSpec((TILE, n), lambda i: (i, 0)),
        out_shape=jax.ShapeDtypeStruct((t, n), BF16),
        compiler_params=_params(("arbitrary",)),
        name="in_projection",
    )(x, g, mod, mod, rope_cos, rope_sin, w)


def _gla_prepare(d, q_ref, k_ref, code_ref, w2hi_ref, w2lo_ref, gb_ref, cum_ref, tot_ref, ops_ref):
    z = _dot(code_ref[...], w2hi_ref[d]) + _dot(code_ref[...], w2lo_ref[d]) + gb_ref[d]
    lg = (jnp.minimum(z, 0.0) - jnp.log(1.0 + jnp.exp(-jnp.abs(z)))) * (1.0 / GLA_TAU)
    lg_hi = lg.astype(BF16)
    lg_lo = (lg - lg_hi.astype(F32)).astype(BF16)
    cum = _dot(cum_ref[d], lg_hi) + _dot(cum_ref[d], lg_lo)
    tot = _dot(tot_ref[...], lg_hi) + _dot(tot_ref[...], lg_lo)
    k = k_ref[...].astype(F32)
    qe = q_ref[...].astype(F32) * jnp.exp(cum)
    kd = k * jnp.exp(tot - cum)
    ops_ref[d, OP_QE] = qe.astype(BF16)
    ops_ref[d, OP_KE] = (k * jnp.exp(-cum)).astype(BF16)
    ops_ref[d, OP_KD] = kd.astype(BF16)
    n = TILE // GLA_CHUNK
    order = list(range(n)) if d == 0 else list(range(n - 1, -1, -1))
    tc = [tot[c * GLA_CHUNK:c * GLA_CHUNK + 1, :] for c in order]
    zero = jnp.zeros_like(tc[0])
    for p, c in enumerate(order):
        rows = slice(c * GLA_CHUNK, (c + 1) * GLA_CHUNK)
        before = sum(tc[:p], zero)
        after = sum(tc[p + 1:], zero)
        gap1 = tc[p - 1] if p >= 1 else zero
        gap2 = tc[p - 1] + tc[p - 2] if p >= 2 else zero
        ops_ref[d, OP_Q1, rows] = (qe[rows] * jnp.exp(gap1)).astype(BF16)
        ops_ref[d, OP_Q2, rows] = (qe[rows] * jnp.exp(gap2)).astype(BF16)
        ops_ref[d, OP_QP, rows] = (qe[rows] * jnp.exp(before)).astype(BF16)
        ops_ref[d, OP_KS, rows] = (kd[rows] * jnp.exp(after)).astype(BF16)
    return jnp.exp(sum(tc, zero))


OP_QE, OP_KE, OP_KD, OP_Q1, OP_Q2, OP_QP, OP_KS = range(7)


def _gla_kernel(qf, kf, vf, cf, qb, kb, vb, cb, w2hi, w2lo, gb, cum, tot, of, ob, st_ref, ops_ref):
    @pl.when(pl.program_id(1) == 0)
    def _():
        st_ref[...] = jnp.zeros_like(st_ref)

    decay = [_gla_prepare(0, qf, kf, cf, w2hi, w2lo, gb, cum, tot, ops_ref),
             _gla_prepare(1, qb, kb, cb, w2hi, w2lo, gb, cum, tot, ops_ref)]
    r = lax.broadcasted_iota(jnp.int32, (TILE, TILE), 0)
    s = lax.broadcasted_iota(jnp.int32, (TILE, TILE), 1)
    for d, (v_ref, o_ref) in enumerate(((vf, of), (vb, ob))):
        gap = (r // GLA_CHUNK - s // GLA_CHUNK) * (1 if d == 0 else -1)
        diag = (gap == 0) & ((s <= r) if d == 0 else (s >= r))
        for h in range(GLA_HEADS):
            kcols = slice(h * GLA_DK, (h + 1) * GLA_DK)
            vcols = slice(h * GLA_DV, (h + 1) * GLA_DV)
            op = lambda which: ops_ref[d, which, :, kcols]
            kd = op(OP_KD)
            a = jnp.where(diag, _dot_nt(op(OP_QE), op(OP_KE)),
                          jnp.where(gap == 1, _dot_nt(op(OP_QE), kd),
                                    jnp.where(gap == 2, _dot_nt(op(OP_Q1), kd),
                                              jnp.where(gap == 3, _dot_nt(op(OP_Q2), kd), 0.0))))
            v = v_ref[:, vcols]
            st = st_ref[d, h]
            o = _dot(a.astype(BF16), v) + _dot_nt(op(OP_QP), st.astype(BF16))
            o_ref[:, vcols] = o.astype(o_ref.dtype)
            st_ref[d, h] = st * decay[d][:, kcols] + _dot_tn(v, op(OP_KS))


def _gla(y, w2hi, w2lo, gbias, cum_m, tot_m, nb, nt):
    t = y.shape[0]
    fwd = lambda b, j: jnp.where(j == 0, b, nb + b * nt + j - 1)
    bwd = lambda b, j: jnp.where(j == 0, b, nb + b * nt + nt - j)

    def ysl(width, col, tile):
        return pl.BlockSpec((TILE, width), lambda b, j: (tile(b, j), col // width))

    def direction(tile):
        return [ysl(GLA_QK, COL_GQ, tile), ysl(GLA_QK, COL_GK, tile), ysl(GLA_V, COL_GV, tile),
                ysl(LANES, COL_CODE, tile)]

    const = lambda shape: pl.BlockSpec(shape, lambda b, j: (0,) * len(shape))
    in_specs = (direction(fwd) + direction(bwd)
                + [const(w2hi.shape), const(w2lo.shape), const(gbias.shape), const(cum_m.shape), const(tot_m.shape)])
    args = [y] * 8 + [w2hi, w2lo, gbias, cum_m, tot_m]
    return pl.pallas_call(
        _gla_kernel,
        grid=(nb, nt + 1),
        in_specs=in_specs,
        out_specs=[pl.BlockSpec((TILE, GLA_V), lambda b, j: (fwd(b, j), 0)),
                   pl.BlockSpec((TILE, GLA_V), lambda b, j: (bwd(b, j), 0))],
        out_shape=[jax.ShapeDtypeStruct((t, GLA_V), BF16)] * 2,
        scratch_shapes=[pltpu.VMEM((2, GLA_HEADS, GLA_DV, GLA_DK), F32),
                        pltpu.VMEM((2, 7, TILE, GLA_QK), BF16)],
        compiler_params=_params(("arbitrary", "arbitrary")),
        name="gla_scan",
    )(*args)


def _nat_kernel(q_ref, k0, k1, k2, kc, v0, v1, v2, vc, bias_ref, mask_ref, o_ref):
    lane = lax.broadcasted_iota(jnp.int32, (TILE, LANES), 1)
    low = lane < NAT_DH
    row_mask = mask_ref[...]
    for p in range(NAT_HEADS // 2):
        cols = slice(p * LANES, (p + 1) * LANES)
        qp = q_ref[:, cols] * (NAT_DH ** -0.5)
        ks = [r[:, cols] for r in (k0, k1, k2, kc)]
        vs = [r[:, cols] for r in (v0, v1, v2, vc)]
        outs = []
        for half in range(2):
            qh = jnp.where(low if half == 0 else ~low, qp, jnp.zeros_like(qp))
            s_lat = jnp.concatenate([_dot_nt(qh, kk) for kk in ks[:3]], axis=1) + (bias_ref[2 * p + half] + row_mask)
            s_ctx = _dot_nt(qh, ks[3])
            m = jnp.maximum(jnp.max(s_lat, axis=-1, keepdims=True), jnp.max(s_ctx, axis=-1, keepdims=True))
            e_lat = jnp.exp(s_lat - m)
            e_ctx = jnp.exp(s_ctx - m)
            den = jnp.sum(e_lat, axis=-1, keepdims=True) + jnp.sum(e_ctx, axis=-1, keepdims=True)
            e_lat = e_lat.astype(BF16)
            acc = _dot(e_ctx.astype(BF16), vs[3])
            for i in range(3):
                acc += _dot(e_lat[:, i * TILE:(i + 1) * TILE], vs[i])
            outs.append(acc / den)
        o_ref[:, cols] = jnp.where(low, outs[0], outs[1]).astype(o_ref.dtype)


def _nat(y, bias, layer, row_masks, nb, nt):
    t = y.shape[0]
    lat = lambda b, tt: nb + b * nt + jnp.clip(tt, 0, nt - 1)
    qtile = lambda b, j: jnp.where(j == 0, b, nb + b * nt + j - 1)
    pattern = lambda b, j: jnp.where(j == 0, 3, jnp.where(j == 1, 0, jnp.where(j == nt, 2, 1)))

    def ysl(col, tile):
        return pl.BlockSpec((TILE, NAT_W), lambda b, j: (tile(b, j), col // NAT_W))

    slab = [lambda b, j: lat(b, j - 2), lambda b, j: lat(b, j - 1), lambda b, j: lat(b, j), lambda b, j: b]
    in_specs = ([ysl(COL_NQ, qtile)] + [ysl(COL_NK, s) for s in slab] + [ysl(COL_NV, s) for s in slab]
                + [pl.BlockSpec((None, NAT_HEADS, TILE, 3 * TILE), lambda b, j: (layer, 0, 0, 0)),
                   pl.BlockSpec((None, TILE, 3 * TILE), lambda b, j: (pattern(b, j), 0, 0))])
    return pl.pallas_call(
        _nat_kernel,
        grid=(nb, nt + 1),
        in_specs=in_specs,
        out_specs=pl.BlockSpec((TILE, NAT_W), lambda b, j: (qtile(b, j), 0)),
        out_shape=jax.ShapeDtypeStruct((t, NAT_W), BF16),
        compiler_params=_params(("arbitrary", "arbitrary")),
        name="nat_attention",
    )(*([y] * 9), bias, row_masks)


def _nat_bias_base(rpb):
    nr, nc = 2 * WIN_H - 1, 2 * WIN_W - 1
    qi = np.arange(TILE_ROWS)
    kj = np.arange(3 * TILE_ROWS)
    qc = np.arange(GRID_W)
    kc = np.arange(GRID_W)
    col_start = np.clip(qc - WIN_W // 2, 0, GRID_W - WIN_W)
    col_ok = (kc[None, :] >= col_start[:, None]) & (kc[None, :] < col_start[:, None] + WIN_W)
    col_off = np.clip(kc[None, :] - qc[:, None], 1 - WIN_W, WIN_W - 1) + (WIN_W - 1)
    row_off = kj[None, :] - qi[:, None] + (WIN_H - 1 - TILE_ROWS)
    assert row_off.min() >= 0 and row_off.max() < nr
    col_sel = ((col_off[None] == np.arange(nc)[:, None, None]) & col_ok[None]).astype(np.float32)
    row_sel = (row_off[None] == np.arange(nr)[:, None, None]).astype(np.float32)
    b = jnp.einsum("dhab,aik,bcl->dhickl", rpb.astype(F32), row_sel, col_sel, precision=lax.Precision.HIGHEST)
    b = b + jnp.where(jnp.asarray(col_ok), 0.0, -jnp.inf)[None, None, None, :, None, :]
    return b.reshape(rpb.shape[0], NAT_HEADS, TILE, 3 * TILE)


def _nat_row_masks(nt):
    rows = nt * TILE_ROWS
    kh = min(WIN_H, rows)
    qi = np.arange(TILE_ROWS)
    kj = np.arange(3 * TILE_ROWS)
    masks = []
    for t in (0, min(1, nt - 1), nt - 1):
        r = TILE_ROWS * t + qi
        row_start = np.clip(r - kh // 2, 0, rows - kh)
        kr = TILE_ROWS * (t - 1) + kj
        valid = (kr >= 0) & (kr < rows)
        masks.append((kr[None, :] >= row_start[:, None]) & (kr[None, :] < row_start[:, None] + kh) & valid[None, :])
    masks.append(np.zeros_like(masks[0]))
    m = np.where(np.stack(masks), 0.0, -np.inf).astype(np.float32)
    m = jnp.broadcast_to(jnp.asarray(m)[:, :, None, :, None], (4, TILE_ROWS, GRID_W, 3 * TILE_ROWS, GRID_W))
    return m.reshape(4, TILE, 3 * TILE)


def _merge_kernel(of_ref, ob_ref, gr_ref, ga_ref, gb_ref, on_ref, x_ref, gain_ref, wbg_ref, wbn_ref, wo_ref,
                  gpost_ref, m2_ref, gpre_ref, m3_ref, m4_ref, *rest, with_router):
    if with_router:
        wrhi_ref, wrlo_ref, x_out, h_out, gate_out, count_out, count_ref = rest
    else:
        x_out, h_out = rest
    o = of_ref[...].astype(F32) + ob_ref[...].astype(F32)
    parts = []
    for h in range(GLA_HEADS):
        oh = o[:, h * GLA_DV:(h + 1) * GLA_DV]
        parts.append(oh * lax.rsqrt(jnp.mean(oh * oh, axis=-1, keepdims=True) + EPS))
    r = gr_ref[...].astype(F32)
    yg = jnp.concatenate(parts, axis=1) * gain_ref[...] * (r * jax.nn.sigmoid(r))
    m = (jax.nn.sigmoid(ga_ref[...].astype(F32)) * _dot(yg.astype(BF16), wbg_ref[...])
         + jax.nn.sigmoid(gb_ref[...].astype(F32)) * _dot(on_ref[...], wbn_ref[...]))
    y = _dot(m.astype(BF16), wo_ref[...])
    x1 = x_ref[...] + m2_ref[...] * _rms(y, gpost_ref[...])
    x_out[...] = x1
    h = _rms(x1, gpre_ref[...]) * (1.0 + m4_ref[...]) + m3_ref[...]
    h_out[...] = h.astype(h_out.dtype)
    if with_router:
        logits = _split_dot(h, wrhi_ref[...], wrlo_ref[...])
        lane = lax.broadcasted_iota(jnp.int32, logits.shape, 1)
        lg = jnp.where(lane < N_EXPERTS, logits, -jnp.inf)
        v1 = jnp.max(lg, axis=-1, keepdims=True)
        i1 = jnp.min(jnp.where(lg == v1, lane, LANES), axis=-1, keepdims=True)
        lg2 = jnp.where(lane == i1, -jnp.inf, lg)
        v2 = jnp.max(lg2, axis=-1, keepdims=True)
        i2 = jnp.min(jnp.where(lg2 == v2, lane, LANES), axis=-1, keepdims=True)
        e2 = jnp.exp(v2 - v1)
        w1 = 1.0 / (1.0 + e2)
        w2 = e2 / (1.0 + e2)

        @pl.when(pl.program_id(0) == 0)
        def _():
            count_ref[...] = jnp.zeros_like(count_ref)

        oh1 = (lane == i1).astype(F32)
        oh2 = (lane == i2).astype(F32)
        rr = lax.broadcasted_iota(jnp.int32, (TILE, TILE), 0)
        ss = lax.broadcasted_iota(jnp.int32, (TILE, TILE), 1)
        before = (ss < rr).astype(BF16)
        carry = count_ref[...]
        cnt1 = jnp.sum(oh1, axis=0, keepdims=True)
        rank1 = jnp.sum(oh1 * (_dot(before, oh1.astype(BF16)) + carry), axis=-1, keepdims=True)
        rank2 = jnp.sum(oh2 * (_dot(before, oh2.astype(BF16)) + (carry + cnt1)), axis=-1, keepdims=True)
        total = carry + cnt1 + jnp.sum(oh2, axis=0, keepdims=True)
        count_ref[...] = total
        count_out[...] = jnp.broadcast_to(total, count_out.shape)
        packed = jnp.zeros_like(logits)
        for ln, val in ((8, i1.astype(F32)), (9, i2.astype(F32)), (10, w1), (11, w2), (12, rank1), (13, rank2)):
            packed = jnp.where(lane == ln, val, packed)
        gate_out[...] = packed


def _merge(y, of, ob, on, x, gain, wbg, wbn, wo, layer, gpost, gpre, mod, router, nb, nt):
    t, d = x.shape
    row_of_tile = lambda i: jnp.where(i < nb, nb, (i - nb) // nt)
    mspec = lambda comp: pl.BlockSpec((None, None, 1, d), lambda i: (row_of_tile(i), comp, 0, 0))
    rowblk = lambda w: pl.BlockSpec((TILE, w), lambda i: (i, 0))
    ycol = lambda w, col: pl.BlockSpec((TILE, w), lambda i: (i, col // w))
    const = lambda a: pl.BlockSpec(a.shape, lambda i: (0,) * a.ndim)
    stacked = lambda a: pl.BlockSpec((None,) + a.shape[1:], lambda i: (layer,) + (0,) * (a.ndim - 1))
    in_specs = [rowblk(GLA_V), rowblk(GLA_V), ycol(GLA_V, COL_GR), ycol(d, COL_GA), ycol(d, COL_GB), rowblk(NAT_W),
                rowblk(d), const(gain), stacked(wbg), stacked(wbn), stacked(wo), const(gpost), mspec(2), const(gpre),
                mspec(3), mspec(4)]
    args = [of, ob, y, y, y, on, x, gain, wbg, wbn, wo, gpost, mod, gpre, mod, mod]
    out_specs = [rowblk(d), rowblk(d)]
    out_shape = [jax.ShapeDtypeStruct((t, d), F32), jax.ShapeDtypeStruct((t, d), BF16)]
    scratch = []
    if router is not None:
        in_specs += [const(router[0]), const(router[1])]
        args += list(router)
        out_specs += [rowblk(LANES), pl.BlockSpec((8, LANES), lambda i: (0, 0))]
        out_shape += [jax.ShapeDtypeStruct((t, LANES), F32), jax.ShapeDtypeStruct((8, LANES), F32)]
        out_shape[1] = jax.ShapeDtypeStruct((t, d), F32)
        scratch = [pltpu.VMEM((1, LANES), F32)]
    return pl.pallas_call(
        functools.partial(_merge_kernel, with_router=router is not None),
        grid=(t // TILE,),
        in_specs=in_specs,
        out_specs=out_specs,
        out_shape=out_shape,
        scratch_shapes=scratch,
        compiler_params=_params(("arbitrary",)),
        name="merge_router" if router is not None else "merge",
    )(*args)


def _ffn_kernel(h_ref, w1_ref, w3_ref, w2_ref, x_ref, gpost_ref, m5_ref, o_ref, acc_ref):
    f = pl.program_id(1)

    @pl.when(f == 0)
    def _():
        acc_ref[...] = jnp.zeros_like(acc_ref)

    h = h_ref[...]
    a = _dot(h, w1_ref[...])
    u = (a * jax.nn.sigmoid(a)) * _dot(h, w3_ref[...])
    acc_ref[...] += _dot(u.astype(BF16), w2_ref[...])

    @pl.when(f == pl.num_programs(1) - 1)
    def _():
        o_ref[...] = x_ref[...] + m5_ref[...] * _rms(acc_ref[...], gpost_ref[...])


def _dispatch_kernel(pos_ref, h_ref, xs_in, xs_out, sem):
    del xs_in
    t = pl.num_programs(0) * TILE
    base = pl.program_id(0) * TILE

    def copies(r):
        src = h_ref.at[pl.ds(r, 1)]
        return (pltpu.make_async_copy(src, xs_out.at[pl.ds(pos_ref[base + r], 1)], sem),
                pltpu.make_async_copy(src, xs_out.at[pl.ds(pos_ref[t + base + r], 1)], sem))

    def start(r, carry):
        for cp in copies(r):
            cp.start()
        return carry

    def wait(r, carry):
        for cp in copies(r):
            cp.wait()
        return carry

    lax.fori_loop(0, TILE, start, 0, unroll=8)
    lax.fori_loop(0, TILE, wait, 0, unroll=8)


def _moe_ffn_kernel(te_ref, nu_ref, x_ref, w1_ref, w3_ref, w2_ref, o_ref, xb_ref):
    del te_ref
    i = pl.program_id(0)
    f = pl.program_id(1)

    @pl.when(i < nu_ref[0])
    def _():
        @pl.when(f == 0)
        def _():
            xb_ref[...] = x_ref[...].astype(BF16)

        h = xb_ref[...]
        a = _dot(h, w1_ref[0])
        u = (a * jax.nn.sigmoid(a)) * _dot(h, w3_ref[0])
        y = _dot(u.astype(BF16), w2_ref[0])

        @pl.when(f == 0)
        def _():
            o_ref[...] = y

        @pl.when(f > 0)
        def _():
            o_ref[...] += y

    @pl.when((i >= nu_ref[0]) & (f == 0))
    def _():
        o_ref[...] = jnp.zeros_like(o_ref)


def _combine_kernel(pos_ref, ys_hbm, g_ref, x_ref, gpost_ref, m5_ref, o_ref, buf_ref, sem, *, t, tile0):
    i = pl.program_id(0)
    n = pl.num_programs(0)

    def copies(step, slot, r):
        base = (step + tile0) * TILE
        return (pltpu.make_async_copy(ys_hbm.at[pl.ds(pos_ref[base + r], 1)],
                                      buf_ref.at[slot, 0, pl.ds(r, 1)], sem.at[slot]),
                pltpu.make_async_copy(ys_hbm.at[pl.ds(pos_ref[t + base + r], 1)],
                                      buf_ref.at[slot, 1, pl.ds(r, 1)], sem.at[slot]))

    def issue(step, slot):
        def body(r, carry):
            for cp in copies(step, slot, r):
                cp.start()
            return carry
        lax.fori_loop(0, TILE, body, 0, unroll=8)

    @pl.when(i == 0)
    def _():
        issue(0, 0)

    @pl.when(i + 1 < n)
    def _():
        issue(i + 1, (i + 1) % 2)

    slot = i % 2

    def wait(r, carry):
        for cp in copies(i, slot, r):
            cp.wait()
        return carry

    lax.fori_loop(0, TILE, wait, 0, unroll=8)
    g = g_ref[...]
    y = g[:, 10:11] * buf_ref[slot, 0] + g[:, 11:12] * buf_ref[slot, 1]
    o_ref[...] = x_ref[...] + m5_ref[...] * _rms(y, gpost_ref[...])


def _ffn_tiles(t, nb, nt):
    tm = 1024 if (nb * TILE) % 1024 == 0 and (nt * TILE) % 1024 == 0 else TILE
    per = tm // TILE
    row_of_tile = lambda i: jnp.where(i * per < nb, nb, (i * per - nb) // nt)
    return tm, row_of_tile


def _ffn(h, w1, w3, w2, layer, x, gpost, mod, nb, nt):
    t, d = x.shape
    dff = w1.shape[2]
    tm, row_of_tile = _ffn_tiles(t, nb, nt)
    tf = 512
    return pl.pallas_call(
        _ffn_kernel,
        grid=(t // tm, dff // tf),
        in_specs=[pl.BlockSpec((tm, d), lambda i, f: (i, 0)),
                  pl.BlockSpec((None, d, tf), lambda i, f: (layer, 0, f)),
                  pl.BlockSpec((None, d, tf), lambda i, f: (layer, 0, f)),
                  pl.BlockSpec((None, tf, d), lambda i, f: (layer, f, 0)),
                  pl.BlockSpec((tm, d), lambda i, f: (i, 0)),
                  pl.BlockSpec((1, d), lambda i, f: (0, 0)),
                  pl.BlockSpec((None, None, 1, d), lambda i, f: (row_of_tile(i), 5, 0, 0))],
        out_specs=pl.BlockSpec((tm, d), lambda i, f: (i, 0)),
        out_shape=jax.ShapeDtypeStruct((t, d), F32),
        scratch_shapes=[pltpu.VMEM((tm, d), F32)],
        compiler_params=_params(("arbitrary", "arbitrary")),
        name="ffn_dense",
    )(h, w1, w3, w2, x, gpost, mod)


MOE_TM = 512
MOE_TF = 896


def _moe_routed(h, pack, counts, w1, w3, w2, layer, x, gpost, mod, nb, nt, latent_only):
    t, d = x.shape
    _, ne, _, dff = w1.shape
    tm = MOE_TM
    tf = MOE_TF if dff % MOE_TF == 0 else 512
    n_tiles = -(-(2 * t) // tm) + ne
    p_rows = n_tiles * tm
    cnt = counts[0, :ne].astype(jnp.int32)
    padded = (cnt + tm - 1) // tm * tm
    ends = jnp.cumsum(padded)
    offs = ends - padded
    e1 = pack[:, 8].astype(jnp.int32)
    e2 = pack[:, 9].astype(jnp.int32)
    pos = jnp.concatenate([offs[e1] + pack[:, 12].astype(jnp.int32), offs[e2] + pack[:, 13].astype(jnp.int32)])
    n_used = (ends[-1] // tm).astype(jnp.int32).reshape(1)
    tile_start = jnp.arange(n_tiles, dtype=jnp.int32) * tm
    tile_start = jnp.minimum(tile_start, ends[-1] - tm)
    tile_expert = jnp.minimum(jnp.sum(tile_start[:, None] >= ends[None, :], axis=1), ne - 1).astype(jnp.int32)

    xs = pl.pallas_call(
        _dispatch_kernel,
        grid_spec=pltpu.PrefetchScalarGridSpec(
            num_scalar_prefetch=1, grid=(t // TILE,),
            in_specs=[pl.BlockSpec((TILE, d), lambda i, pos: (i, 0)), pl.BlockSpec(memory_space=pl.ANY)],
            out_specs=pl.BlockSpec(memory_space=pl.ANY),
            scratch_shapes=[pltpu.SemaphoreType.DMA(())]),
        out_shape=jax.ShapeDtypeStruct((p_rows, d), F32),
        input_output_aliases={2: 0},
        compiler_params=_params(("arbitrary",)),
        name="moe_dispatch",
    )(pos, h, jnp.zeros((p_rows, d), F32))

    nf = dff // tf
    row = lambda i, nu: jnp.minimum(i, nu[0] - 1)
    fcol = lambda i, f, nu: jnp.where(i < nu[0], f, nf - 1)
    ys = pl.pallas_call(
        _moe_ffn_kernel,
        grid_spec=pltpu.PrefetchScalarGridSpec(
            num_scalar_prefetch=2, grid=(n_tiles, nf),
            in_specs=[pl.BlockSpec((tm, d), lambda i, f, te, nu: (row(i, nu), 0)),
                      pl.BlockSpec((None, 1, d, tf), lambda i, f, te, nu: (layer, te[i], 0, fcol(i, f, nu))),
                      pl.BlockSpec((None, 1, d, tf), lambda i, f, te, nu: (layer, te[i], 0, fcol(i, f, nu))),
                      pl.BlockSpec((None, 1, tf, d), lambda i, f, te, nu: (layer, te[i], fcol(i, f, nu), 0))],
            out_specs=pl.BlockSpec((tm, d), lambda i, f, te, nu: (i, 0)),
            scratch_shapes=[pltpu.VMEM((tm, d), BF16)]),
        out_shape=jax.ShapeDtypeStruct((p_rows, d), F32),
        compiler_params=_params(("arbitrary", "arbitrary")),
        name="moe_ffn",
    )(tile_expert, n_used, xs, w1, w3, w2)

    tile0 = nb if latent_only else 0
    n_out = t // TILE - tile0
    row_of_tile = lambda i: jnp.where(i + tile0 < nb, nb, (i + tile0 - nb) // nt)
    return pl.pallas_call(
        functools.partial(_combine_kernel, t=t, tile0=tile0),
        grid_spec=pltpu.PrefetchScalarGridSpec(
            num_scalar_prefetch=1, grid=(n_out,),
            in_specs=[pl.BlockSpec(memory_space=pl.ANY),
                      pl.BlockSpec((TILE, LANES), lambda i, pos: (i + tile0, 0)),
                      pl.BlockSpec((TILE, d), lambda i, pos: (i + tile0, 0)),
                      pl.BlockSpec((1, d), lambda i, pos: (0, 0)),
                      pl.BlockSpec((None, None, 1, d), lambda i, pos: (row_of_tile(i), 5, 0, 0))],
            out_specs=pl.BlockSpec((TILE, d), lambda i, pos: (i, 0)),
            scratch_shapes=[pltpu.VMEM((2, 2, TILE, d), F32), pltpu.SemaphoreType.DMA((2,))]),
        out_shape=jax.ShapeDtypeStruct((n_out * TILE, d), F32),
        compiler_params=_params(("arbitrary",)),
        name="moe_combine",
    )(pos, ys, pack, x, gpost, mod)


def _reorder_w_in(w):
    a = GLA_QK * 2 + GLA_V * 2
    c0 = a + 2 * GLA_RANK
    g0 = c0 + 3 * NAT_W
    wb = w.astype(BF16)
    pad = jnp.zeros(w.shape[:2] + (LANES - 2 * GLA_RANK,), BF16)
    return jnp.concatenate([wb[..., :a], wb[..., g0:], wb[..., c0:g0], wb[..., a:c0], pad], axis=-1)


def _rope_tables(nt):
    t = np.arange(nt * TILE)
    pos = np.stack([t // GRID_W, t % GRID_W], axis=1).astype(np.float32)
    nf = GLA_DK // 4
    inv = (ROPE_BASE ** (-jnp.arange(nf, dtype=F32) / nf))
    ang = jnp.asarray(pos)[:, :, None] * inv[None, None, :]
    cos = jnp.concatenate([jnp.cos(ang), jnp.cos(ang)], axis=-1).reshape(nt * TILE, GLA_DK)
    sin = jnp.concatenate([-jnp.sin(ang), jnp.sin(ang)], axis=-1).reshape(nt * TILE, GLA_DK)
    ident = jnp.ones((TILE, GLA_DK), F32)
    return (jnp.concatenate([ident, cos], axis=0), jnp.concatenate([jnp.zeros_like(ident), sin], axis=0))


def _chunk_matrices():
    i = np.arange(TILE)
    same = (i[:, None] // GLA_CHUNK) == (i[None, :] // GLA_CHUNK)
    lower = same & (i[None, :] <= i[:, None])
    upper = same & (i[None, :] >= i[:, None])
    cum = jnp.asarray(np.stack([lower, upper]).astype(np.float32), BF16)
    return cum, jnp.asarray(same.astype(np.float32), BF16)


def _hi_lo(w):
    hi = w.astype(BF16)
    return hi, (w - hi.astype(F32)).astype(BF16)


def kernel(x, c, ctx, c_ctx, w_mod, b_mod, norm_mix_pre, norm_mix_post, w_in, gla_gate_w2, gla_gate_b, gla_norm,
           nat_rpb, w_branch_gla, w_branch_nat, w_out, norm_ffn_pre, norm_ffn_post, ffn_w1, ffn_w3, ffn_w2,
           moe_router, moe_w1, moe_w3, moe_w2):
    nb, seq, d = x.shape
    depth = w_mod.shape[0]
    assert ctx.shape[1] == TILE and seq % TILE == 0 and d % LANES == 0
    nt = seq // TILE
    mod_rows = -(-(nb + 1) // 8) * 8
    c_rows = jnp.concatenate([c, c_ctx[None], jnp.zeros((mod_rows - nb - 1, d), c.dtype)], axis=0)
    mod_all = _modulation(c_rows, w_mod, b_mod).reshape(depth, mod_rows, 6, 1, d)
    xs = jnp.concatenate([ctx.reshape(nb * TILE, d), x.reshape(nb * seq, d)], axis=0)
    rope_cos, rope_sin = _rope_tables(nt)
    cum_m, tot_m = _chunk_matrices()
    row = lambda v: v.reshape(1, -1)
    w_in_b = _reorder_w_in(w_in)
    wbg_b, wbn_b, wo_b = w_branch_gla.astype(BF16), w_branch_nat.astype(BF16), w_out.astype(BF16)
    ffn_b = [w.astype(BF16) for w in (ffn_w1, ffn_w3, ffn_w2)]
    moe_b = [w.astype(BF16) for w in (moe_w1, moe_w3, moe_w2)]
    nat_bias = _nat_bias_base(nat_rpb)
    nat_masks = _nat_row_masks(nt)
    w2p = jnp.zeros((depth, 2, LANES, GLA_QK), F32)
    w2p = w2p.at[:, 0, :GLA_RANK].set(gla_gate_w2[:, 0]).at[:, 1, GLA_RANK:2 * GLA_RANK].set(gla_gate_w2[:, 1])
    w2hi, w2lo = _hi_lo(w2p)
    wr = jnp.zeros((moe_router.shape[0], d, LANES), F32).at[:, :, :N_EXPERTS].set(moe_router)
    wr_hi, wr_lo = _hi_lo(wr)
    for i in range(depth):
        mod = mod_all[i]
        last = i == depth - 1
        y = _in_projection(xs, row(norm_mix_pre[i]), mod, rope_cos, rope_sin, w_in_b, i, nb, nt)
        of, ob = _gla(y, w2hi[i], w2lo[i], gla_gate_b[i].reshape(2, 1, GLA_QK), cum_m, tot_m, nb, nt)
        on = _nat(y, nat_bias, i, nat_masks, nb, nt)
        is_moe = i % 2 == 1
        j = i // 2
        router = (wr_hi[j], wr_lo[j]) if is_moe else None
        outs = _merge(y, of, ob, on, xs, row(gla_norm[i]), wbg_b, wbn_b, wo_b, i, row(norm_mix_post[i]),
                      row(norm_ffn_pre[i]), mod, router, nb, nt)
        if is_moe:
            xs, h, pack, counts = outs
            xs = _moe_routed(h, pack, counts, *moe_b, j, xs, row(norm_ffn_post[i]), mod, nb, nt, latent_only=last)
        else:
            xs, h = outs
            xs = _ffn(h, *ffn_b, j, xs, row(norm_ffn_post[i]), mod, nb, nt)
            if last:
                xs = xs[nb * TILE:]
    return xs.reshape(nb, seq, d)
```

```python
import functools

import numpy as np
import jax
import jax.numpy as jnp
from jax import lax
from jax.experimental import pallas as pl
from jax.experimental.pallas import tpu as pltpu

EPS = 1e-6
GRID_W = 64
TILE = 256
TILE_ROWS = TILE // GRID_W
GLA_HEADS, GLA_DK, GLA_DV, GLA_RANK, GLA_TAU, GLA_CHUNK = 4, 128, 256, 16, 16.0, 64
NAT_HEADS, NAT_DH = 8, 64
WIN_H, WIN_W = 8, 16
N_EXPERTS = 8
ROPE_BASE = 10000.0
GLA_QK = GLA_HEADS * GLA_DK
GLA_V = GLA_HEADS * GLA_DV
NAT_W = NAT_HEADS * NAT_DH
LANES = 128
COL_GQ, COL_GK, COL_GV, COL_GR, COL_GA, COL_GB = 0, 512, 1024, 2048, 3072, 4096
COL_NQ, COL_NK, COL_NV, COL_CODE = 5120, 5632, 6144, 6656
N_IN = COL_CODE + LANES
VMEM_LIMIT = 56 * 1024 * 1024
BF16 = jnp.bfloat16
F32 = jnp.float32


def _dot(a, b):
    return jnp.dot(a, b, preferred_element_type=F32)


def _dot_nt(a, b):
    return lax.dot_general(a, b, (((1,), (1,)), ((), ())), preferred_element_type=F32)


def _dot_tn(a, b):
    return lax.dot_general(a, b, (((0,), (0,)), ((), ())), preferred_element_type=F32)


def _rms(x, g):
    return x * lax.rsqrt(jnp.mean(x * x, axis=-1, keepdims=True) + EPS) * g


def _split_dot(a, b_hi, b_lo):
    a_hi = a.astype(BF16)
    a_lo = (a - a_hi.astype(F32)).astype(BF16)
    return _dot(a_hi, b_hi) + (_dot(a_lo, b_hi) + _dot(a_hi, b_lo))


def _params(sem):
    return pltpu.CompilerParams(dimension_semantics=sem, vmem_limit_bytes=VMEM_LIMIT)


def _mod_kernel(c_ref, w_ref, b_ref, o_ref):
    c = c_ref[...]
    s = c * jax.nn.sigmoid(c)
    o_ref[0] = _dot(s.astype(BF16), w_ref[0].astype(BF16)) + b_ref[0]


def _modulation(c_rows, w_mod, b_mod):
    depth, d, n = w_mod.shape
    rows = c_rows.shape[0]
    tn = 1536
    return pl.pallas_call(
        _mod_kernel,
        grid=(depth, n // tn),
        in_specs=[pl.BlockSpec((rows, d), lambda l, j: (0, 0)),
                  pl.BlockSpec((1, d, tn), lambda l, j: (l, 0, j)),
                  pl.BlockSpec((1, 1, tn), lambda l, j: (l, 0, j))],
        out_specs=pl.BlockSpec((1, rows, tn), lambda l, j: (l, 0, j)),
        out_shape=jax.ShapeDtypeStruct((depth, rows, n), F32),
        compiler_params=_params(("arbitrary", "arbitrary")),
        name="modulation",
    )(c_rows, w_mod, b_mod.reshape(depth, 1, n))


def _inproj_kernel(x_ref, g_ref, sh_ref, sc_ref, cos_ref, sin_ref, w_ref, o_ref):
    h = _rms(x_ref[...], g_ref[...]) * (1.0 + sc_ref[...]) + sh_ref[...]
    hb = h.astype(BF16)
    n = o_ref.shape[1]
    qk = _dot(hb, w_ref[:, :2 * GLA_QK])
    lane = lax.broadcasted_iota(jnp.int32, qk.shape, 1)
    reps = 2 * GLA_HEADS
    cos = jnp.concatenate([cos_ref[...]] * reps, axis=1)
    sin = jnp.concatenate([sin_ref[...]] * reps, axis=1)
    partner = jnp.where((lane % 64) < 32, pltpu.roll(qk, 2 * GLA_QK - 32, 1), pltpu.roll(qk, 32, 1))
    qk = qk * cos + partner * sin
    qk = jnp.where(lane < GLA_QK, qk * (GLA_DK ** -0.5), qk)
    o_ref[:, :2 * GLA_QK] = qk.astype(o_ref.dtype)
    step = 1024
    for j in range(2 * GLA_QK, n, step):
        w = min(step, n - j)
        o_ref[:, j:j + w] = _dot(hb, w_ref[:, j:j + w]).astype(o_ref.dtype)


def _in_projection(x, g, mod, rope_cos, rope_sin, w, layer, nb, nt):
    t, d = x.shape
    n = w.shape[2]
    row_of_tile = lambda i: jnp.where(i < nb, nb, (i - nb) // nt)
    rope_tile = lambda i: jnp.where(i < nb, 0, 1 + (i - nb) % nt)
    dm = mod.shape[-1]
    mspec = lambda comp: pl.BlockSpec((None, None, 1, dm), lambda i: (row_of_tile(i), comp, 0, 0))
    return pl.pallas_call(
        _inproj_kernel,
        grid=(t // TILE,),
        in_specs=[pl.BlockSpec((TILE, d), lambda i: (i, 0)),
                  pl.BlockSpec((1, d), lambda i: (0, 0)),
                  mspec(0), mspec(1),
                  pl.BlockSpec((TILE, LANES), lambda i: (rope_tile(i), 0)),
                  pl.BlockSpec((TILE, LANES), lambda i: (rope_tile(i), 0)),
                  pl.BlockSpec((None, d, n), lambda i: (layer, 0, 0))],
        out_specs=pl.BlockSpec((TILE, n), lambda i: (i, 0)),
        out_shape=jax.ShapeDtypeStruct((t, n), BF16),
        compiler_params=_params(("arbitrary",)),
        name="in_projection",
    )(x, g, mod, mod, rope_cos, rope_sin, w)


def _gla_prepare(d, q_ref, k_ref, code_ref, w2hi_ref, w2lo_ref, gb_ref, cum_ref, tot_ref, ops_ref):
    z = _dot(code_ref[...], w2hi_ref[d]) + _dot(code_ref[...], w2lo_ref[d]) + gb_ref[d]
    lg = (jnp.minimum(z, 0.0) - jnp.log(1.0 + jnp.exp(-jnp.abs(z)))) * (1.0 / GLA_TAU)
    lg_hi = lg.astype(BF16)
    lg_lo = (lg - lg_hi.astype(F32)).astype(BF16)
    cum = _dot(cum_ref[d], lg_hi) + _dot(cum_ref[d], lg_lo)
    tot = _dot(tot_ref[...], lg_hi) + _dot(tot_ref[...], lg_lo)
    k = k_ref[...].astype(F32)
    qe = q_ref[...].astype(F32) * jnp.exp(cum)
    kd = k * jnp.exp(tot - cum)
    ops_ref[d, OP_QE] = qe.astype(BF16)
    ops_ref[d, OP_KE] = (k * jnp.exp(-cum)).astype(BF16)
    ops_ref[d, OP_KD] = kd.astype(BF16)
    n = TILE // GLA_CHUNK
    order = list(range(n)) if d == 0 else list(range(n - 1, -1, -1))
    tc = [tot[c * GLA_CHUNK:c * GLA_CHUNK + 1, :] for c in order]
    zero = jnp.zeros_like(tc[0])
    for p, c in enumerate(order):
        rows = slice(c * GLA_CHUNK, (c + 1) * GLA_CHUNK)
        before = sum(tc[:p], zero)
        after = sum(tc[p + 1:], zero)
        gap1 = tc[p - 1] if p >= 1 else zero
        gap2 = tc[p - 1] + tc[p - 2] if p >= 2 else zero
        ops_ref[d, OP_Q1, rows] = (qe[rows] * jnp.exp(gap1)).astype(BF16)
        ops_ref[d, OP_Q2, rows] = (qe[rows] * jnp.exp(gap2)).astype(BF16)
        ops_ref[d, OP_QP, rows] = (qe[rows] * jnp.exp(before)).astype(BF16)
        ops_ref[d, OP_KS, rows] = (kd[rows] * jnp.exp(after)).astype(BF16)
    return jnp.exp(sum(tc, zero))


OP_QE, OP_KE, OP_KD, OP_Q1, OP_Q2, OP_QP, OP_KS = range(7)


def _gla_kernel(qf, kf, vf, cf, qb, kb, vb, cb, w2hi, w2lo, gb, cum, tot, of, ob, st_ref, ops_ref):
    @pl.when(pl.program_id(1) == 0)
    def _():
        st_ref[...] = jnp.zeros_like(st_ref)

    decay = [_gla_prepare(0, qf, kf, cf, w2hi, w2lo, gb, cum, tot, ops_ref),
             _gla_prepare(1, qb, kb, cb, w2hi, w2lo, gb, cum, tot, ops_ref)]
    r = lax.broadcasted_iota(jnp.int32, (TILE, TILE), 0)
    s = lax.broadcasted_iota(jnp.int32, (TILE, TILE), 1)
    for d, (v_ref, o_ref) in enumerate(((vf, of), (vb, ob))):
        gap = (r // GLA_CHUNK - s // GLA_CHUNK) * (1 if d == 0 else -1)
        diag = (gap == 0) & ((s <= r) if d == 0 else (s >= r))
        for h in range(GLA_HEADS):
            kcols = slice(h * GLA_DK, (h + 1) * GLA_DK)
            vcols = slice(h * GLA_DV, (h + 1) * GLA_DV)
            op = lambda which: ops_ref[d, which, :, kcols]
            kd = op(OP_KD)
            a = jnp.where(diag, _dot_nt(op(OP_QE), op(OP_KE)),
                          jnp.where(gap == 1, _dot_nt(op(OP_QE), kd),
                                    jnp.where(gap == 2, _dot_nt(op(OP_Q1), kd),
                                              jnp.where(gap == 3, _dot_nt(op(OP_Q2), kd), 0.0))))
            v = v_ref[:, vcols]
            st = st_ref[d, h]
            o = _dot(a.astype(BF16), v) + _dot_nt(op(OP_QP), st.astype(BF16))
            o_ref[:, vcols] = o.astype(o_ref.dtype)
            st_ref[d, h] = st * decay[d][:, kcols] + _dot_tn(v, op(OP_KS))


def _gla(y, w2hi, w2lo, gbias, cum_m, tot_m, nb, nt):
    t = y.shape[0]
    fwd = lambda b, j: jnp.where(j == 0, b, nb + b * nt + j - 1)
    bwd = lambda b, j: jnp.where(j == 0, b, nb + b * nt + nt - j)

    def ysl(width, col, tile):
        return pl.BlockSpec((TILE, width), lambda b, j: (tile(b, j), col // width))

    def direction(tile):
        return [ysl(GLA_QK, COL_GQ, tile), ysl(GLA_QK, COL_GK, tile), ysl(GLA_V, COL_GV, tile),
                ysl(LANES, COL_CODE, tile)]

    const = lambda shape: pl.BlockSpec(shape, lambda b, j: (0,) * len(shape))
    in_specs = (direction(fwd) + direction(bwd)
                + [const(w2hi.shape), const(w2lo.shape), const(gbias.shape), const(cum_m.shape), const(tot_m.shape)])
    args = [y] * 8 + [w2hi, w2lo, gbias, cum_m, tot_m]
    return pl.pallas_call(
        _gla_kernel,
        grid=(nb, nt + 1),
        in_specs=in_specs,
        out_specs=[pl.BlockSpec((TILE, GLA_V), lambda b, j: (fwd(b, j), 0)),
                   pl.BlockSpec((TILE, GLA_V), lambda b, j: (bwd(b, j), 0))],
        out_shape=[jax.ShapeDtypeStruct((t, GLA_V), BF16)] * 2,
        scratch_shapes=[pltpu.VMEM((2, GLA_HEADS, GLA_DV, GLA_DK), F32),
                        pltpu.VMEM((2, 7, TILE, GLA_QK), BF16)],
        compiler_params=_params(("arbitrary", "arbitrary")),
        name="gla_scan",
    )(*args)


def _nat_kernel(q_ref, k0, k1, k2, kc, v0, v1, v2, vc, bias_ref, mask_ref, o_ref):
    lane = lax.broadcasted_iota(jnp.int32, (TILE, LANES), 1)
    low = lane < NAT_DH
    row_mask = mask_ref[...]
    for p in range(NAT_HEADS // 2):
        cols = slice(p * LANES, (p + 1) * LANES)
        qp = q_ref[:, cols] * (NAT_DH ** -0.5)
        ks = [r[:, cols] for r in (k0, k1, k2, kc)]
        vs = [r[:, cols] for r in (v0, v1, v2, vc)]
        outs = []
        for half in range(2):
            qh = jnp.where(low if half == 0 else ~low, qp, jnp.zeros_like(qp))
            s_lat = jnp.concatenate([_dot_nt(qh, kk) for kk in ks[:3]], axis=1) + (bias_ref[2 * p + half] + row_mask)
            s_ctx = _dot_nt(qh, ks[3])
            m = jnp.maximum(jnp.max(s_lat, axis=-1, keepdims=True), jnp.max(s_ctx, axis=-1, keepdims=True))
            e_lat = jnp.exp(s_lat - m)
            e_ctx = jnp.exp(s_ctx - m)
            den = jnp.sum(e_lat, axis=-1, keepdims=True) + jnp.sum(e_ctx, axis=-1, keepdims=True)
            e_lat = e_lat.astype(BF16)
            acc = _dot(e_ctx.astype(BF16), vs[3])
            for i in range(3):
                acc += _dot(e_lat[:, i * TILE:(i + 1) * TILE], vs[i])
            outs.append(acc / den)
        o_ref[:, cols] = jnp.where(low, outs[0], outs[1]).astype(o_ref.dtype)


def _nat(y, bias, layer, row_masks, nb, nt):
    t = y.shape[0]
    lat = lambda b, tt: nb + b * nt + jnp.clip(tt, 0, nt - 1)
    qtile = lambda b, j: jnp.where(j == 0, b, nb + b * nt + j - 1)
    pattern = lambda b, j: jnp.where(j == 0, 3, jnp.where(j == 1, 0, jnp.where(j == nt, 2, 1)))

    def ysl(col, tile):
        return pl.BlockSpec((TILE, NAT_W), lambda b, j: (tile(b, j), col // NAT_W))

    slab = [lambda b, j: lat(b, j - 2), lambda b, j: lat(b, j - 1), lambda b, j: lat(b, j), lambda b, j: b]
    in_specs = ([ysl(COL_NQ, qtile)] + [ysl(COL_NK, s) for s in slab] + [ysl(COL_NV, s) for s in slab]
                + [pl.BlockSpec((None, NAT_HEADS, TILE, 3 * TILE), lambda b, j: (layer, 0, 0, 0)),
                   pl.BlockSpec((None, TILE, 3 * TILE), lambda b, j: (pattern(b, j), 0, 0))])
    return pl.pallas_call(
        _nat_kernel,
        grid=(nb, nt + 1),
        in_specs=in_specs,
        out_specs=pl.BlockSpec((TILE, NAT_W), lambda b, j: (qtile(b, j), 0)),
        out_shape=jax.ShapeDtypeStruct((t, NAT_W), BF16),
        compiler_params=_params(("arbitrary", "arbitrary")),
        name="nat_attention",
    )(*([y] * 9), bias, row_masks)


def _nat_bias_base(rpb):
    nr, nc = 2 * WIN_H - 1, 2 * WIN_W - 1
    qi = np.arange(TILE_ROWS)
    kj = np.arange(3 * TILE_ROWS)
    qc = np.arange(GRID_W)
    kc = np.arange(GRID_W)
    col_start = np.clip(qc - WIN_W // 2, 0, GRID_W - WIN_W)
    col_ok = (kc[None, :] >= col_start[:, None]) & (kc[None, :] < col_start[:, None] + WIN_W)
    col_off = np.clip(kc[None, :] - qc[:, None], 1 - WIN_W, WIN_W - 1) + (WIN_W - 1)
    row_off = kj[None, :] - qi[:, None] + (WIN_H - 1 - TILE_ROWS)
    assert row_off.min() >= 0 and row_off.max() < nr
    col_sel = ((col_off[None] == np.arange(nc)[:, None, None]) & col_ok[None]).astype(np.float32)
    row_sel = (row_off[None] == np.arange(nr)[:, None, None]).astype(np.float32)
    b = jnp.einsum("dhab,aik,bcl->dhickl", rpb.astype(F32), row_sel, col_sel, precision=lax.Precision.HIGHEST)
    b = b + jnp.where(jnp.asarray(col_ok), 0.0, -jnp.inf)[None, None, None, :, None, :]
    return b.reshape(rpb.shape[0], NAT_HEADS, TILE, 3 * TILE)


def _nat_row_masks(nt):
    rows = nt * TILE_ROWS
    kh = min(WIN_H, rows)
    qi = np.arange(TILE_ROWS)
    kj = np.arange(3 * TILE_ROWS)
    masks = []
    for t in (0, min(1, nt - 1), nt - 1):
        r = TILE_ROWS * t + qi
        row_start = np.clip(r - kh // 2, 0, rows - kh)
        kr = TILE_ROWS * (t - 1) + kj
        valid = (kr >= 0) & (kr < rows)
        masks.append((kr[None, :] >= row_start[:, None]) & (kr[None, :] < row_start[:, None] + kh) & valid[None, :])
    masks.append(np.zeros_like(masks[0]))
    m = np.where(np.stack(masks), 0.0, -np.inf).astype(np.float32)
    m = jnp.broadcast_to(jnp.asarray(m)[:, :, None, :, None], (4, TILE_ROWS, GRID_W, 3 * TILE_ROWS, GRID_W))
    return m.reshape(4, TILE, 3 * TILE)


def _merge_kernel(of_ref, ob_ref, gr_ref, ga_ref, gb_ref, on_ref, x_ref, gain_ref, wbg_ref, wbn_ref, wo_ref,
                  gpost_ref, m2_ref, gpre_ref, m3_ref, m4_ref, *rest, with_router):
    if with_router:
        wrhi_ref, wrlo_ref, x_out, h_out, gate_out, count_out, count_ref = rest
    else:
        x_out, h_out = rest
    o = of_ref[...].astype(F32) + ob_ref[...].astype(F32)
    parts = []
    for h in range(GLA_HEADS):
        oh = o[:, h * GLA_DV:(h + 1) * GLA_DV]
        parts.append(oh * lax.rsqrt(jnp.mean(oh * oh, axis=-1, keepdims=True) + EPS))
    r = gr_ref[...].astype(F32)
    yg = jnp.concatenate(parts, axis=1) * gain_ref[...] * (r * jax.nn.sigmoid(r))
    m = (jax.nn.sigmoid(ga_ref[...].astype(F32)) * _dot(yg.astype(BF16), wbg_ref[...])
         + jax.nn.sigmoid(gb_ref[...].astype(F32)) * _dot(on_ref[...], wbn_ref[...]))
    y = _dot(m.astype(BF16), wo_ref[...])
    x1 = x_ref[...] + m2_ref[...] * _rms(y, gpost_ref[...])
    x_out[...] = x1
    h = _rms(x1, gpre_ref[...]) * (1.0 + m4_ref[...]) + m3_ref[...]
    h_out[...] = h.astype(h_out.dtype)
    if with_router:
        logits = _split_dot(h, wrhi_ref[...], wrlo_ref[...])
        lane = lax.broadcasted_iota(jnp.int32, logits.shape, 1)
        lg = jnp.where(lane < N_EXPERTS, logits, -jnp.inf)
        v1 = jnp.max(lg, axis=-1, keepdims=True)
        i1 = jnp.min(jnp.where(lg == v1, lane, LANES), axis=-1, keepdims=True)
        lg2 = jnp.where(lane == i1, -jnp.inf, lg)
        v2 = jnp.max(lg2, axis=-1, keepdims=True)
        i2 = jnp.min(jnp.where(lg2 == v2, lane, LANES), axis=-1, keepdims=True)
        e2 = jnp.exp(v2 - v1)
        w1 = 1.0 / (1.0 + e2)
        w2 = e2 / (1.0 + e2)

        @pl.when(pl.program_id(0) == 0)
        def _():
            count_ref[...] = jnp.zeros_like(count_ref)

        oh1 = (lane == i1).astype(F32)
        oh2 = (lane == i2).astype(F32)
        rr = lax.broadcasted_iota(jnp.int32, (TILE, TILE), 0)
        ss = lax.broadcasted_iota(jnp.int32, (TILE, TILE), 1)
        before = (ss < rr).astype(BF16)
        carry = count_ref[...]
        cnt1 = jnp.sum(oh1, axis=0, keepdims=True)
        rank1 = jnp.sum(oh1 * (_dot(before, oh1.astype(BF16)) + carry), axis=-1, keepdims=True)
        rank2 = jnp.sum(oh2 * (_dot(before, oh2.astype(BF16)) + (carry + cnt1)), axis=-1, keepdims=True)
        total = carry + cnt1 + jnp.sum(oh2, axis=0, keepdims=True)
        count_ref[...] = total
        count_out[...] = jnp.broadcast_to(total, count_out.shape)
        packed = jnp.zeros_like(logits)
        for ln, val in ((8, i1.astype(F32)), (9, i2.astype(F32)), (10, w1), (11, w2), (12, rank1), (13, rank2)):
            packed = jnp.where(lane == ln, val, packed)
        gate_out[...] = packed


def _merge(y, of, ob, on, x, gain, wbg, wbn, wo, layer, gpost, gpre, mod, router, nb, nt):
    t, d = x.shape
    row_of_tile = lambda i: jnp.where(i < nb, nb, (i - nb) // nt)
    mspec = lambda comp: pl.BlockSpec((None, None, 1, d), lambda i: (row_of_tile(i), comp, 0, 0))
    rowblk = lambda w: pl.BlockSpec((TILE, w), lambda i: (i, 0))
    ycol = lambda w, col: pl.BlockSpec((TILE, w), lambda i: (i, col // w))
    const = lambda a: pl.BlockSpec(a.shape, lambda i: (0,) * a.ndim)
    stacked = lambda a: pl.BlockSpec((None,) + a.shape[1:], lambda i: (layer,) + (0,) * (a.ndim - 1))
    in_specs = [rowblk(GLA_V), rowblk(GLA_V), ycol(GLA_V, COL_GR), ycol(d, COL_GA), ycol(d, COL_GB), rowblk(NAT_W),
                rowblk(d), const(gain), stacked(wbg), stacked(wbn), stacked(wo), const(gpost), mspec(2), const(gpre),
                mspec(3), mspec(4)]
    args = [of, ob, y, y, y, on, x, gain, wbg, wbn, wo, gpost, mod, gpre, mod, mod]
    out_specs = [rowblk(d), rowblk(d)]
    out_shape = [jax.ShapeDtypeStruct((t, d), F32), jax.ShapeDtypeStruct((t, d), BF16)]
    scratch = []
    if router is not None:
        in_specs += [const(router[0]), const(router[1])]
        args += list(router)
        out_specs += [rowblk(LANES), pl.BlockSpec((8, LANES), lambda i: (0, 0))]
        out_shape += [jax.ShapeDtypeStruct((t, LANES), F32), jax.ShapeDtypeStruct((8, LANES), F32)]
        out_shape[1] = jax.ShapeDtypeStruct((t, d), F32)
        scratch = [pltpu.VMEM((1, LANES), F32)]
    return pl.pallas_call(
        functools.partial(_merge_kernel, with_router=router is not None),
        grid=(t // TILE,),
        in_specs=in_specs,
        out_specs=out_specs,
        out_shape=out_shape,
        scratch_shapes=scratch,
        compiler_params=_params(("arbitrary",)),
        name="merge_router" if router is not None else "merge",
    )(*args)


def _ffn_kernel(h_ref, w1_ref, w3_ref, w2_ref, x_ref, gpost_ref, m5_ref, o_ref, acc_ref):
    f = pl.program_id(1)

    @pl.when(f == 0)
    def _():
        acc_ref[...] = jnp.zeros_like(acc_ref)

    h = h_ref[...]
    a = _dot(h, w1_ref[...].astype(BF16))
    u = (a * jax.nn.sigmoid(a)) * _dot(h, w3_ref[...].astype(BF16))
    acc_ref[...] += _dot(u.astype(BF16), w2_ref[...].astype(BF16))

    @pl.when(f == pl.num_programs(1) - 1)
    def _():
        o_ref[...] = x_ref[...] + m5_ref[...] * _rms(acc_ref[...], gpost_ref[...])


def _dispatch_kernel(pos_ref, h_ref, xs_in, xs_out, sem):
    del xs_in
    t = pl.num_programs(0) * TILE
    base = pl.program_id(0) * TILE

    def copies(r):
        src = h_ref.at[pl.ds(r, 1)]
        return (pltpu.make_async_copy(src, xs_out.at[pl.ds(pos_ref[base + r], 1)], sem),
                pltpu.make_async_copy(src, xs_out.at[pl.ds(pos_ref[t + base + r], 1)], sem))

    def start(r, carry):
        for queue, cp in enumerate(copies(r)):
            cp.start(priority=queue)
        return carry

    def wait(r, carry):
        for cp in copies(r):
            cp.wait()
        return carry

    lax.fori_loop(0, TILE, start, 0, unroll=8)
    lax.fori_loop(0, TILE, wait, 0, unroll=8)


def _moe_ffn_kernel(te_ref, nu_ref, x_ref, w1_ref, w3_ref, w2_ref, o_ref, xb_ref):
    del te_ref
    i = pl.program_id(0)
    f = pl.program_id(1)

    @pl.when(i < nu_ref[0])
    def _():
        @pl.when(f == 0)
        def _():
            xb_ref[...] = x_ref[...].astype(BF16)

        h = xb_ref[...]
        a = _dot(h, w1_ref[0].astype(BF16))
        u = (a * jax.nn.sigmoid(a)) * _dot(h, w3_ref[0].astype(BF16))
        y = _dot(u.astype(BF16), w2_ref[0].astype(BF16))

        @pl.when(f == 0)
        def _():
            o_ref[...] = y

        @pl.when(f > 0)
        def _():
            o_ref[...] += y

    @pl.when((i >= nu_ref[0]) & (f == 0))
    def _():
        o_ref[...] = jnp.zeros_like(o_ref)


def _combine_kernel(pos_ref, ys_hbm, g_ref, x_ref, gpost_ref, m5_ref, o_ref, buf_ref, sem, *, t, tile0):
    i = pl.program_id(0)
    n = pl.num_programs(0)

    def copies(step, slot, r):
        base = (step + tile0) * TILE
        return (pltpu.make_async_copy(ys_hbm.at[pl.ds(pos_ref[base + r], 1)],
                                      buf_ref.at[slot, 0, pl.ds(r, 1)], sem.at[slot]),
                pltpu.make_async_copy(ys_hbm.at[pl.ds(pos_ref[t + base + r], 1)],
                                      buf_ref.at[slot, 1, pl.ds(r, 1)], sem.at[slot]))

    def issue(step, slot):
        def body(r, carry):
            for queue, cp in enumerate(copies(step, slot, r)):
                cp.start(priority=queue)
            return carry
        lax.fori_loop(0, TILE, body, 0, unroll=8)

    @pl.when(i == 0)
    def _():
        issue(0, 0)

    @pl.when(i + 1 < n)
    def _():
        issue(i + 1, (i + 1) % 2)

    slot = i % 2

    def wait(r, carry):
        for cp in copies(i, slot, r):
            cp.wait()
        return carry

    lax.fori_loop(0, TILE, wait, 0, unroll=8)
    g = g_ref[...]
    y = g[:, 10:11] * buf_ref[slot, 0] + g[:, 11:12] * buf_ref[slot, 1]
    o_ref[...] = x_ref[...] + m5_ref[...] * _rms(y, gpost_ref[...])


def _ffn_tiles(t, nb, nt):
    tm = 1024 if (nb * TILE) % 1024 == 0 and (nt * TILE) % 1024 == 0 else TILE
    per = tm // TILE
    row_of_tile = lambda i: jnp.where(i * per < nb, nb, (i * per - nb) // nt)
    return tm, row_of_tile


def _ffn(h, w1, w3, w2, layer, x, gpost, mod, nb, nt):
    t, d = x.shape
    dff = w1.shape[2]
    tm, row_of_tile = _ffn_tiles(t, nb, nt)
    tf = 512
    return pl.pallas_call(
        _ffn_kernel,
        grid=(t // tm, dff // tf),
        in_specs=[pl.BlockSpec((tm, d), lambda i, f: (i, 0)),
                  pl.BlockSpec((None, d, tf), lambda i, f: (layer, 0, f)),
                  pl.BlockSpec((None, d, tf), lambda i, f: (layer, 0, f)),
                  pl.BlockSpec((None, tf, d), lambda i, f: (layer, f, 0)),
                  pl.BlockSpec((tm, d), lambda i, f: (i, 0)),
                  pl.BlockSpec((1, d), lambda i, f: (0, 0)),
                  pl.BlockSpec((None, None, 1, d), lambda i, f: (row_of_tile(i), 5, 0, 0))],
        out_specs=pl.BlockSpec((tm, d), lambda i, f: (i, 0)),
        out_shape=jax.ShapeDtypeStruct((t, d), F32),
        scratch_shapes=[pltpu.VMEM((tm, d), F32)],
        compiler_params=_params(("arbitrary", "arbitrary")),
        name="ffn_dense",
    )(h, w1, w3, w2, x, gpost, mod)


MOE_TM = 1024
MOE_TF = 512


def _moe_routed(h, pack, counts, w1, w3, w2, layer, x, gpost, mod, nb, nt, latent_only):
    t, d = x.shape
    _, ne, _, dff = w1.shape
    tm = MOE_TM
    tf = MOE_TF if dff % MOE_TF == 0 else 512
    n_tiles = -(-(2 * t) // tm) + ne
    p_rows = n_tiles * tm
    cnt = counts[0, :ne].astype(jnp.int32)
    padded = (cnt + tm - 1) // tm * tm
    ends = jnp.cumsum(padded)
    offs = ends - padded
    e1 = pack[:, 8].astype(jnp.int32)
    e2 = pack[:, 9].astype(jnp.int32)
    pos = jnp.concatenate([offs[e1] + pack[:, 12].astype(jnp.int32), offs[e2] + pack[:, 13].astype(jnp.int32)])
    n_used = (ends[-1] // tm).astype(jnp.int32).reshape(1)
    tile_start = jnp.arange(n_tiles, dtype=jnp.int32) * tm
    tile_start = jnp.minimum(tile_start, ends[-1] - tm)
    tile_expert = jnp.minimum(jnp.sum(tile_start[:, None] >= ends[None, :], axis=1), ne - 1).astype(jnp.int32)

    xs = pl.pallas_call(
        _dispatch_kernel,
        grid_spec=pltpu.PrefetchScalarGridSpec(
            num_scalar_prefetch=1, grid=(t // TILE,),
            in_specs=[pl.BlockSpec((TILE, d), lambda i, pos: (i, 0)), pl.BlockSpec(memory_space=pl.ANY)],
            out_specs=pl.BlockSpec(memory_space=pl.ANY),
            scratch_shapes=[pltpu.SemaphoreType.DMA(())]),
        out_shape=jax.ShapeDtypeStruct((p_rows, d), F32),
        input_output_aliases={2: 0},
        compiler_params=_params(("arbitrary",)),
        name="moe_dispatch",
    )(pos, h, jnp.zeros((p_rows, d), F32))

    nf = dff // tf
    row = lambda i, nu: jnp.minimum(i, nu[0] - 1)
    fcol = lambda i, f, nu: jnp.where(i < nu[0], f, nf - 1)
    ys = pl.pallas_call(
        _moe_ffn_kernel,
        grid_spec=pltpu.PrefetchScalarGridSpec(
            num_scalar_prefetch=2, grid=(n_tiles, nf),
            in_specs=[pl.BlockSpec((tm, d), lambda i, f, te, nu: (row(i, nu), 0)),
                      pl.BlockSpec((None, 1, d, tf), lambda i, f, te, nu: (layer, te[i], 0, fcol(i, f, nu))),
                      pl.BlockSpec((None, 1, d, tf), lambda i, f, te, nu: (layer, te[i], 0, fcol(i, f, nu))),
                      pl.BlockSpec((None, 1, tf, d), lambda i, f, te, nu: (layer, te[i], fcol(i, f, nu), 0))],
            out_specs=pl.BlockSpec((tm, d), lambda i, f, te, nu: (i, 0)),
            scratch_shapes=[pltpu.VMEM((tm, d), BF16)]),
        out_shape=jax.ShapeDtypeStruct((p_rows, d), F32),
        compiler_params=_params(("arbitrary", "arbitrary")),
        name="moe_ffn",
    )(tile_expert, n_used, xs, w1, w3, w2)

    tile0 = nb if latent_only else 0
    n_out = t // TILE - tile0
    row_of_tile = lambda i: jnp.where(i + tile0 < nb, nb, (i + tile0 - nb) // nt)
    return pl.pallas_call(
        functools.partial(_combine_kernel, t=t, tile0=tile0),
        grid_spec=pltpu.PrefetchScalarGridSpec(
            num_scalar_prefetch=1, grid=(n_out,),
            in_specs=[pl.BlockSpec(memory_space=pl.ANY),
                      pl.BlockSpec((TILE, LANES), lambda i, pos: (i + tile0, 0)),
                      pl.BlockSpec((TILE, d), lambda i, pos: (i + tile0, 0)),
                      pl.BlockSpec((1, d), lambda i, pos: (0, 0)),
                      pl.BlockSpec((None, None, 1, d), lambda i, pos: (row_of_tile(i), 5, 0, 0))],
            out_specs=pl.BlockSpec((TILE, d), lambda i, pos: (i, 0)),
            scratch_shapes=[pltpu.VMEM((2, 2, TILE, d), F32), pltpu.SemaphoreType.DMA((2,))]),
        out_shape=jax.ShapeDtypeStruct((n_out * TILE, d), F32),
        compiler_params=_params(("arbitrary",)),
        name="moe_combine",
    )(pos, ys, pack, x, gpost, mod)


def _reorder_w_in(w):
    a = GLA_QK * 2 + GLA_V * 2
    c0 = a + 2 * GLA_RANK
    g0 = c0 + 3 * NAT_W
    wb = w.astype(BF16)
    pad = jnp.zeros(w.shape[:2] + (LANES - 2 * GLA_RANK,), BF16)
    return jnp.concatenate([wb[..., :a], wb[..., g0:], wb[..., c0:g0], wb[..., a:c0], pad], axis=-1)


def _rope_tables(nt):
    t = np.arange(nt * TILE)
    pos = np.stack([t // GRID_W, t % GRID_W], axis=1).astype(np.float32)
    nf = GLA_DK // 4
    inv = (ROPE_BASE ** (-jnp.arange(nf, dtype=F32) / nf))
    ang = jnp.asarray(pos)[:, :, None] * inv[None, None, :]
    cos = jnp.concatenate([jnp.cos(ang), jnp.cos(ang)], axis=-1).reshape(nt * TILE, GLA_DK)
    sin = jnp.concatenate([-jnp.sin(ang), jnp.sin(ang)], axis=-1).reshape(nt * TILE, GLA_DK)
    ident = jnp.ones((TILE, GLA_DK), F32)
    return (jnp.concatenate([ident, cos], axis=0), jnp.concatenate([jnp.zeros_like(ident), sin], axis=0))


def _chunk_matrices():
    i = np.arange(TILE)
    same = (i[:, None] // GLA_CHUNK) == (i[None, :] // GLA_CHUNK)
    lower = same & (i[None, :] <= i[:, None])
    upper = same & (i[None, :] >= i[:, None])
    cum = jnp.asarray(np.stack([lower, upper]).astype(np.float32), BF16)
    return cum, jnp.asarray(same.astype(np.float32), BF16)


def _hi_lo(w):
    hi = w.astype(BF16)
    return hi, (w - hi.astype(F32)).astype(BF16)


def kernel(x, c, ctx, c_ctx, w_mod, b_mod, norm_mix_pre, norm_mix_post, w_in, gla_gate_w2, gla_gate_b, gla_norm,
           nat_rpb, w_branch_gla, w_branch_nat, w_out, norm_ffn_pre, norm_ffn_post, ffn_w1, ffn_w3, ffn_w2,
           moe_router, moe_w1, moe_w3, moe_w2):
    nb, seq, d = x.shape
    depth = w_mod.shape[0]
    assert ctx.shape[1] == TILE and seq % TILE == 0 and d % LANES == 0
    nt = seq // TILE
    mod_rows = -(-(nb + 1) // 8) * 8
    c_rows = jnp.concatenate([c, c_ctx[None], jnp.zeros((mod_rows - nb - 1, d), c.dtype)], axis=0)
    mod_all = _modulation(c_rows, w_mod, b_mod).reshape(depth, mod_rows, 6, 1, d)
    xs = jnp.concatenate([ctx.reshape(nb * TILE, d), x.reshape(nb * seq, d)], axis=0)
    rope_cos, rope_sin = _rope_tables(nt)
    cum_m, tot_m = _chunk_matrices()
    row = lambda v: v.reshape(1, -1)
    w_in_b = _reorder_w_in(w_in)
    wbg_b, wbn_b, wo_b = w_branch_gla.astype(BF16), w_branch_nat.astype(BF16), w_out.astype(BF16)
    ffn_b = (ffn_w1, ffn_w3, ffn_w2)
    moe_b = (moe_w1, moe_w3, moe_w2)
    nat_bias = _nat_bias_base(nat_rpb)
    nat_masks = _nat_row_masks(nt)
    w2p = jnp.zeros((depth, 2, LANES, GLA_QK), F32)
    w2p = w2p.at[:, 0, :GLA_RANK].set(gla_gate_w2[:, 0]).at[:, 1, GLA_RANK:2 * GLA_RANK].set(gla_gate_w2[:, 1])
    w2hi, w2lo = _hi_lo(w2p)
    wr = jnp.zeros((moe_router.shape[0], d, LANES), F32).at[:, :, :N_EXPERTS].set(moe_router)
    wr_hi, wr_lo = _hi_lo(wr)
    for i in range(depth):
        mod = mod_all[i]
        last = i == depth - 1
        y = _in_projection(xs, row(norm_mix_pre[i]), mod, rope_cos, rope_sin, w_in_b, i, nb, nt)
        of, ob = _gla(y, w2hi[i], w2lo[i], gla_gate_b[i].reshape(2, 1, GLA_QK), cum_m, tot_m, nb, nt)
        on = _nat(y, nat_bias, i, nat_masks, nb, nt)
        is_moe = i % 2 == 1
        j = i // 2
        router = (wr_hi[j], wr_lo[j]) if is_moe else None
        outs = _merge(y, of, ob, on, xs, row(gla_norm[i]), wbg_b, wbn_b, wo_b, i, row(norm_mix_post[i]),
                      row(norm_ffn_pre[i]), mod, router, nb, nt)
        if is_moe:
            xs, h, pack, counts = outs
            xs = _moe_routed(h, pack, counts, *moe_b, j, xs, row(norm_ffn_post[i]), mod, nb, nt, latent_only=last)
        else:
            xs, h = outs
            xs = _ffn(h, *ffn_b, j, xs, row(norm_ffn_post[i]), mod, nb, nt)
            if last:
                xs = xs[nb * TILE:]
    return xs.reshape(nb, seq, d)
```

```python
import functools

import numpy as np
import jax
import jax.numpy as jnp
from jax import lax
from jax.experimental import pallas as pl
from jax.experimental.pallas import tpu as pltpu

EPS = 1e-6
GRID_W = 64
TILE = 256
TILE_ROWS = TILE // GRID_W
GLA_HEADS, GLA_DK, GLA_DV, GLA_RANK, GLA_TAU, GLA_CHUNK = 4, 128, 256, 16, 16.0, 64
NAT_HEADS, NAT_DH = 8, 64
WIN_H, WIN_W = 8, 16
N_EXPERTS = 8
ROPE_BASE = 10000.0
GLA_QK = GLA_HEADS * GLA_DK
GLA_V = GLA_HEADS * GLA_DV
NAT_W = NAT_HEADS * NAT_DH
LANES = 128
COL_GQ, COL_GK, COL_GV, COL_GR, COL_GA, COL_GB = 0, 512, 1024, 2048, 3072, 4096
COL_NQ, COL_NK, COL_NV, COL_CODE = 5120, 5632, 6144, 6656
N_IN = COL_CODE + LANES
VMEM_LIMIT = 56 * 1024 * 1024
BF16 = jnp.bfloat16
F32 = jnp.float32


def _dot(a, b):
    return jnp.dot(a, b, preferred_element_type=F32)


def _dot_nt(a, b):
    return lax.dot_general(a, b, (((1,), (1,)), ((), ())), preferred_element_type=F32)


def _dot_tn(a, b):
    return lax.dot_general(a, b, (((0,), (0,)), ((), ())), preferred_element_type=F32)


def _rms(x, g):
    return x * lax.rsqrt(jnp.mean(x * x, axis=-1, keepdims=True) + EPS) * g


def _split_dot(a, b_hi, b_lo):
    a_hi = a.astype(BF16)
    a_lo = (a - a_hi.astype(F32)).astype(BF16)
    return _dot(a_hi, b_hi) + (_dot(a_lo, b_hi) + _dot(a_hi, b_lo))


def _params(sem):
    return pltpu.CompilerParams(dimension_semantics=sem, vmem_limit_bytes=VMEM_LIMIT)


def _mod_kernel(c_ref, w_ref, b_ref, o_ref):
    c = c_ref[...]
    s = c * jax.nn.sigmoid(c)
    o_ref[0] = _dot(s.astype(BF16), w_ref[0].astype(BF16)) + b_ref[0]


def _modulation(c_rows, w_mod, b_mod):
    depth, d, n = w_mod.shape
    rows = c_rows.shape[0]
    tn = 1536
    return pl.pallas_call(
        _mod_kernel,
        grid=(depth, n // tn),
        in_specs=[pl.BlockSpec((rows, d), lambda l, j: (0, 0)),
                  pl.BlockSpec((1, d, tn), lambda l, j: (l, 0, j)),
                  pl.BlockSpec((1, 1, tn), lambda l, j: (l, 0, j))],
        out_specs=pl.BlockSpec((1, rows, tn), lambda l, j: (l, 0, j)),
        out_shape=jax.ShapeDtypeStruct((depth, rows, n), F32),
        compiler_params=_params(("arbitrary", "arbitrary")),
        name="modulation",
    )(c_rows, w_mod, b_mod.reshape(depth, 1, n))


def _inproj_kernel(x_ref, g_ref, sh_ref, sc_ref, cos_ref, sin_ref, w_ref, o_ref):
    h = _rms(x_ref[...], g_ref[...]) * (1.0 + sc_ref[...]) + sh_ref[...]
    hb = h.astype(BF16)
    n = o_ref.shape[1]
    qk = _dot(hb, w_ref[:, :2 * GLA_QK])
    lane = lax.broadcasted_iota(jnp.int32, qk.shape, 1)
    reps = 2 * GLA_HEADS
    cos = jnp.concatenate([cos_ref[...]] * reps, axis=1)
    sin = jnp.concatenate([sin_ref[...]] * reps, axis=1)
    partner = jnp.where((lane % 64) < 32, pltpu.roll(qk, 2 * GLA_QK - 32, 1), pltpu.roll(qk, 32, 1))
    qk = qk * cos + partner * sin
    qk = jnp.where(lane < GLA_QK, qk * (GLA_DK ** -0.5), qk)
    o_ref[:, :2 * GLA_QK] = qk.astype(o_ref.dtype)
    step = 1024
    for j in range(2 * GLA_QK, n, step):
        w = min(step, n - j)
        o_ref[:, j:j + w] = _dot(hb, w_ref[:, j:j + w]).astype(o_ref.dtype)


def _in_projection(x, g, mod, rope_cos, rope_sin, w, layer, nb, nt):
    t, d = x.shape
    n = w.shape[2]
    row_of_tile = lambda i: jnp.where(i < nb, nb, (i - nb) // nt)
    rope_tile = lambda i: jnp.where(i < nb, 0, 1 + (i - nb) % nt)
    dm = mod.shape[-1]
    mspec = lambda comp: pl.BlockSpec((None, None, 1, dm), lambda i: (row_of_tile(i), comp, 0, 0))
    return pl.pallas_call(
        _inproj_kernel,
        grid=(t // TILE,),
        in_specs=[pl.BlockSpec((TILE, d), lambda i: (i, 0)),
                  pl.BlockSpec((1, d), lambda i: (0, 0)),
                  mspec(0), mspec(1),
                  pl.BlockSpec((TILE, LANES), lambda i: (rope_tile(i), 0)),
                  pl.BlockSpec((TILE, LANES), lambda i: (rope_tile(i), 0)),
                  pl.BlockSpec((None, d, n), lambda i: (layer, 0, 0))],
        out_specs=pl.BlockSpec((TILE, n), lambda i: (i, 0)),
        out_shape=jax.ShapeDtypeStruct((t, n), BF16),
        compiler_params=_params(("arbitrary",)),
        name="in_projection",
    )(x, g, mod, mod, rope_cos, rope_sin, w)


def _gla_prepare(d, q_ref, k_ref, code_ref, w2hi_ref, w2lo_ref, gb_ref, cum_ref, tot_ref, ops_ref):
    z = _dot(code_ref[...], w2hi_ref[d]) + _dot(code_ref[...], w2lo_ref[d]) + gb_ref[d]
    lg = (jnp.minimum(z, 0.0) - jnp.log(1.0 + jnp.exp(-jnp.abs(z)))) * (1.0 / GLA_TAU)
    lg_hi = lg.astype(BF16)
    lg_lo = (lg - lg_hi.astype(F32)).astype(BF16)
    cum = _dot(cum_ref[d], lg_hi) + _dot(cum_ref[d], lg_lo)
    tot = _dot(tot_ref[...], lg_hi) + _dot(tot_ref[...], lg_lo)
    k = k_ref[...].astype(F32)
    qe = q_ref[...].astype(F32) * jnp.exp(cum)
    kd = k * jnp.exp(tot - cum)
    ops_ref[d, OP_QE] = qe.astype(BF16)
    ops_ref[d, OP_KE] = (k * jnp.exp(-cum)).astype(BF16)
    n = TILE // GLA_CHUNK
    order = list(range(n)) if d == 0 else list(range(n - 1, -1, -1))
    tc = [tot[c * GLA_CHUNK:c * GLA_CHUNK + 1, :] for c in order]
    zero = jnp.zeros_like(tc[0])
    u_log = [zero, -tc[1], zero, tc[2]]
    w_log = [tc[1], zero, -tc[2], zero]
    for p, c in enumerate(order):
        rows = slice(c * GLA_CHUNK, (c + 1) * GLA_CHUNK)
        before = sum(tc[:p], zero)
        after = sum(tc[p + 1:], zero)
        ops_ref[d, OP_QU, rows] = (qe[rows] * jnp.exp(u_log[p])).astype(BF16)
        ops_ref[d, OP_KW, rows] = (kd[rows] * jnp.exp(w_log[p])).astype(BF16)
        ops_ref[d, OP_QP, rows] = (qe[rows] * jnp.exp(before)).astype(BF16)
        ops_ref[d, OP_KS, rows] = (kd[rows] * jnp.exp(after)).astype(BF16)
    return jnp.exp(sum(tc, zero))


OP_QE, OP_KE, OP_QU, OP_KW, OP_QP, OP_KS = range(6)


def _gla_kernel(qf, kf, vf, cf, qb, kb, vb, cb, w2hi, w2lo, gb, cum, tot, of, ob, st_ref, ops_ref):
    assert TILE // GLA_CHUNK == 4

    @pl.when(pl.program_id(1) == 0)
    def _():
        st_ref[...] = jnp.zeros_like(st_ref)

    decay = [_gla_prepare(0, qf, kf, cf, w2hi, w2lo, gb, cum, tot, ops_ref),
             _gla_prepare(1, qb, kb, cb, w2hi, w2lo, gb, cum, tot, ops_ref)]
    r = lax.broadcasted_iota(jnp.int32, (TILE, TILE), 0)
    s = lax.broadcasted_iota(jnp.int32, (TILE, TILE), 1)
    for d, (v_ref, o_ref) in enumerate(((vf, of), (vb, ob))):
        gap = (r // GLA_CHUNK - s // GLA_CHUNK) * (1 if d == 0 else -1)
        diag = (gap == 0) & ((s <= r) if d == 0 else (s >= r))
        for h in range(GLA_HEADS):
            kcols = slice(h * GLA_DK, (h + 1) * GLA_DK)
            vcols = slice(h * GLA_DV, (h + 1) * GLA_DV)
            op = lambda which: ops_ref[d, which, :, kcols]
            a = jnp.where(diag, _dot_nt(op(OP_QE), op(OP_KE)),
                          jnp.where(gap >= 1, _dot_nt(op(OP_QU), op(OP_KW)), 0.0))
            v = v_ref[:, vcols]
            st = st_ref[d, h]
            o = _dot(a.astype(BF16), v) + _dot_nt(op(OP_QP), st.astype(BF16))
            o_ref[:, vcols] = o.astype(o_ref.dtype)
            st_ref[d, h] = st * decay[d][:, kcols] + _dot_tn(v, op(OP_KS))


def _gla(y, w2hi, w2lo, gbias, cum_m, tot_m, nb, nt):
    t = y.shape[0]
    fwd = lambda b, j: jnp.where(j == 0, b, nb + b * nt + j - 1)
    bwd = lambda b, j: jnp.where(j == 0, b, nb + b * nt + nt - j)

    def ysl(width, col, tile):
        return pl.BlockSpec((TILE, width), lambda b, j: (tile(b, j), col // width))

    def direction(tile):
        return [ysl(GLA_QK, COL_GQ, tile), ysl(GLA_QK, COL_GK, tile), ysl(GLA_V, COL_GV, tile),
                ysl(LANES, COL_CODE, tile)]

    const = lambda shape: pl.BlockSpec(shape, lambda b, j: (0,) * len(shape))
    in_specs = (direction(fwd) + direction(bwd)
                + [const(w2hi.shape), const(w2lo.shape), const(gbias.shape), const(cum_m.shape), const(tot_m.shape)])
    args = [y] * 8 + [w2hi, w2lo, gbias, cum_m, tot_m]
    return pl.pallas_call(
        _gla_kernel,
        grid=(nb, nt + 1),
        in_specs=in_specs,
        out_specs=[pl.BlockSpec((TILE, GLA_V), lambda b, j: (fwd(b, j), 0)),
                   pl.BlockSpec((TILE, GLA_V), lambda b, j: (bwd(b, j), 0))],
        out_shape=[jax.ShapeDtypeStruct((t, GLA_V), BF16)] * 2,
        scratch_shapes=[pltpu.VMEM((2, GLA_HEADS, GLA_DV, GLA_DK), F32),
                        pltpu.VMEM((2, 6, TILE, GLA_QK), BF16)],
        compiler_params=_params(("arbitrary", "arbitrary")),
        name="gla_scan",
    )(*args)


def _nat_kernel(q_ref, k0, k1, k2, kc, v0, v1, v2, vc, bias_ref, mask_ref, o_ref):
    lane = lax.broadcasted_iota(jnp.int32, (TILE, LANES), 1)
    low = lane < NAT_DH
    row_mask = mask_ref[...]
    for p in range(NAT_HEADS // 2):
        cols = slice(p * LANES, (p + 1) * LANES)
        qp = q_ref[:, cols] * (NAT_DH ** -0.5)
        ks = [r[:, cols] for r in (k0, k1, k2, kc)]
        vs = [r[:, cols] for r in (v0, v1, v2, vc)]
        outs = []
        for half in range(2):
            qh = jnp.where(low if half == 0 else ~low, qp, jnp.zeros_like(qp))
            s_lat = jnp.concatenate([_dot_nt(qh, kk) for kk in ks[:3]], axis=1) + (bias_ref[2 * p + half] + row_mask)
            s_ctx = _dot_nt(qh, ks[3])
            m = jnp.maximum(jnp.max(s_lat, axis=-1, keepdims=True), jnp.max(s_ctx, axis=-1, keepdims=True))
            e_lat = jnp.exp(s_lat - m)
            e_ctx = jnp.exp(s_ctx - m)
            den = jnp.sum(e_lat, axis=-1, keepdims=True) + jnp.sum(e_ctx, axis=-1, keepdims=True)
            e_lat = e_lat.astype(BF16)
            acc = _dot(e_ctx.astype(BF16), vs[3])
            for i in range(3):
                acc += _dot(e_lat[:, i * TILE:(i + 1) * TILE], vs[i])
            outs.append(acc / den)
        o_ref[:, cols] = jnp.where(low, outs[0], outs[1]).astype(o_ref.dtype)


def _nat(y, bias, layer, row_masks, nb, nt):
    t = y.shape[0]
    lat = lambda b, tt: nb + b * nt + jnp.clip(tt, 0, nt - 1)
    qtile = lambda b, j: jnp.where(j == 0, b, nb + b * nt + j - 1)
    pattern = lambda b, j: jnp.where(j == 0, 3, jnp.where(j == 1, 0, jnp.where(j == nt, 2, 1)))

    def ysl(col, tile):
        return pl.BlockSpec((TILE, NAT_W), lambda b, j: (tile(b, j), col // NAT_W))

    slab = [lambda b, j: lat(b, j - 2), lambda b, j: lat(b, j - 1), lambda b, j: lat(b, j), lambda b, j: b]
    in_specs = ([ysl(COL_NQ, qtile)] + [ysl(COL_NK, s) for s in slab] + [ysl(COL_NV, s) for s in slab]
                + [pl.BlockSpec((None, NAT_HEADS, TILE, 3 * TILE), lambda b, j: (layer, 0, 0, 0)),
                   pl.BlockSpec((None, TILE, 3 * TILE), lambda b, j: (pattern(b, j), 0, 0))])
    return pl.pallas_call(
        _nat_kernel,
        grid=(nb, nt + 1),
        in_specs=in_specs,
        out_specs=pl.BlockSpec((TILE, NAT_W), lambda b, j: (qtile(b, j), 0)),
        out_shape=jax.ShapeDtypeStruct((t, NAT_W), BF16),
        compiler_params=_params(("arbitrary", "arbitrary")),
        name="nat_attention",
    )(*([y] * 9), bias, row_masks)


def _nat_bias_base(rpb):
    nr, nc = 2 * WIN_H - 1, 2 * WIN_W - 1
    qi = np.arange(TILE_ROWS)
    kj = np.arange(3 * TILE_ROWS)
    qc = np.arange(GRID_W)
    kc = np.arange(GRID_W)
    col_start = np.clip(qc - WIN_W // 2, 0, GRID_W - WIN_W)
    col_ok = (kc[None, :] >= col_start[:, None]) & (kc[None, :] < col_start[:, None] + WIN_W)
    col_off = np.clip(kc[None, :] - qc[:, None], 1 - WIN_W, WIN_W - 1) + (WIN_W - 1)
    row_off = kj[None, :] - qi[:, None] + (WIN_H - 1 - TILE_ROWS)
    assert row_off.min() >= 0 and row_off.max() < nr
    col_sel = ((col_off[None] == np.arange(nc)[:, None, None]) & col_ok[None]).astype(np.float32)
    row_sel = (row_off[None] == np.arange(nr)[:, None, None]).astype(np.float32)
    b = jnp.einsum("dhab,aik,bcl->dhickl", rpb.astype(F32), row_sel, col_sel, precision=lax.Precision.HIGHEST)
    b = b + jnp.where(jnp.asarray(col_ok), 0.0, -jnp.inf)[None, None, None, :, None, :]
    return b.reshape(rpb.shape[0], NAT_HEADS, TILE, 3 * TILE)


def _nat_row_masks(nt):
    rows = nt * TILE_ROWS
    kh = min(WIN_H, rows)
    qi = np.arange(TILE_ROWS)
    kj = np.arange(3 * TILE_ROWS)
    masks = []
    for t in (0, min(1, nt - 1), nt - 1):
        r = TILE_ROWS * t + qi
        row_start = np.clip(r - kh // 2, 0, rows - kh)
        kr = TILE_ROWS * (t - 1) + kj
        valid = (kr >= 0) & (kr < rows)
        masks.append((kr[None, :] >= row_start[:, None]) & (kr[None, :] < row_start[:, None] + kh) & valid[None, :])
    masks.append(np.zeros_like(masks[0]))
    m = np.where(np.stack(masks), 0.0, -np.inf).astype(np.float32)
    m = jnp.broadcast_to(jnp.asarray(m)[:, :, None, :, None], (4, TILE_ROWS, GRID_W, 3 * TILE_ROWS, GRID_W))
    return m.reshape(4, TILE, 3 * TILE)


def _merge_kernel(of_ref, ob_ref, gr_ref, ga_ref, gb_ref, on_ref, x_ref, gain_ref, wbg_ref, wbn_ref, wo_ref,
                  gpost_ref, m2_ref, gpre_ref, m3_ref, m4_ref, *rest, with_router):
    if with_router:
        wrhi_ref, wrlo_ref, x_out, h_out, gate_out, count_out, count_ref = rest
    else:
        x_out, h_out = rest
    o = of_ref[...].astype(F32) + ob_ref[...].astype(F32)
    parts = []
    for h in range(GLA_HEADS):
        oh = o[:, h * GLA_DV:(h + 1) * GLA_DV]
        parts.append(oh * lax.rsqrt(jnp.mean(oh * oh, axis=-1, keepdims=True) + EPS))
    r = gr_ref[...].astype(F32)
    yg = jnp.concatenate(parts, axis=1) * gain_ref[...] * (r * jax.nn.sigmoid(r))
    m = (jax.nn.sigmoid(ga_ref[...].astype(F32)) * _dot(yg.astype(BF16), wbg_ref[...])
         + jax.nn.sigmoid(gb_ref[...].astype(F32)) * _dot(on_ref[...], wbn_ref[...]))
    y = _dot(m.astype(BF16), wo_ref[...])
    x1 = x_ref[...] + m2_ref[...] * _rms(y, gpost_ref[...])
    x_out[...] = x1
    h = _rms(x1, gpre_ref[...]) * (1.0 + m4_ref[...]) + m3_ref[...]
    h_out[...] = h.astype(h_out.dtype)
    if with_router:
        logits = _split_dot(h, wrhi_ref[...], wrlo_ref[...])
        lane = lax.broadcasted_iota(jnp.int32, logits.shape, 1)
        lg = jnp.where(lane < N_EXPERTS, logits, -jnp.inf)
        v1 = jnp.max(lg, axis=-1, keepdims=True)
        i1 = jnp.min(jnp.where(lg == v1, lane, LANES), axis=-1, keepdims=True)
        lg2 = jnp.where(lane == i1, -jnp.inf, lg)
        v2 = jnp.max(lg2, axis=-1, keepdims=True)
        i2 = jnp.min(jnp.where(lg2 == v2, lane, LANES), axis=-1, keepdims=True)
        e2 = jnp.exp(v2 - v1)
        w1 = 1.0 / (1.0 + e2)
        w2 = e2 / (1.0 + e2)

        @pl.when(pl.program_id(0) == 0)
        def _():
            count_ref[...] = jnp.zeros_like(count_ref)

        oh1 = (lane == i1).astype(F32)
        oh2 = (lane == i2).astype(F32)
        rr = lax.broadcasted_iota(jnp.int32, (TILE, TILE), 0)
        ss = lax.broadcasted_iota(jnp.int32, (TILE, TILE), 1)
        before = (ss < rr).astype(BF16)
        carry = count_ref[...]
        cnt1 = jnp.sum(oh1, axis=0, keepdims=True)
        rank1 = jnp.sum(oh1 * (_dot(before, oh1.astype(BF16)) + carry), axis=-1, keepdims=True)
        rank2 = jnp.sum(oh2 * (_dot(before, oh2.astype(BF16)) + (carry + cnt1)), axis=-1, keepdims=True)
        total = carry + cnt1 + jnp.sum(oh2, axis=0, keepdims=True)
        count_ref[...] = total
        count_out[...] = jnp.broadcast_to(total, count_out.shape)
        packed = jnp.zeros_like(logits)
        for ln, val in ((8, i1.astype(F32)), (9, i2.astype(F32)), (10, w1), (11, w2), (12, rank1), (13, rank2)):
            packed = jnp.where(lane == ln, val, packed)
        gate_out[...] = packed


def _merge(y, of, ob, on, x, gain, wbg, wbn, wo, layer, gpost, gpre, mod, router, nb, nt):
    t, d = x.shape
    row_of_tile = lambda i: jnp.where(i < nb, nb, (i - nb) // nt)
    mspec = lambda comp: pl.BlockSpec((None, None, 1, d), lambda i: (row_of_tile(i), comp, 0, 0))
    rowblk = lambda w: pl.BlockSpec((TILE, w), lambda i: (i, 0))
    ycol = lambda w, col: pl.BlockSpec((TILE, w), lambda i: (i, col // w))
    const = lambda a: pl.BlockSpec(a.shape, lambda i: (0,) * a.ndim)
    stacked = lambda a: pl.BlockSpec((None,) + a.shape[1:], lambda i: (layer,) + (0,) * (a.ndim - 1))
    in_specs = [rowblk(GLA_V), rowblk(GLA_V), ycol(GLA_V, COL_GR), ycol(d, COL_GA), ycol(d, COL_GB), rowblk(NAT_W),
                rowblk(d), const(gain), stacked(wbg), stacked(wbn), stacked(wo), const(gpost), mspec(2), const(gpre),
                mspec(3), mspec(4)]
    args = [of, ob, y, y, y, on, x, gain, wbg, wbn, wo, gpost, mod, gpre, mod, mod]
    out_specs = [rowblk(d), rowblk(d)]
    out_shape = [jax.ShapeDtypeStruct((t, d), F32), jax.ShapeDtypeStruct((t, d), BF16)]
    scratch = []
    if router is not None:
        in_specs += [const(router[0]), const(router[1])]
        args += list(router)
        out_specs += [rowblk(LANES), pl.BlockSpec((8, LANES), lambda i: (0, 0))]
        out_shape += [jax.ShapeDtypeStruct((t, LANES), F32), jax.ShapeDtypeStruct((8, LANES), F32)]
        out_shape[1] = jax.ShapeDtypeStruct((t, d), F32)
        scratch = [pltpu.VMEM((1, LANES), F32)]
    return pl.pallas_call(
        functools.partial(_merge_kernel, with_router=router is not None),
        grid=(t // TILE,),
        in_specs=in_specs,
        out_specs=out_specs,
        out_shape=out_shape,
        scratch_shapes=scratch,
        compiler_params=_params(("arbitrary",)),
        name="merge_router" if router is not None else "merge",
    )(*args)


def _ffn_kernel(h_ref, w1_ref, w3_ref, w2_ref, x_ref, gpost_ref, m5_ref, o_ref, acc_ref):
    f = pl.program_id(1)

    @pl.when(f == 0)
    def _():
        acc_ref[...] = jnp.zeros_like(acc_ref)

    h = h_ref[...]
    a = _dot(h, w1_ref[...].astype(BF16))
    u = (a * jax.nn.sigmoid(a)) * _dot(h, w3_ref[...].astype(BF16))
    acc_ref[...] += _dot(u.astype(BF16), w2_ref[...].astype(BF16))

    @pl.when(f == pl.num_programs(1) - 1)
    def _():
        o_ref[...] = x_ref[...] + m5_ref[...] * _rms(acc_ref[...], gpost_ref[...])


def _dispatch_kernel(pos_ref, h_ref, xs_in, xs_out, sem):
    del xs_in
    t = pl.num_programs(0) * TILE
    base = pl.program_id(0) * TILE

    def copies(r):
        src = h_ref.at[pl.ds(r, 1)]
        return (pltpu.make_async_copy(src, xs_out.at[pl.ds(pos_ref[base + r], 1)], sem),
                pltpu.make_async_copy(src, xs_out.at[pl.ds(pos_ref[t + base + r], 1)], sem))

    def start(r, carry):
        for queue, cp in enumerate(copies(r)):
            cp.start(priority=queue)
        return carry

    def wait(r, carry):
        for cp in copies(r):
            cp.wait()
        return carry

    for r in range(TILE):
        start(r, 0)
    lax.fori_loop(0, TILE, wait, 0, unroll=8)


def _moe_ffn_kernel(te_ref, nu_ref, x_ref, w1_ref, w3_ref, w2_ref, o_ref, xb_ref):
    del te_ref
    i = pl.program_id(0)
    f = pl.program_id(1)

    @pl.when(i < nu_ref[0])
    def _():
        @pl.when(f == 0)
        def _():
            xb_ref[...] = x_ref[...].astype(BF16)

        h = xb_ref[...]
        a = _dot(h, w1_ref[0].astype(BF16))
        u = (a * jax.nn.sigmoid(a)) * _dot(h, w3_ref[0].astype(BF16))
        y = _dot(u.astype(BF16), w2_ref[0].astype(BF16))

        @pl.when(f == 0)
        def _():
            o_ref[...] = y

        @pl.when(f > 0)
        def _():
            o_ref[...] += y

    @pl.when((i >= nu_ref[0]) & (f == 0))
    def _():
        o_ref[...] = jnp.zeros_like(o_ref)


def _combine_kernel(pos_ref, ys_hbm, g_ref, x_ref, gpost_ref, m5_ref, o_ref, buf_ref, sem, *, t, tile0):
    i = pl.program_id(0)
    n = pl.num_programs(0)

    def copies(step, slot, r):
        base = (step + tile0) * TILE
        return (pltpu.make_async_copy(ys_hbm.at[pl.ds(pos_ref[base + r], 1)],
                                      buf_ref.at[slot, 0, pl.ds(r, 1)], sem.at[slot]),
                pltpu.make_async_copy(ys_hbm.at[pl.ds(pos_ref[t + base + r], 1)],
                                      buf_ref.at[slot, 1, pl.ds(r, 1)], sem.at[slot]))

    def issue(step, slot):
        for r in range(TILE):
            for queue, cp in enumerate(copies(step, slot, r)):
                cp.start(priority=queue)

    @pl.when(i == 0)
    def _():
        issue(0, 0)

    @pl.when(i + 1 < n)
    def _():
        issue(i + 1, (i + 1) % 2)

    slot = i % 2

    def wait(r, carry):
        for cp in copies(i, slot, r):
            cp.wait()
        return carry

    lax.fori_loop(0, TILE, wait, 0, unroll=8)
    g = g_ref[...]
    y = g[:, 10:11] * buf_ref[slot, 0] + g[:, 11:12] * buf_ref[slot, 1]
    o_ref[...] = x_ref[...] + m5_ref[...] * _rms(y, gpost_ref[...])


def _ffn_tiles(t, nb, nt):
    tm = 1024 if (nb * TILE) % 1024 == 0 and (nt * TILE) % 1024 == 0 else TILE
    per = tm // TILE
    row_of_tile = lambda i: jnp.where(i * per < nb, nb, (i * per - nb) // nt)
    return tm, row_of_tile


def _ffn(h, w1, w3, w2, layer, x, gpost, mod, nb, nt):
    t, d = x.shape
    dff = w1.shape[2]
    tm, row_of_tile = _ffn_tiles(t, nb, nt)
    tf = 512
    return pl.pallas_call(
        _ffn_kernel,
        grid=(t // tm, dff // tf),
        in_specs=[pl.BlockSpec((tm, d), lambda i, f: (i, 0)),
                  pl.BlockSpec((None, d, tf), lambda i, f: (layer, 0, f)),
                  pl.BlockSpec((None, d, tf), lambda i, f: (layer, 0, f)),
                  pl.BlockSpec((None, tf, d), lambda i, f: (layer, f, 0)),
                  pl.BlockSpec((tm, d), lambda i, f: (i, 0)),
                  pl.BlockSpec((1, d), lambda i, f: (0, 0)),
                  pl.BlockSpec((None, None, 1, d), lambda i, f: (row_of_tile(i), 5, 0, 0))],
        out_specs=pl.BlockSpec((tm, d), lambda i, f: (i, 0)),
        out_shape=jax.ShapeDtypeStruct((t, d), F32),
        scratch_shapes=[pltpu.VMEM((tm, d), F32)],
        compiler_params=_params(("arbitrary", "arbitrary")),
        name="ffn_dense",
    )(h, w1, w3, w2, x, gpost, mod)


MOE_TM = 1024
MOE_TF = 512


def _moe_routed(h, pack, counts, w1, w3, w2, layer, x, gpost, mod, nb, nt, latent_only):
    t, d = x.shape
    _, ne, _, dff = w1.shape
    tm = MOE_TM
    tf = MOE_TF if dff % MOE_TF == 0 else 512
    n_tiles = -(-(2 * t) // tm) + ne
    p_rows = n_tiles * tm
    cnt = counts[0, :ne].astype(jnp.int32)
    padded = (cnt + tm - 1) // tm * tm
    ends = jnp.cumsum(padded)
    offs = ends - padded
    e1 = pack[:, 8].astype(jnp.int32)
    e2 = pack[:, 9].astype(jnp.int32)
    pos = jnp.concatenate([offs[e1] + pack[:, 12].astype(jnp.int32), offs[e2] + pack[:, 13].astype(jnp.int32)])
    n_used = (ends[-1] // tm).astype(jnp.int32).reshape(1)
    tile_start = jnp.arange(n_tiles, dtype=jnp.int32) * tm
    tile_start = jnp.minimum(tile_start, ends[-1] - tm)
    tile_expert = jnp.minimum(jnp.sum(tile_start[:, None] >= ends[None, :], axis=1), ne - 1).astype(jnp.int32)

    xs = pl.pallas_call(
        _dispatch_kernel,
        grid_spec=pltpu.PrefetchScalarGridSpec(
            num_scalar_prefetch=1, grid=(t // TILE,),
            in_specs=[pl.BlockSpec((TILE, d), lambda i, pos: (i, 0)), pl.BlockSpec(memory_space=pl.ANY)],
            out_specs=pl.BlockSpec(memory_space=pl.ANY),
            scratch_shapes=[pltpu.SemaphoreType.DMA(())]),
        out_shape=jax.ShapeDtypeStruct((p_rows, d), F32),
        input_output_aliases={2: 0},
        compiler_params=_params(("arbitrary",)),
        name="moe_dispatch",
    )(pos, h, jnp.zeros((p_rows, d), F32))

    nf = dff // tf
    row = lambda i, nu: jnp.minimum(i, nu[0] - 1)
    fcol = lambda i, f, nu: jnp.where(i < nu[0], f, nf - 1)
    ys = pl.pallas_call(
        _moe_ffn_kernel,
        grid_spec=pltpu.PrefetchScalarGridSpec(
            num_scalar_prefetch=2, grid=(n_tiles, nf),
            in_specs=[pl.BlockSpec((tm, d), lambda i, f, te, nu: (row(i, nu), 0)),
                      pl.BlockSpec((None, 1, d, tf), lambda i, f, te, nu: (layer, te[i], 0, fcol(i, f, nu))),
                      pl.BlockSpec((None, 1, d, tf), lambda i, f, te, nu: (layer, te[i], 0, fcol(i, f, nu))),
                      pl.BlockSpec((None, 1, tf, d), lambda i, f, te, nu: (layer, te[i], fcol(i, f, nu), 0))],
            out_specs=pl.BlockSpec((tm, d), lambda i, f, te, nu: (i, 0)),
            scratch_shapes=[pltpu.VMEM((tm, d), BF16)]),
        out_shape=jax.ShapeDtypeStruct((p_rows, d), F32),
        compiler_params=_params(("arbitrary", "arbitrary")),
        name="moe_ffn",
    )(tile_expert, n_used, xs, w1, w3, w2)

    tile0 = nb if latent_only else 0
    n_out = t // TILE - tile0
    row_of_tile = lambda i: jnp.where(i + tile0 < nb, nb, (i + tile0 - nb) // nt)
    return pl.pallas_call(
        functools.partial(_combine_kernel, t=t, tile0=tile0),
        grid_spec=pltpu.PrefetchScalarGridSpec(
            num_scalar_prefetch=1, grid=(n_out,),
            in_specs=[pl.BlockSpec(memory_space=pl.ANY),
                      pl.BlockSpec((TILE, LANES), lambda i, pos: (i + tile0, 0)),
                      pl.BlockSpec((TILE, d), lambda i, pos: (i + tile0, 0)),
                      pl.BlockSpec((1, d), lambda i, pos: (0, 0)),
                      pl.BlockSpec((None, None, 1, d), lambda i, pos: (row_of_tile(i), 5, 0, 0))],
            out_specs=pl.BlockSpec((TILE, d), lambda i, pos: (i, 0)),
            scratch_shapes=[pltpu.VMEM((2, 2, TILE, d), F32), pltpu.SemaphoreType.DMA((2,))]),
        out_shape=jax.ShapeDtypeStruct((n_out * TILE, d), F32),
        compiler_params=_params(("arbitrary",)),
        name="moe_combine",
    )(pos, ys, pack, x, gpost, mod)


def _reorder_w_in(w):
    a = GLA_QK * 2 + GLA_V * 2
    c0 = a + 2 * GLA_RANK
    g0 = c0 + 3 * NAT_W
    wb = w.astype(BF16)
    pad = jnp.zeros(w.shape[:2] + (LANES - 2 * GLA_RANK,), BF16)
    return jnp.concatenate([wb[..., :a], wb[..., g0:], wb[..., c0:g0], wb[..., a:c0], pad], axis=-1)


def _rope_tables(nt):
    t = np.arange(nt * TILE)
    pos = np.stack([t // GRID_W, t % GRID_W], axis=1).astype(np.float32)
    nf = GLA_DK // 4
    inv = (ROPE_BASE ** (-jnp.arange(nf, dtype=F32) / nf))
    ang = jnp.asarray(pos)[:, :, None] * inv[None, None, :]
    cos = jnp.concatenate([jnp.cos(ang), jnp.cos(ang)], axis=-1).reshape(nt * TILE, GLA_DK)
    sin = jnp.concatenate([-jnp.sin(ang), jnp.sin(ang)], axis=-1).reshape(nt * TILE, GLA_DK)
    ident = jnp.ones((TILE, GLA_DK), F32)
    return (jnp.concatenate([ident, cos], axis=0), jnp.concatenate([jnp.zeros_like(ident), sin], axis=0))


def _chunk_matrices():
    i = np.arange(TILE)
    same = (i[:, None] // GLA_CHUNK) == (i[None, :] // GLA_CHUNK)
    lower = same & (i[None, :] <= i[:, None])
    upper = same & (i[None, :] >= i[:, None])
    cum = jnp.asarray(np.stack([lower, upper]).astype(np.float32), BF16)
    return cum, jnp.asarray(same.astype(np.float32), BF16)


def _hi_lo(w):
    hi = w.astype(BF16)
    return hi, (w - hi.astype(F32)).astype(BF16)


def kernel(x, c, ctx, c_ctx, w_mod, b_mod, norm_mix_pre, norm_mix_post, w_in, gla_gate_w2, gla_gate_b, gla_norm,
           nat_rpb, w_branch_gla, w_branch_nat, w_out, norm_ffn_pre, norm_ffn_post, ffn_w1, ffn_w3, ffn_w2,
           moe_router, moe_w1, moe_w3, moe_w2):
    nb, seq, d = x.shape
    depth = w_mod.shape[0]
    assert ctx.shape[1] == TILE and seq % TILE == 0 and d % LANES == 0
    nt = seq // TILE
    mod_rows = -(-(nb + 1) // 8) * 8
    c_rows = jnp.concatenate([c, c_ctx[None], jnp.zeros((mod_rows - nb - 1, d), c.dtype)], axis=0)
    mod_all = _modulation(c_rows, w_mod, b_mod).reshape(depth, mod_rows, 6, 1, d)
    xs = jnp.concatenate([ctx.reshape(nb * TILE, d), x.reshape(nb * seq, d)], axis=0)
    rope_cos, rope_sin = _rope_tables(nt)
    cum_m, tot_m = _chunk_matrices()
    row = lambda v: v.reshape(1, -1)
    w_in_b = _reorder_w_in(w_in)
    wbg_b, wbn_b, wo_b = w_branch_gla.astype(BF16), w_branch_nat.astype(BF16), w_out.astype(BF16)
    ffn_b = (ffn_w1, ffn_w3, ffn_w2)
    moe_b = (moe_w1, moe_w3, moe_w2)
    nat_bias = _nat_bias_base(nat_rpb)
    nat_masks = _nat_row_masks(nt)
    w2p = jnp.zeros((depth, 2, LANES, GLA_QK), F32)
    w2p = w2p.at[:, 0, :GLA_RANK].set(gla_gate_w2[:, 0]).at[:, 1, GLA_RANK:2 * GLA_RANK].set(gla_gate_w2[:, 1])
    w2hi, w2lo = _hi_lo(w2p)
    wr = jnp.zeros((moe_router.shape[0], d, LANES), F32).at[:, :, :N_EXPERTS].set(moe_router)
    wr_hi, wr_lo = _hi_lo(wr)
    for i in range(depth):
        mod = mod_all[i]
        last = i == depth - 1
        y = _in_projection(xs, row(norm_mix_pre[i]), mod, rope_cos, rope_sin, w_in_b, i, nb, nt)
        of, ob = _gla(y, w2hi[i], w2lo[i], gla_gate_b[i].reshape(2, 1, GLA_QK), cum_m, tot_m, nb, nt)
        on = _nat(y, nat_bias, i, nat_masks, nb, nt)
        is_moe = i % 2 == 1
        j = i // 2
        router = (wr_hi[j], wr_lo[j]) if is_moe else None
        outs = _merge(y, of, ob, on, xs, row(gla_norm[i]), wbg_b, wbn_b, wo_b, i, row(norm_mix_post[i]),
                      row(norm_ffn_pre[i]), mod, router, nb, nt)
        if is_moe:
            xs, h, pack, counts = outs
            xs = _moe_routed(h, pack, counts, *moe_b, j, xs, row(norm_ffn_post[i]), mod, nb, nt, latent_only=last)
        else:
            xs, h = outs
            xs = _ffn(h, *ffn_b, j, xs, row(norm_ffn_post[i]), mod, nb, nt)
            if last:
                xs = xs[nb * TILE:]
    return xs.reshape(nb, seq, d)
```

```python
import functools

import numpy as np
import jax
import jax.numpy as jnp
from jax import lax
from jax.experimental import pallas as pl
from jax.experimental.pallas import tpu as pltpu

EPS = 1e-6
GRID_W = 64
TILE = 256
TILE_ROWS = TILE // GRID_W
GLA_HEADS, GLA_DK, GLA_DV, GLA_RANK, GLA_TAU, GLA_CHUNK = 4, 128, 256, 16, 16.0, 64
NAT_HEADS, NAT_DH = 8, 64
WIN_H, WIN_W = 8, 16
N_EXPERTS = 8
ROPE_BASE = 10000.0
GLA_QK = GLA_HEADS * GLA_DK
GLA_V = GLA_HEADS * GLA_DV
NAT_W = NAT_HEADS * NAT_DH
LANES = 128
COL_GQ, COL_GK, COL_GV, COL_GR, COL_GA, COL_GB = 0, 512, 1024, 2048, 3072, 4096
COL_NQ, COL_NK, COL_NV, COL_CODE = 5120, 5632, 6144, 6656
N_IN = COL_CODE + LANES
VMEM_LIMIT = 56 * 1024 * 1024
BF16 = jnp.bfloat16
F32 = jnp.float32


def _dot(a, b):
    return jnp.dot(a, b, preferred_element_type=F32)


def _dot_nt(a, b):
    return lax.dot_general(a, b, (((1,), (1,)), ((), ())), preferred_element_type=F32)


def _dot_tn(a, b):
    return lax.dot_general(a, b, (((0,), (0,)), ((), ())), preferred_element_type=F32)


def _rms(x, g):
    return x * lax.rsqrt(jnp.mean(x * x, axis=-1, keepdims=True) + EPS) * g


def _split_dot(a, b_hi, b_lo):
    a_hi = a.astype(BF16)
    a_lo = (a - a_hi.astype(F32)).astype(BF16)
    return _dot(a_hi, b_hi) + (_dot(a_lo, b_hi) + _dot(a_hi, b_lo))


def _params(sem):
    return pltpu.CompilerParams(dimension_semantics=sem, vmem_limit_bytes=VMEM_LIMIT)


def _mod_kernel(c_ref, w_ref, b_ref, o_ref):
    c = c_ref[...]
    s = c * jax.nn.sigmoid(c)
    o_ref[0] = _dot(s.astype(BF16), w_ref[0].astype(BF16)) + b_ref[0]


def _modulation(c_rows, w_mod, b_mod):
    depth, d, n = w_mod.shape
    rows = c_rows.shape[0]
    tn = 1536
    return pl.pallas_call(
        _mod_kernel,
        grid=(depth, n // tn),
        in_specs=[pl.BlockSpec((rows, d), lambda l, j: (0, 0)),
                  pl.BlockSpec((1, d, tn), lambda l, j: (l, 0, j)),
                  pl.BlockSpec((1, 1, tn), lambda l, j: (l, 0, j))],
        out_specs=pl.BlockSpec((1, rows, tn), lambda l, j: (l, 0, j)),
        out_shape=jax.ShapeDtypeStruct((depth, rows, n), F32),
        compiler_params=_params(("arbitrary", "arbitrary")),
        name="modulation",
    )(c_rows, w_mod, b_mod.reshape(depth, 1, n))


def _inproj_kernel(x_ref, g_ref, sh_ref, sc_ref, cos_ref, sin_ref, w_ref, o_ref):
    h = _rms(x_ref[...], g_ref[...]) * (1.0 + sc_ref[...]) + sh_ref[...]
    hb = h.astype(BF16)
    n = o_ref.shape[1]
    qk = _dot(hb, w_ref[:, :2 * GLA_QK])
    lane = lax.broadcasted_iota(jnp.int32, qk.shape, 1)
    reps = 2 * GLA_HEADS
    cos = jnp.concatenate([cos_ref[...]] * reps, axis=1)
    sin = jnp.concatenate([sin_ref[...]] * reps, axis=1)
    partner = jnp.where((lane % 64) < 32, pltpu.roll(qk, 2 * GLA_QK - 32, 1), pltpu.roll(qk, 32, 1))
    qk = qk * cos + partner * sin
    qk = jnp.where(lane < GLA_QK, qk * (GLA_DK ** -0.5), qk)
    o_ref[:, :2 * GLA_QK] = qk.astype(o_ref.dtype)
    step = 1024
    for j in range(2 * GLA_QK, n, step):
        w = min(step, n - j)
        o_ref[:, j:j + w] = _dot(hb, w_ref[:, j:j + w]).astype(o_ref.dtype)


def _in_projection(x, g, mod, rope_cos, rope_sin, w, layer, nb, nt):
    t, d = x.shape
    n = w.shape[2]
    row_of_tile = lambda i: jnp.where(i < nb, nb, (i - nb) // nt)
    rope_tile = lambda i: jnp.where(i < nb, 0, 1 + (i - nb) % nt)
    dm = mod.shape[-1]
    mspec = lambda comp: pl.BlockSpec((None, None, 1, dm), lambda i: (row_of_tile(i), comp, 0, 0))
    return pl.pallas_call(
        _inproj_kernel,
        grid=(t // TILE,),
        in_specs=[pl.BlockSpec((TILE, d), lambda i: (i, 0)),
                  pl.BlockSpec((1, d), lambda i: (0, 0)),
                  mspec(0), mspec(1),
                  pl.BlockSpec((TILE, LANES), lambda i: (rope_tile(i), 0)),
                  pl.BlockSpec((TILE, LANES), lambda i: (rope_tile(i), 0)),
                  pl.BlockSpec((None, d, n), lambda i: (layer, 0, 0))],
        out_specs=pl.BlockSpec((TILE, n), lambda i: (i, 0)),
        out_shape=jax.ShapeDtypeStruct((t, n), BF16),
        compiler_params=_params(("arbitrary",)),
        name="in_projection",
    )(x, g, mod, mod, rope_cos, rope_sin, w)


def _gla_prepare(d, q_ref, k_ref, code_ref, w2hi_ref, w2lo_ref, gb_ref, cum_ref, tot_ref, ops_ref):
    z = _dot(code_ref[...], w2hi_ref[d]) + _dot(code_ref[...], w2lo_ref[d]) + gb_ref[d]
    lg = (jnp.minimum(z, 0.0) - jnp.log(1.0 + jnp.exp(-jnp.abs(z)))) * (1.0 / GLA_TAU)
    lg_hi = lg.astype(BF16)
    lg_lo = (lg - lg_hi.astype(F32)).astype(BF16)
    cum = _dot(cum_ref[d], lg_hi) + _dot(cum_ref[d], lg_lo)
    tot = _dot(tot_ref[...], lg_hi) + _dot(tot_ref[...], lg_lo)
    k = k_ref[...].astype(F32)
    qe = q_ref[...].astype(F32) * jnp.exp(cum)
    kd = k * jnp.exp(tot - cum)
    ops_ref[d, OP_QE] = qe.astype(BF16)
    ops_ref[d, OP_KE] = (k * jnp.exp(-cum)).astype(BF16)
    n = TILE // GLA_CHUNK
    order = list(range(n)) if d == 0 else list(range(n - 1, -1, -1))
    tc = [tot[c * GLA_CHUNK:c * GLA_CHUNK + 1, :] for c in order]
    zero = jnp.zeros_like(tc[0])
    u_log = [zero, -tc[1], zero, tc[2]]
    w_log = [tc[1], zero, -tc[2], zero]
    for p, c in enumerate(order):
        rows = slice(c * GLA_CHUNK, (c + 1) * GLA_CHUNK)
        before = sum(tc[:p], zero)
        after = sum(tc[p + 1:], zero)
        ops_ref[d, OP_QU, rows] = (qe[rows] * jnp.exp(u_log[p])).astype(BF16)
        ops_ref[d, OP_KW, rows] = (kd[rows] * jnp.exp(w_log[p])).astype(BF16)
        ops_ref[d, OP_QP, rows] = (qe[rows] * jnp.exp(before)).astype(BF16)
        ops_ref[d, OP_KS, rows] = (kd[rows] * jnp.exp(after)).astype(BF16)
    return jnp.exp(sum(tc, zero))


OP_QE, OP_KE, OP_QU, OP_KW, OP_QP, OP_KS = range(6)


def _gla_kernel(qf, kf, vf, cf, qb, kb, vb, cb, w2hi, w2lo, gb, cum, tot, of, ob, st_ref, ops_ref):
    assert TILE // GLA_CHUNK == 4

    @pl.when(pl.program_id(1) == 0)
    def _():
        st_ref[...] = jnp.zeros_like(st_ref)

    decay = [_gla_prepare(0, qf, kf, cf, w2hi, w2lo, gb, cum, tot, ops_ref),
             _gla_prepare(1, qb, kb, cb, w2hi, w2lo, gb, cum, tot, ops_ref)]
    r = lax.broadcasted_iota(jnp.int32, (TILE, TILE), 0)
    s = lax.broadcasted_iota(jnp.int32, (TILE, TILE), 1)
    for d, (v_ref, o_ref) in enumerate(((vf, of), (vb, ob))):
        gap = (r // GLA_CHUNK - s // GLA_CHUNK) * (1 if d == 0 else -1)
        diag = (gap == 0) & ((s <= r) if d == 0 else (s >= r))
        for h in range(GLA_HEADS):
            kcols = slice(h * GLA_DK, (h + 1) * GLA_DK)
            vcols = slice(h * GLA_DV, (h + 1) * GLA_DV)
            op = lambda which: ops_ref[d, which, :, kcols]
            a = jnp.where(diag, _dot_nt(op(OP_QE), op(OP_KE)),
                          jnp.where(gap >= 1, _dot_nt(op(OP_QU), op(OP_KW)), 0.0))
            v = v_ref[:, vcols]
            st = st_ref[d, h]
            o = _dot(a.astype(BF16), v) + _dot_nt(op(OP_QP), st.astype(BF16))
            o_ref[:, vcols] = o.astype(o_ref.dtype)
            st_ref[d, h] = st * decay[d][:, kcols] + _dot_tn(v, op(OP_KS))


def _gla(y, w2hi, w2lo, gbias, cum_m, tot_m, nb, nt):
    t = y.shape[0]
    fwd = lambda b, j: jnp.where(j == 0, b, nb + b * nt + j - 1)
    bwd = lambda b, j: jnp.where(j == 0, b, nb + b * nt + nt - j)

    def ysl(width, col, tile):
        return pl.BlockSpec((TILE, width), lambda b, j: (tile(b, j), col // width))

    def direction(tile):
        return [ysl(GLA_QK, COL_GQ, tile), ysl(GLA_QK, COL_GK, tile), ysl(GLA_V, COL_GV, tile),
                ysl(LANES, COL_CODE, tile)]

    const = lambda shape: pl.BlockSpec(shape, lambda b, j: (0,) * len(shape))
    in_specs = (direction(fwd) + direction(bwd)
                + [const(w2hi.shape), const(w2lo.shape), const(gbias.shape), const(cum_m.shape), const(tot_m.shape)])
    args = [y] * 8 + [w2hi, w2lo, gbias, cum_m, tot_m]
    return pl.pallas_call(
        _gla_kernel,
        grid=(nb, nt + 1),
        in_specs=in_specs,
        out_specs=[pl.BlockSpec((TILE, GLA_V), lambda b, j: (fwd(b, j), 0)),
                   pl.BlockSpec((TILE, GLA_V), lambda b, j: (bwd(b, j), 0))],
        out_shape=[jax.ShapeDtypeStruct((t, GLA_V), BF16)] * 2,
        scratch_shapes=[pltpu.VMEM((2, GLA_HEADS, GLA_DV, GLA_DK), F32),
                        pltpu.VMEM((2, 6, TILE, GLA_QK), BF16)],
        compiler_params=_params(("arbitrary", "arbitrary")),
        name="gla_scan",
    )(*args)


def _nat_kernel(q_ref, k0, k1, k2, kc, v0, v1, v2, vc, bias_ref, mask_ref, o_ref):
    lane = lax.broadcasted_iota(jnp.int32, (TILE, LANES), 1)
    low = lane < NAT_DH
    row_mask = mask_ref[...]
    for p in range(NAT_HEADS // 2):
        cols = slice(p * LANES, (p + 1) * LANES)
        qp = q_ref[:, cols] * (NAT_DH ** -0.5)
        ks = [r[:, cols] for r in (k0, k1, k2, kc)]
        vs = [r[:, cols] for r in (v0, v1, v2, vc)]
        outs = []
        for half in range(2):
            qh = jnp.where(low if half == 0 else ~low, qp, jnp.zeros_like(qp))
            s_lat = jnp.concatenate([_dot_nt(qh, kk) for kk in ks[:3]], axis=1) + (bias_ref[2 * p + half] + row_mask)
            s_ctx = _dot_nt(qh, ks[3])
            m = jnp.maximum(jnp.max(s_lat, axis=-1, keepdims=True), jnp.max(s_ctx, axis=-1, keepdims=True))
            e_lat = jnp.exp(s_lat - m)
            e_ctx = jnp.exp(s_ctx - m)
            den = jnp.sum(e_lat, axis=-1, keepdims=True) + jnp.sum(e_ctx, axis=-1, keepdims=True)
            e_lat = e_lat.astype(BF16)
            acc = _dot(e_ctx.astype(BF16), vs[3])
            for i in range(3):
                acc += _dot(e_lat[:, i * TILE:(i + 1) * TILE], vs[i])
            outs.append(acc / den)
        o_ref[:, cols] = jnp.where(low, outs[0], outs[1]).astype(o_ref.dtype)


def _nat(y, bias, layer, row_masks, nb, nt):
    t = y.shape[0]
    lat = lambda b, tt: nb + b * nt + jnp.clip(tt, 0, nt - 1)
    qtile = lambda b, j: jnp.where(j == 0, b, nb + b * nt + j - 1)
    pattern = lambda b, j: jnp.where(j == 0, 3, jnp.where(j == 1, 0, jnp.where(j == nt, 2, 1)))

    def ysl(col, tile):
        return pl.BlockSpec((TILE, NAT_W), lambda b, j: (tile(b, j), col // NAT_W))

    slab = [lambda b, j: lat(b, j - 2), lambda b, j: lat(b, j - 1), lambda b, j: lat(b, j), lambda b, j: b]
    in_specs = ([ysl(COL_NQ, qtile)] + [ysl(COL_NK, s) for s in slab] + [ysl(COL_NV, s) for s in slab]
                + [pl.BlockSpec((None, NAT_HEADS, TILE, 3 * TILE), lambda b, j: (layer, 0, 0, 0)),
                   pl.BlockSpec((None, TILE, 3 * TILE), lambda b, j: (pattern(b, j), 0, 0))])
    return pl.pallas_call(
        _nat_kernel,
        grid=(nb, nt + 1),
        in_specs=in_specs,
        out_specs=pl.BlockSpec((TILE, NAT_W), lambda b, j: (qtile(b, j), 0)),
        out_shape=jax.ShapeDtypeStruct((t, NAT_W), BF16),
        compiler_params=_params(("arbitrary", "arbitrary")),
        name="nat_attention",
    )(*([y] * 9), bias, row_masks)


def _nat_bias_base(rpb):
    nr, nc = 2 * WIN_H - 1, 2 * WIN_W - 1
    qi = np.arange(TILE_ROWS)
    kj = np.arange(3 * TILE_ROWS)
    qc = np.arange(GRID_W)
    kc = np.arange(GRID_W)
    col_start = np.clip(qc - WIN_W // 2, 0, GRID_W - WIN_W)
    col_ok = (kc[None, :] >= col_start[:, None]) & (kc[None, :] < col_start[:, None] + WIN_W)
    col_off = np.clip(kc[None, :] - qc[:, None], 1 - WIN_W, WIN_W - 1) + (WIN_W - 1)
    row_off = kj[None, :] - qi[:, None] + (WIN_H - 1 - TILE_ROWS)
    assert row_off.min() >= 0 and row_off.max() < nr
    col_sel = ((col_off[None] == np.arange(nc)[:, None, None]) & col_ok[None]).astype(np.float32)
    row_sel = (row_off[None] == np.arange(nr)[:, None, None]).astype(np.float32)
    b = jnp.einsum("dhab,aik,bcl->dhickl", rpb.astype(F32), row_sel, col_sel, precision=lax.Precision.HIGHEST)
    b = b + jnp.where(jnp.asarray(col_ok), 0.0, -jnp.inf)[None, None, None, :, None, :]
    return b.reshape(rpb.shape[0], NAT_HEADS, TILE, 3 * TILE)


def _nat_row_masks(nt):
    rows = nt * TILE_ROWS
    kh = min(WIN_H, rows)
    qi = np.arange(TILE_ROWS)
    kj = np.arange(3 * TILE_ROWS)
    masks = []
    for t in (0, min(1, nt - 1), nt - 1):
        r = TILE_ROWS * t + qi
        row_start = np.clip(r - kh // 2, 0, rows - kh)
        kr = TILE_ROWS * (t - 1) + kj
        valid = (kr >= 0) & (kr < rows)
        masks.append((kr[None, :] >= row_start[:, None]) & (kr[None, :] < row_start[:, None] + kh) & valid[None, :])
    masks.append(np.zeros_like(masks[0]))
    m = np.where(np.stack(masks), 0.0, -np.inf).astype(np.float32)
    m = jnp.broadcast_to(jnp.asarray(m)[:, :, None, :, None], (4, TILE_ROWS, GRID_W, 3 * TILE_ROWS, GRID_W))
    return m.reshape(4, TILE, 3 * TILE)


def _merge_kernel(of_ref, ob_ref, gr_ref, ga_ref, gb_ref, on_ref, x_ref, gain_ref, wbg_ref, wbn_ref, wo_ref,
                  gpost_ref, m2_ref, gpre_ref, m3_ref, m4_ref, *rest, with_router):
    if with_router:
        wrhi_ref, wrlo_ref, x_out, h_out, gate_out, count_out, count_ref = rest
    else:
        x_out, h_out = rest
    o = of_ref[...].astype(F32) + ob_ref[...].astype(F32)
    parts = []
    for h in range(GLA_HEADS):
        oh = o[:, h * GLA_DV:(h + 1) * GLA_DV]
        parts.append(oh * lax.rsqrt(jnp.mean(oh * oh, axis=-1, keepdims=True) + EPS))
    r = gr_ref[...].astype(F32)
    yg = jnp.concatenate(parts, axis=1) * gain_ref[...] * (r * jax.nn.sigmoid(r))
    m = (jax.nn.sigmoid(ga_ref[...].astype(F32)) * _dot(yg.astype(BF16), wbg_ref[...])
         + jax.nn.sigmoid(gb_ref[...].astype(F32)) * _dot(on_ref[...], wbn_ref[...]))
    y = _dot(m.astype(BF16), wo_ref[...])
    x1 = x_ref[...] + m2_ref[...] * _rms(y, gpost_ref[...])
    x_out[...] = x1
    h = _rms(x1, gpre_ref[...]) * (1.0 + m4_ref[...]) + m3_ref[...]
    h_out[...] = h.astype(h_out.dtype)
    if with_router:
        logits = _split_dot(h, wrhi_ref[...], wrlo_ref[...])
        lane = lax.broadcasted_iota(jnp.int32, logits.shape, 1)
        lg = jnp.where(lane < N_EXPERTS, logits, -jnp.inf)
        v1 = jnp.max(lg, axis=-1, keepdims=True)
        i1 = jnp.min(jnp.where(lg == v1, lane, LANES), axis=-1, keepdims=True)
        lg2 = jnp.where(lane == i1, -jnp.inf, lg)
        v2 = jnp.max(lg2, axis=-1, keepdims=True)
        i2 = jnp.min(jnp.where(lg2 == v2, lane, LANES), axis=-1, keepdims=True)
        e2 = jnp.exp(v2 - v1)
        w1 = 1.0 / (1.0 + e2)
        w2 = e2 / (1.0 + e2)

        @pl.when(pl.program_id(0) == 0)
        def _():
            count_ref[...] = jnp.zeros_like(count_ref)

        oh1 = (lane == i1).astype(F32)
        oh2 = (lane == i2).astype(F32)
        rr = lax.broadcasted_iota(jnp.int32, (TILE, TILE), 0)
        ss = lax.broadcasted_iota(jnp.int32, (TILE, TILE), 1)
        before = (ss < rr).astype(BF16)
        carry = count_ref[...]
        cnt1 = jnp.sum(oh1, axis=0, keepdims=True)
        rank1 = jnp.sum(oh1 * (_dot(before, oh1.astype(BF16)) + carry), axis=-1, keepdims=True)
        rank2 = jnp.sum(oh2 * (_dot(before, oh2.astype(BF16)) + (carry + cnt1)), axis=-1, keepdims=True)
        total = carry + cnt1 + jnp.sum(oh2, axis=0, keepdims=True)
        count_ref[...] = total
        count_out[...] = jnp.broadcast_to(total, count_out.shape)
        packed = jnp.zeros_like(logits)
        for ln, val in ((8, i1.astype(F32)), (9, i2.astype(F32)), (10, w1), (11, w2), (12, rank1), (13, rank2)):
            packed = jnp.where(lane == ln, val, packed)
        gate_out[...] = packed


def _merge(y, of, ob, on, x, gain, wbg, wbn, wo, layer, gpost, gpre, mod, router, nb, nt):
    t, d = x.shape
    row_of_tile = lambda i: jnp.where(i < nb, nb, (i - nb) // nt)
    mspec = lambda comp: pl.BlockSpec((None, None, 1, d), lambda i: (row_of_tile(i), comp, 0, 0))
    rowblk = lambda w: pl.BlockSpec((TILE, w), lambda i: (i, 0))
    ycol = lambda w, col: pl.BlockSpec((TILE, w), lambda i: (i, col // w))
    const = lambda a: pl.BlockSpec(a.shape, lambda i: (0,) * a.ndim)
    stacked = lambda a: pl.BlockSpec((None,) + a.shape[1:], lambda i: (layer,) + (0,) * (a.ndim - 1))
    in_specs = [rowblk(GLA_V), rowblk(GLA_V), ycol(GLA_V, COL_GR), ycol(d, COL_GA), ycol(d, COL_GB), rowblk(NAT_W),
                rowblk(d), const(gain), stacked(wbg), stacked(wbn), stacked(wo), const(gpost), mspec(2), const(gpre),
                mspec(3), mspec(4)]
    args = [of, ob, y, y, y, on, x, gain, wbg, wbn, wo, gpost, mod, gpre, mod, mod]
    out_specs = [rowblk(d), rowblk(d)]
    out_shape = [jax.ShapeDtypeStruct((t, d), F32), jax.ShapeDtypeStruct((t, d), BF16)]
    scratch = []
    if router is not None:
        in_specs += [const(router[0]), const(router[1])]
        args += list(router)
        out_specs += [rowblk(LANES), pl.BlockSpec((8, LANES), lambda i: (0, 0))]
        out_shape += [jax.ShapeDtypeStruct((t, LANES), F32), jax.ShapeDtypeStruct((8, LANES), F32)]
        out_shape[1] = jax.ShapeDtypeStruct((t, d), F32)
        scratch = [pltpu.VMEM((1, LANES), F32)]
    return pl.pallas_call(
        functools.partial(_merge_kernel, with_router=router is not None),
        grid=(t // TILE,),
        in_specs=in_specs,
        out_specs=out_specs,
        out_shape=out_shape,
        scratch_shapes=scratch,
        compiler_params=_params(("arbitrary",)),
        name="merge_router" if router is not None else "merge",
    )(*args)


def _swiglu_up(h, w1, w3):
    a = _dot(h, w1.astype(BF16))
    return ((a * jax.nn.sigmoid(a)) * _dot(h, w3.astype(BF16))).astype(BF16)


def _ffn_kernel(h_ref, w1_ref, w3_ref, w2_ref, x_ref, gpost_ref, m5_ref, o_ref, acc_ref, u_ref):
    f = pl.program_id(1)
    last = pl.num_programs(1) - 1

    @pl.when(f == 0)
    def _():
        acc_ref[...] = jnp.zeros_like(acc_ref)
        u_ref[0] = _swiglu_up(h_ref[...], w1_ref[...], w3_ref[...])

    @pl.when((f > 0) & (f < last))
    def _():
        acc_ref[...] += _dot(u_ref[(f + 1) % 2], w2_ref[...].astype(BF16))
        u_ref[f % 2] = _swiglu_up(h_ref[...], w1_ref[...], w3_ref[...])

    @pl.when(f == last)
    def _():
        y = acc_ref[...] + _dot(u_ref[(f + 1) % 2], w2_ref[...].astype(BF16))
        o_ref[...] = x_ref[...] + m5_ref[...] * _rms(y, gpost_ref[...])


def _dispatch_kernel(pos_ref, h_ref, xs_in, xs_out, sem):
    del xs_in
    t = pl.num_programs(0) * TILE
    base = pl.program_id(0) * TILE

    def copies(r):
        src = h_ref.at[pl.ds(r, 1)]
        return (pltpu.make_async_copy(src, xs_out.at[pl.ds(pos_ref[base + r], 1)], sem),
                pltpu.make_async_copy(src, xs_out.at[pl.ds(pos_ref[t + base + r], 1)], sem))

    def start(r, carry):
        for queue, cp in enumerate(copies(r)):
            cp.start(priority=queue)
        return carry

    def wait(r, carry):
        for cp in copies(r):
            cp.wait()
        return carry

    for r in range(TILE):
        start(r, 0)
    lax.fori_loop(0, TILE, wait, 0, unroll=8)


def _moe_ffn_kernel(te_ref, nu_ref, x_ref, w1_ref, w3_ref, w2_ref, o_ref, xb_ref, u_ref):
    del te_ref
    i = pl.program_id(0)
    f = pl.program_id(1)
    last = pl.num_programs(1) - 1
    used = i < nu_ref[0]

    @pl.when(f == 0)
    def _():
        o_ref[...] = jnp.zeros_like(o_ref)

    @pl.when(used & (f == 0))
    def _():
        xb_ref[...] = x_ref[...].astype(BF16)
        u_ref[0] = _swiglu_up(xb_ref[...], w1_ref[0], w3_ref[0])

    @pl.when(used & (f > 0) & (f < last))
    def _():
        o_ref[...] += _dot(u_ref[(f + 1) % 2], w2_ref[0].astype(BF16))
        u_ref[f % 2] = _swiglu_up(xb_ref[...], w1_ref[0], w3_ref[0])

    @pl.when(used & (f == last))
    def _():
        o_ref[...] += _dot(u_ref[(f + 1) % 2], w2_ref[0].astype(BF16))


def _combine_kernel(pos_ref, ys_hbm, g_ref, x_ref, gpost_ref, m5_ref, o_ref, buf_ref, sem, *, t, tile0):
    i = pl.program_id(0)
    n = pl.num_programs(0)

    def copies(step, slot, r):
        base = (step + tile0) * TILE
        return (pltpu.make_async_copy(ys_hbm.at[pl.ds(pos_ref[base + r], 1)],
                                      buf_ref.at[slot, 0, pl.ds(r, 1)], sem.at[slot]),
                pltpu.make_async_copy(ys_hbm.at[pl.ds(pos_ref[t + base + r], 1)],
                                      buf_ref.at[slot, 1, pl.ds(r, 1)], sem.at[slot]))

    def issue(step, slot):
        for r in range(TILE):
            for queue, cp in enumerate(copies(step, slot, r)):
                cp.start(priority=queue)

    @pl.when(i == 0)
    def _():
        issue(0, 0)

    @pl.when(i + 1 < n)
    def _():
        issue(i + 1, (i + 1) % 2)

    slot = i % 2

    def wait(r, carry):
        for cp in copies(i, slot, r):
            cp.wait()
        return carry

    lax.fori_loop(0, TILE, wait, 0, unroll=8)
    g = g_ref[...]
    y = g[:, 10:11] * buf_ref[slot, 0] + g[:, 11:12] * buf_ref[slot, 1]
    o_ref[...] = x_ref[...] + m5_ref[...] * _rms(y, gpost_ref[...])


def _ffn_tiles(t, nb, nt):
    tm = 1024 if (nb * TILE) % 1024 == 0 and (nt * TILE) % 1024 == 0 else TILE
    per = tm // TILE
    row_of_tile = lambda i: jnp.where(i * per < nb, nb, (i * per - nb) // nt)
    return tm, row_of_tile


def _ffn(h, w1, w3, w2, layer, x, gpost, mod, nb, nt):
    t, d = x.shape
    dff = w1.shape[2]
    tm, row_of_tile = _ffn_tiles(t, nb, nt)
    tf = 512
    nf = dff // tf
    up = lambda f: jnp.minimum(f, nf - 1)
    down = lambda f: jnp.maximum(f - 1, 0)
    return pl.pallas_call(
        _ffn_kernel,
        grid=(t // tm, nf + 1),
        in_specs=[pl.BlockSpec((tm, d), lambda i, f: (i, 0)),
                  pl.BlockSpec((None, d, tf), lambda i, f: (layer, 0, up(f))),
                  pl.BlockSpec((None, d, tf), lambda i, f: (layer, 0, up(f))),
                  pl.BlockSpec((None, tf, d), lambda i, f: (layer, down(f), 0)),
                  pl.BlockSpec((tm, d), lambda i, f: (i, 0)),
                  pl.BlockSpec((1, d), lambda i, f: (0, 0)),
                  pl.BlockSpec((None, None, 1, d), lambda i, f: (row_of_tile(i), 5, 0, 0))],
        out_specs=pl.BlockSpec((tm, d), lambda i, f: (i, 0)),
        out_shape=jax.ShapeDtypeStruct((t, d), F32),
        scratch_shapes=[pltpu.VMEM((tm, d), F32), pltpu.VMEM((2, tm, tf), BF16)],
        compiler_params=_params(("arbitrary", "arbitrary")),
        name="ffn_dense",
    )(h, w1, w3, w2, x, gpost, mod)


MOE_TM = 1024
MOE_TF = 512


def _moe_routed(h, pack, counts, w1, w3, w2, layer, x, gpost, mod, nb, nt, latent_only):
    t, d = x.shape
    _, ne, _, dff = w1.shape
    tm = MOE_TM
    tf = MOE_TF if dff % MOE_TF == 0 else 512
    n_tiles = -(-(2 * t) // tm) + ne
    p_rows = n_tiles * tm
    cnt = counts[0, :ne].astype(jnp.int32)
    padded = (cnt + tm - 1) // tm * tm
    ends = jnp.cumsum(padded)
    offs = ends - padded
    e1 = pack[:, 8].astype(jnp.int32)
    e2 = pack[:, 9].astype(jnp.int32)
    pos = jnp.concatenate([offs[e1] + pack[:, 12].astype(jnp.int32), offs[e2] + pack[:, 13].astype(jnp.int32)])
    n_used = (ends[-1] // tm).astype(jnp.int32).reshape(1)
    tile_start = jnp.arange(n_tiles, dtype=jnp.int32) * tm
    tile_start = jnp.minimum(tile_start, ends[-1] - tm)
    tile_expert = jnp.minimum(jnp.sum(tile_start[:, None] >= ends[None, :], axis=1), ne - 1).astype(jnp.int32)

    xs = pl.pallas_call(
        _dispatch_kernel,
        grid_spec=pltpu.PrefetchScalarGridSpec(
            num_scalar_prefetch=1, grid=(t // TILE,),
            in_specs=[pl.BlockSpec((TILE, d), lambda i, pos: (i, 0)), pl.BlockSpec(memory_space=pl.ANY)],
            out_specs=pl.BlockSpec(memory_space=pl.ANY),
            scratch_shapes=[pltpu.SemaphoreType.DMA(())]),
        out_shape=jax.ShapeDtypeStruct((p_rows, d), F32),
        input_output_aliases={2: 0},
        compiler_params=_params(("arbitrary",)),
        name="moe_dispatch",
    )(pos, h, jnp.zeros((p_rows, d), F32))

    nf = dff // tf
    row = lambda i, nu: jnp.minimum(i, nu[0] - 1)
    up = lambda i, f, nu: jnp.where(i < nu[0], jnp.minimum(f, nf - 1), nf - 1)
    down = lambda i, f, nu: jnp.where(i < nu[0], jnp.maximum(f - 1, 0), nf - 1)
    ys = pl.pallas_call(
        _moe_ffn_kernel,
        grid_spec=pltpu.PrefetchScalarGridSpec(
            num_scalar_prefetch=2, grid=(n_tiles, nf + 1),
            in_specs=[pl.BlockSpec((tm, d), lambda i, f, te, nu: (row(i, nu), 0)),
                      pl.BlockSpec((None, 1, d, tf), lambda i, f, te, nu: (layer, te[i], 0, up(i, f, nu))),
                      pl.BlockSpec((None, 1, d, tf), lambda i, f, te, nu: (layer, te[i], 0, up(i, f, nu))),
                      pl.BlockSpec((None, 1, tf, d), lambda i, f, te, nu: (layer, te[i], down(i, f, nu), 0))],
            out_specs=pl.BlockSpec((tm, d), lambda i, f, te, nu: (i, 0)),
            scratch_shapes=[pltpu.VMEM((tm, d), BF16), pltpu.VMEM((2, tm, tf), BF16)]),
        out_shape=jax.ShapeDtypeStruct((p_rows, d), F32),
        compiler_params=_params(("arbitrary", "arbitrary")),
        name="moe_ffn",
    )(tile_expert, n_used, xs, w1, w3, w2)

    tile0 = nb if latent_only else 0
    n_out = t // TILE - tile0
    row_of_tile = lambda i: jnp.where(i + tile0 < nb, nb, (i + tile0 - nb) // nt)
    return pl.pallas_call(
        functools.partial(_combine_kernel, t=t, tile0=tile0),
        grid_spec=pltpu.PrefetchScalarGridSpec(
            num_scalar_prefetch=1, grid=(n_out,),
            in_specs=[pl.BlockSpec(memory_space=pl.ANY),
                      pl.BlockSpec((TILE, LANES), lambda i, pos: (i + tile0, 0)),
                      pl.BlockSpec((TILE, d), lambda i, pos: (i + tile0, 0)),
                      pl.BlockSpec((1, d), lambda i, pos: (0, 0)),
                      pl.BlockSpec((None, None, 1, d), lambda i, pos: (row_of_tile(i), 5, 0, 0))],
            out_specs=pl.BlockSpec((TILE, d), lambda i, pos: (i, 0)),
            scratch_shapes=[pltpu.VMEM((2, 2, TILE, d), F32), pltpu.SemaphoreType.DMA((2,))]),
        out_shape=jax.ShapeDtypeStruct((n_out * TILE, d), F32),
        compiler_params=_params(("arbitrary",)),
        name="moe_combine",
    )(pos, ys, pack, x, gpost, mod)


def _reorder_w_in(w):
    a = GLA_QK * 2 + GLA_V * 2
    c0 = a + 2 * GLA_RANK
    g0 = c0 + 3 * NAT_W
    wb = w.astype(BF16)
    pad = jnp.zeros(w.shape[:2] + (LANES - 2 * GLA_RANK,), BF16)
    return jnp.concatenate([wb[..., :a], wb[..., g0:], wb[..., c0:g0], wb[..., a:c0], pad], axis=-1)


def _rope_tables(nt):
    t = np.arange(nt * TILE)
    pos = np.stack([t // GRID_W, t % GRID_W], axis=1).astype(np.float32)
    nf = GLA_DK // 4
    inv = (ROPE_BASE ** (-jnp.arange(nf, dtype=F32) / nf))
    ang = jnp.asarray(pos)[:, :, None] * inv[None, None, :]
    cos = jnp.concatenate([jnp.cos(ang), jnp.cos(ang)], axis=-1).reshape(nt * TILE, GLA_DK)
    sin = jnp.concatenate([-jnp.sin(ang), jnp.sin(ang)], axis=-1).reshape(nt * TILE, GLA_DK)
    ident = jnp.ones((TILE, GLA_DK), F32)
    return (jnp.concatenate([ident, cos], axis=0), jnp.concatenate([jnp.zeros_like(ident), sin], axis=0))


def _chunk_matrices():
    i = np.arange(TILE)
    same = (i[:, None] // GLA_CHUNK) == (i[None, :] // GLA_CHUNK)
    lower = same & (i[None, :] <= i[:, None])
    upper = same & (i[None, :] >= i[:, None])
    cum = jnp.asarray(np.stack([lower, upper]).astype(np.float32), BF16)
    return cum, jnp.asarray(same.astype(np.float32), BF16)


def _hi_lo(w):
    hi = w.astype(BF16)
    return hi, (w - hi.astype(F32)).astype(BF16)


def kernel(x, c, ctx, c_ctx, w_mod, b_mod, norm_mix_pre, norm_mix_post, w_in, gla_gate_w2, gla_gate_b, gla_norm,
           nat_rpb, w_branch_gla, w_branch_nat, w_out, norm_ffn_pre, norm_ffn_post, ffn_w1, ffn_w3, ffn_w2,
           moe_router, moe_w1, moe_w3, moe_w2):
    nb, seq, d = x.shape
    depth = w_mod.shape[0]
    assert ctx.shape[1] == TILE and seq % TILE == 0 and d % LANES == 0
    nt = seq // TILE
    mod_rows = -(-(nb + 1) // 8) * 8
    c_rows = jnp.concatenate([c, c_ctx[None], jnp.zeros((mod_rows - nb - 1, d), c.dtype)], axis=0)
    mod_all = _modulation(c_rows, w_mod, b_mod).reshape(depth, mod_rows, 6, 1, d)
    xs = jnp.concatenate([ctx.reshape(nb * TILE, d), x.reshape(nb * seq, d)], axis=0)
    rope_cos, rope_sin = _rope_tables(nt)
    cum_m, tot_m = _chunk_matrices()
    row = lambda v: v.reshape(1, -1)
    w_in_b = _reorder_w_in(w_in)
    wbg_b, wbn_b, wo_b = w_branch_gla.astype(BF16), w_branch_nat.astype(BF16), w_out.astype(BF16)
    ffn_b = (ffn_w1, ffn_w3, ffn_w2)
    moe_b = (moe_w1, moe_w3, moe_w2)
    nat_bias = _nat_bias_base(nat_rpb)
    nat_masks = _nat_row_masks(nt)
    w2p = jnp.zeros((depth, 2, LANES, GLA_QK), F32)
    w2p = w2p.at[:, 0, :GLA_RANK].set(gla_gate_w2[:, 0]).at[:, 1, GLA_RANK:2 * GLA_RANK].set(gla_gate_w2[:, 1])
    w2hi, w2lo = _hi_lo(w2p)
    wr = jnp.zeros((moe_router.shape[0], d, LANES), F32).at[:, :, :N_EXPERTS].set(moe_router)
    wr_hi, wr_lo = _hi_lo(wr)
    for i in range(depth):
        mod = mod_all[i]
        last = i == depth - 1
        y = _in_projection(xs, row(norm_mix_pre[i]), mod, rope_cos, rope_sin, w_in_b, i, nb, nt)
        of, ob = _gla(y, w2hi[i], w2lo[i], gla_gate_b[i].reshape(2, 1, GLA_QK), cum_m, tot_m, nb, nt)
        on = _nat(y, nat_bias, i, nat_masks, nb, nt)
        is_moe = i % 2 == 1
        j = i // 2
        router = (wr_hi[j], wr_lo[j]) if is_moe else None
        outs = _merge(y, of, ob, on, xs, row(gla_norm[i]), wbg_b, wbn_b, wo_b, i, row(norm_mix_post[i]),
                      row(norm_ffn_pre[i]), mod, router, nb, nt)
        if is_moe:
            xs, h, pack, counts = outs
            xs = _moe_routed(h, pack, counts, *moe_b, j, xs, row(norm_ffn_post[i]), mod, nb, nt, latent_only=last)
        else:
            xs, h = outs
            xs = _ffn(h, *ffn_b, j, xs, row(norm_ffn_post[i]), mod, nb, nt)
            if last:
                xs = xs[nb * TILE:]
    return xs.reshape(nb, seq, d)
```

```python
import functools

import numpy as np
import jax
import jax.numpy as jnp
from jax import lax
from jax.experimental import pallas as pl
from jax.experimental.pallas import tpu as pltpu

EPS = 1e-6
GRID_W = 64
TILE = 256
TILE_ROWS = TILE // GRID_W
GLA_HEADS, GLA_DK, GLA_DV, GLA_RANK, GLA_TAU, GLA_CHUNK = 4, 128, 256, 16, 16.0, 64
NAT_HEADS, NAT_DH = 8, 64
WIN_H, WIN_W = 8, 16
N_EXPERTS = 8
ROPE_BASE = 10000.0
GLA_QK = GLA_HEADS * GLA_DK
GLA_V = GLA_HEADS * GLA_DV
NAT_W = NAT_HEADS * NAT_DH
LANES = 128
COL_GQ, COL_GK, COL_GV, COL_GR, COL_GA, COL_GB = 0, 512, 1024, 2048, 3072, 4096
COL_NQ, COL_NK, COL_NV, COL_CODE = 5120, 5632, 6144, 6656
N_IN = COL_CODE + LANES
VMEM_LIMIT = 56 * 1024 * 1024
BF16 = jnp.bfloat16
F32 = jnp.float32


def _dot(a, b):
    return jnp.dot(a, b, preferred_element_type=F32)


def _dot_nt(a, b):
    return lax.dot_general(a, b, (((1,), (1,)), ((), ())), preferred_element_type=F32)


def _dot_tn(a, b):
    return lax.dot_general(a, b, (((0,), (0,)), ((), ())), preferred_element_type=F32)


def _rms(x, g):
    return x * lax.rsqrt(jnp.mean(x * x, axis=-1, keepdims=True) + EPS) * g


def _split_dot(a, b_hi, b_lo):
    a_hi = a.astype(BF16)
    a_lo = (a - a_hi.astype(F32)).astype(BF16)
    return _dot(a_hi, b_hi) + (_dot(a_lo, b_hi) + _dot(a_hi, b_lo))


def _params(sem):
    return pltpu.CompilerParams(dimension_semantics=sem, vmem_limit_bytes=VMEM_LIMIT)


def _mod_kernel(c_ref, w_ref, b_ref, o_ref):
    c = c_ref[...]
    s = c * jax.nn.sigmoid(c)
    o_ref[0] = _dot(s.astype(BF16), w_ref[0].astype(BF16)) + b_ref[0]


def _modulation(c_rows, w_mod, b_mod):
    depth, d, n = w_mod.shape
    rows = c_rows.shape[0]
    tn = 1536
    return pl.pallas_call(
        _mod_kernel,
        grid=(depth, n // tn),
        in_specs=[pl.BlockSpec((rows, d), lambda l, j: (0, 0)),
                  pl.BlockSpec((1, d, tn), lambda l, j: (l, 0, j)),
                  pl.BlockSpec((1, 1, tn), lambda l, j: (l, 0, j))],
        out_specs=pl.BlockSpec((1, rows, tn), lambda l, j: (l, 0, j)),
        out_shape=jax.ShapeDtypeStruct((depth, rows, n), F32),
        compiler_params=_params(("arbitrary", "arbitrary")),
        name="modulation",
    )(c_rows, w_mod, b_mod.reshape(depth, 1, n))


def _inproj_kernel(x_ref, g_ref, sh_ref, sc_ref, cos_ref, sin_ref, w_ref, o_ref):
    h = _rms(x_ref[...], g_ref[...]) * (1.0 + sc_ref[...]) + sh_ref[...]
    hb = h.astype(BF16)
    n = o_ref.shape[1]
    qk = _dot(hb, w_ref[:, :2 * GLA_QK])
    lane = lax.broadcasted_iota(jnp.int32, qk.shape, 1)
    reps = 2 * GLA_HEADS
    cos = jnp.concatenate([cos_ref[...]] * reps, axis=1)
    sin = jnp.concatenate([sin_ref[...]] * reps, axis=1)
    partner = jnp.where((lane % 64) < 32, pltpu.roll(qk, 2 * GLA_QK - 32, 1), pltpu.roll(qk, 32, 1))
    qk = qk * cos + partner * sin
    qk = jnp.where(lane < GLA_QK, qk * (GLA_DK ** -0.5), qk)
    o_ref[:, :2 * GLA_QK] = qk.astype(o_ref.dtype)
    step = 1024
    for j in range(2 * GLA_QK, n, step):
        w = min(step, n - j)
        o_ref[:, j:j + w] = _dot(hb, w_ref[:, j:j + w]).astype(o_ref.dtype)


def _in_projection(x, g, mod, rope_cos, rope_sin, w, layer, nb, nt):
    t, d = x.shape
    n = w.shape[2]
    row_of_tile = lambda i: jnp.where(i < nb, nb, (i - nb) // nt)
    rope_tile = lambda i: jnp.where(i < nb, 0, 1 + (i - nb) % nt)
    dm = mod.shape[-1]
    mspec = lambda comp: pl.BlockSpec((None, None, 1, dm), lambda i: (row_of_tile(i), comp, 0, 0))
    return pl.pallas_call(
        _inproj_kernel,
        grid=(t // TILE,),
        in_specs=[pl.BlockSpec((TILE, d), lambda i: (i, 0)),
                  pl.BlockSpec((1, d), lambda i: (0, 0)),
                  mspec(0), mspec(1),
                  pl.BlockSpec((TILE, LANES), lambda i: (rope_tile(i), 0)),
                  pl.BlockSpec((TILE, LANES), lambda i: (rope_tile(i), 0)),
                  pl.BlockSpec((None, d, n), lambda i: (layer, 0, 0))],
        out_specs=pl.BlockSpec((TILE, n), lambda i: (i, 0)),
        out_shape=jax.ShapeDtypeStruct((t, n), BF16),
        compiler_params=_params(("arbitrary",)),
        name="in_projection",
    )(x, g, mod, mod, rope_cos, rope_sin, w)


def _gla_prepare(d, q_ref, k_ref, code_ref, w2hi_ref, w2lo_ref, gb_ref, cum_ref, tot_ref, ops_ref):
    z = _dot(code_ref[...], w2hi_ref[d]) + _dot(code_ref[...], w2lo_ref[d]) + gb_ref[d]
    lg = (jnp.minimum(z, 0.0) - jnp.log(1.0 + jnp.exp(-jnp.abs(z)))) * (1.0 / GLA_TAU)
    lg_hi = lg.astype(BF16)
    lg_lo = (lg - lg_hi.astype(F32)).astype(BF16)
    cum = _dot(cum_ref[d], lg_hi) + _dot(cum_ref[d], lg_lo)
    tot = _dot(tot_ref[...], lg_hi) + _dot(tot_ref[...], lg_lo)
    k = k_ref[...].astype(F32)
    qe = q_ref[...].astype(F32) * jnp.exp(cum)
    kd = k * jnp.exp(tot - cum)
    ops_ref[d, OP_QE] = qe.astype(BF16)
    ops_ref[d, OP_KE] = (k * jnp.exp(-cum)).astype(BF16)
    n = TILE // GLA_CHUNK
    order = list(range(n)) if d == 0 else list(range(n - 1, -1, -1))
    tc = [tot[c * GLA_CHUNK:c * GLA_CHUNK + 1, :] for c in order]
    zero = jnp.zeros_like(tc[0])
    u_log = [zero, -tc[1], zero, tc[2]]
    w_log = [tc[1], zero, -tc[2], zero]
    for p, c in enumerate(order):
        rows = slice(c * GLA_CHUNK, (c + 1) * GLA_CHUNK)
        before = sum(tc[:p], zero)
        after = sum(tc[p + 1:], zero)
        ops_ref[d, OP_QU, rows] = (qe[rows] * jnp.exp(u_log[p])).astype(BF16)
        ops_ref[d, OP_KW, rows] = (kd[rows] * jnp.exp(w_log[p])).astype(BF16)
        ops_ref[d, OP_QP, rows] = (qe[rows] * jnp.exp(before)).astype(BF16)
        ops_ref[d, OP_KS, rows] = (kd[rows] * jnp.exp(after)).astype(BF16)
    return jnp.exp(sum(tc, zero))


OP_QE, OP_KE, OP_QU, OP_KW, OP_QP, OP_KS = range(6)


def _gla_kernel(qf, kf, vf, cf, qb, kb, vb, cb, w2hi, w2lo, gb, cum, tot, of, ob, st_ref, ops_ref):
    assert TILE // GLA_CHUNK == 4

    @pl.when(pl.program_id(1) == 0)
    def _():
        st_ref[...] = jnp.zeros_like(st_ref)

    decay = [_gla_prepare(0, qf, kf, cf, w2hi, w2lo, gb, cum, tot, ops_ref),
             _gla_prepare(1, qb, kb, cb, w2hi, w2lo, gb, cum, tot, ops_ref)]
    r = lax.broadcasted_iota(jnp.int32, (TILE, TILE), 0)
    s = lax.broadcasted_iota(jnp.int32, (TILE, TILE), 1)
    for d, (v_ref, o_ref) in enumerate(((vf, of), (vb, ob))):
        gap = (r // GLA_CHUNK - s // GLA_CHUNK) * (1 if d == 0 else -1)
        diag = (gap == 0) & ((s <= r) if d == 0 else (s >= r))
        for h in range(GLA_HEADS):
            kcols = slice(h * GLA_DK, (h + 1) * GLA_DK)
            vcols = slice(h * GLA_DV, (h + 1) * GLA_DV)
            op = lambda which: ops_ref[d, which, :, kcols]
            a = jnp.where(diag, _dot_nt(op(OP_QE), op(OP_KE)),
                          jnp.where(gap >= 1, _dot_nt(op(OP_QU), op(OP_KW)), 0.0))
            v = v_ref[:, vcols]
            st = st_ref[d, h]
            o = _dot(a.astype(BF16), v) + _dot_nt(op(OP_QP), st.astype(BF16))
            o_ref[:, vcols] = o.astype(o_ref.dtype)
            st_ref[d, h] = st * decay[d][:, kcols] + _dot_tn(v, op(OP_KS))


def _gla(y, w2hi, w2lo, gbias, cum_m, tot_m, nb, nt):
    t = y.shape[0]
    fwd = lambda b, j: jnp.where(j == 0, b, nb + b * nt + j - 1)
    bwd = lambda b, j: jnp.where(j == 0, b, nb + b * nt + nt - j)

    def ysl(width, col, tile):
        return pl.BlockSpec((TILE, width), lambda b, j: (tile(b, j), col // width))

    def direction(tile):
        return [ysl(GLA_QK, COL_GQ, tile), ysl(GLA_QK, COL_GK, tile), ysl(GLA_V, COL_GV, tile),
                ysl(LANES, COL_CODE, tile)]

    const = lambda shape: pl.BlockSpec(shape, lambda b, j: (0,) * len(shape))
    in_specs = (direction(fwd) + direction(bwd)
                + [const(w2hi.shape), const(w2lo.shape), const(gbias.shape), const(cum_m.shape), const(tot_m.shape)])
    args = [y] * 8 + [w2hi, w2lo, gbias, cum_m, tot_m]
    return pl.pallas_call(
        _gla_kernel,
        grid=(nb, nt + 1),
        in_specs=in_specs,
        out_specs=[pl.BlockSpec((TILE, GLA_V), lambda b, j: (fwd(b, j), 0)),
                   pl.BlockSpec((TILE, GLA_V), lambda b, j: (bwd(b, j), 0))],
        out_shape=[jax.ShapeDtypeStruct((t, GLA_V), BF16)] * 2,
        scratch_shapes=[pltpu.VMEM((2, GLA_HEADS, GLA_DV, GLA_DK), F32),
                        pltpu.VMEM((2, 6, TILE, GLA_QK), BF16)],
        compiler_params=_params(("arbitrary", "arbitrary")),
        name="gla_scan",
    )(*args)


def _nat_kernel(q_ref, k0, k1, k2, kc, v0, v1, v2, vc, bias_ref, mask_ref, o_ref):
    lane = lax.broadcasted_iota(jnp.int32, (TILE, LANES), 1)
    low = lane < NAT_DH
    row_mask = mask_ref[...]
    for p in range(NAT_HEADS // 2):
        cols = slice(p * LANES, (p + 1) * LANES)
        qp = q_ref[:, cols] * (NAT_DH ** -0.5)
        ks = [r[:, cols] for r in (k0, k1, k2, kc)]
        vs = [r[:, cols] for r in (v0, v1, v2, vc)]
        outs = []
        for half in range(2):
            qh = jnp.where(low if half == 0 else ~low, qp, jnp.zeros_like(qp))
            s_lat = jnp.concatenate([_dot_nt(qh, kk) for kk in ks[:3]], axis=1) + (bias_ref[2 * p + half] + row_mask)
            s_ctx = _dot_nt(qh, ks[3])
            m = jnp.maximum(jnp.max(s_lat, axis=-1, keepdims=True), jnp.max(s_ctx, axis=-1, keepdims=True))
            e_lat = jnp.exp(s_lat - m)
            e_ctx = jnp.exp(s_ctx - m)
            den = jnp.sum(e_lat, axis=-1, keepdims=True) + jnp.sum(e_ctx, axis=-1, keepdims=True)
            e_lat = e_lat.astype(BF16)
            acc = _dot(e_ctx.astype(BF16), vs[3])
            for i in range(3):
                acc += _dot(e_lat[:, i * TILE:(i + 1) * TILE], vs[i])
            outs.append(acc / den)
        o_ref[:, cols] = jnp.where(low, outs[0], outs[1]).astype(o_ref.dtype)


def _nat(y, bias, layer, row_masks, nb, nt):
    t = y.shape[0]
    lat = lambda b, tt: nb + b * nt + jnp.clip(tt, 0, nt - 1)
    qtile = lambda b, j: jnp.where(j == 0, b, nb + b * nt + j - 1)
    pattern = lambda b, j: jnp.where(j == 0, 3, jnp.where(j == 1, 0, jnp.where(j == nt, 2, 1)))

    def ysl(col, tile):
        return pl.BlockSpec((TILE, NAT_W), lambda b, j: (tile(b, j), col // NAT_W))

    slab = [lambda b, j: lat(b, j - 2), lambda b, j: lat(b, j - 1), lambda b, j: lat(b, j), lambda b, j: b]
    in_specs = ([ysl(COL_NQ, qtile)] + [ysl(COL_NK, s) for s in slab] + [ysl(COL_NV, s) for s in slab]
                + [pl.BlockSpec((None, NAT_HEADS, TILE, 3 * TILE), lambda b, j: (layer, 0, 0, 0)),
                   pl.BlockSpec((None, TILE, 3 * TILE), lambda b, j: (pattern(b, j), 0, 0))])
    return pl.pallas_call(
        _nat_kernel,
        grid=(nb, nt + 1),
        in_specs=in_specs,
        out_specs=pl.BlockSpec((TILE, NAT_W), lambda b, j: (qtile(b, j), 0)),
        out_shape=jax.ShapeDtypeStruct((t, NAT_W), BF16),
        compiler_params=_params(("arbitrary", "arbitrary")),
        name="nat_attention",
    )(*([y] * 9), bias, row_masks)


def _nat_bias_base(rpb):
    nr, nc = 2 * WIN_H - 1, 2 * WIN_W - 1
    qi = np.arange(TILE_ROWS)
    kj = np.arange(3 * TILE_ROWS)
    qc = np.arange(GRID_W)
    kc = np.arange(GRID_W)
    col_start = np.clip(qc - WIN_W // 2, 0, GRID_W - WIN_W)
    col_ok = (kc[None, :] >= col_start[:, None]) & (kc[None, :] < col_start[:, None] + WIN_W)
    col_off = np.clip(kc[None, :] - qc[:, None], 1 - WIN_W, WIN_W - 1) + (WIN_W - 1)
    row_off = kj[None, :] - qi[:, None] + (WIN_H - 1 - TILE_ROWS)
    assert row_off.min() >= 0 and row_off.max() < nr
    col_sel = ((col_off[None] == np.arange(nc)[:, None, None]) & col_ok[None]).astype(np.float32)
    row_sel = (row_off[None] == np.arange(nr)[:, None, None]).astype(np.float32)
    b = jnp.einsum("dhab,aik,bcl->dhickl", rpb.astype(F32), row_sel, col_sel, precision=lax.Precision.HIGHEST)
    b = b + jnp.where(jnp.asarray(col_ok), 0.0, -jnp.inf)[None, None, None, :, None, :]
    return b.reshape(rpb.shape[0], NAT_HEADS, TILE, 3 * TILE)


def _nat_row_masks(nt):
    rows = nt * TILE_ROWS
    kh = min(WIN_H, rows)
    qi = np.arange(TILE_ROWS)
    kj = np.arange(3 * TILE_ROWS)
    masks = []
    for t in (0, min(1, nt - 1), nt - 1):
        r = TILE_ROWS * t + qi
        row_start = np.clip(r - kh // 2, 0, rows - kh)
        kr = TILE_ROWS * (t - 1) + kj
        valid = (kr >= 0) & (kr < rows)
        masks.append((kr[None, :] >= row_start[:, None]) & (kr[None, :] < row_start[:, None] + kh) & valid[None, :])
    masks.append(np.zeros_like(masks[0]))
    m = np.where(np.stack(masks), 0.0, -np.inf).astype(np.float32)
    m = jnp.broadcast_to(jnp.asarray(m)[:, :, None, :, None], (4, TILE_ROWS, GRID_W, 3 * TILE_ROWS, GRID_W))
    return m.reshape(4, TILE, 3 * TILE)


def _merge_kernel(of_ref, ob_ref, gr_ref, ga_ref, gb_ref, on_ref, x_ref, gain_ref, wbg_ref, wbn_ref, wo_ref,
                  gpost_ref, m2_ref, gpre_ref, m3_ref, m4_ref, *rest, with_router):
    if with_router:
        wrhi_ref, wrlo_ref, x_out, h_out, gate_out, count_out, count_ref = rest
    else:
        x_out, h_out = rest
    o = of_ref[...].astype(F32) + ob_ref[...].astype(F32)
    parts = []
    for h in range(GLA_HEADS):
        oh = o[:, h * GLA_DV:(h + 1) * GLA_DV]
        parts.append(oh * lax.rsqrt(jnp.mean(oh * oh, axis=-1, keepdims=True) + EPS))
    r = gr_ref[...].astype(F32)
    yg = jnp.concatenate(parts, axis=1) * gain_ref[...] * (r * jax.nn.sigmoid(r))
    m = (jax.nn.sigmoid(ga_ref[...].astype(F32)) * _dot(yg.astype(BF16), wbg_ref[...])
         + jax.nn.sigmoid(gb_ref[...].astype(F32)) * _dot(on_ref[...], wbn_ref[...]))
    y = _dot(m.astype(BF16), wo_ref[...])
    x1 = x_ref[...] + m2_ref[...] * _rms(y, gpost_ref[...])
    x_out[...] = x1
    h = _rms(x1, gpre_ref[...]) * (1.0 + m4_ref[...]) + m3_ref[...]
    h_out[...] = h.astype(h_out.dtype)
    if with_router:
        logits = _split_dot(h, wrhi_ref[...], wrlo_ref[...])
        lane = lax.broadcasted_iota(jnp.int32, logits.shape, 1)
        lg = jnp.where(lane < N_EXPERTS, logits, -jnp.inf)
        v1 = jnp.max(lg, axis=-1, keepdims=True)
        i1 = jnp.min(jnp.where(lg == v1, lane, LANES), axis=-1, keepdims=True)
        lg2 = jnp.where(lane == i1, -jnp.inf, lg)
        v2 = jnp.max(lg2, axis=-1, keepdims=True)
        i2 = jnp.min(jnp.where(lg2 == v2, lane, LANES), axis=-1, keepdims=True)
        e2 = jnp.exp(v2 - v1)
        w1 = 1.0 / (1.0 + e2)
        w2 = e2 / (1.0 + e2)

        @pl.when(pl.program_id(0) == 0)
        def _():
            count_ref[...] = jnp.zeros_like(count_ref)

        oh1 = (lane == i1).astype(F32)
        oh2 = (lane == i2).astype(F32)
        rr = lax.broadcasted_iota(jnp.int32, (TILE, TILE), 0)
        ss = lax.broadcasted_iota(jnp.int32, (TILE, TILE), 1)
        before = (ss < rr).astype(BF16)
        carry = count_ref[...]
        cnt1 = jnp.sum(oh1, axis=0, keepdims=True)
        rank1 = jnp.sum(oh1 * (_dot(before, oh1.astype(BF16)) + carry), axis=-1, keepdims=True)
        rank2 = jnp.sum(oh2 * (_dot(before, oh2.astype(BF16)) + (carry + cnt1)), axis=-1, keepdims=True)
        total = carry + cnt1 + jnp.sum(oh2, axis=0, keepdims=True)
        count_ref[...] = total
        count_out[...] = jnp.broadcast_to(total, count_out.shape)
        packed = jnp.zeros_like(logits)
        for ln, val in ((8, i1.astype(F32)), (9, i2.astype(F32)), (10, w1), (11, w2), (12, rank1), (13, rank2)):
            packed = jnp.where(lane == ln, val, packed)
        gate_out[...] = packed


def _merge(y, of, ob, on, x, gain, wbg, wbn, wo, layer, gpost, gpre, mod, router, nb, nt):
    t, d = x.shape
    row_of_tile = lambda i: jnp.where(i < nb, nb, (i - nb) // nt)
    mspec = lambda comp: pl.BlockSpec((None, None, 1, d), lambda i: (row_of_tile(i), comp, 0, 0))
    rowblk = lambda w: pl.BlockSpec((TILE, w), lambda i: (i, 0))
    ycol = lambda w, col: pl.BlockSpec((TILE, w), lambda i: (i, col // w))
    const = lambda a: pl.BlockSpec(a.shape, lambda i: (0,) * a.ndim)
    stacked = lambda a: pl.BlockSpec((None,) + a.shape[1:], lambda i: (layer,) + (0,) * (a.ndim - 1))
    in_specs = [rowblk(GLA_V), rowblk(GLA_V), ycol(GLA_V, COL_GR), ycol(d, COL_GA), ycol(d, COL_GB), rowblk(NAT_W),
                rowblk(d), const(gain), stacked(wbg), stacked(wbn), stacked(wo), const(gpost), mspec(2), const(gpre),
                mspec(3), mspec(4)]
    args = [of, ob, y, y, y, on, x, gain, wbg, wbn, wo, gpost, mod, gpre, mod, mod]
    out_specs = [rowblk(d), rowblk(d)]
    out_shape = [jax.ShapeDtypeStruct((t, d), F32), jax.ShapeDtypeStruct((t, d), BF16)]
    scratch = []
    if router is not None:
        in_specs += [const(router[0]), const(router[1])]
        args += list(router)
        out_specs += [rowblk(LANES), pl.BlockSpec((8, LANES), lambda i: (0, 0))]
        out_shape += [jax.ShapeDtypeStruct((t, LANES), F32), jax.ShapeDtypeStruct((8, LANES), F32)]
        out_shape[1] = jax.ShapeDtypeStruct((t, d), F32)
        scratch = [pltpu.VMEM((1, LANES), F32)]
    return pl.pallas_call(
        functools.partial(_merge_kernel, with_router=router is not None),
        grid=(t // TILE,),
        in_specs=in_specs,
        out_specs=out_specs,
        out_shape=out_shape,
        scratch_shapes=scratch,
        compiler_params=_params(("arbitrary",)),
        name="merge_router" if router is not None else "merge",
    )(*args)


def _swiglu_up(h, w1, w3):
    a = _dot(h, w1)
    return ((a * jax.nn.sigmoid(a)) * _dot(h, w3)).astype(BF16)


def _ffn_kernel(h_ref, w1_ref, w3_ref, w2_ref, x_ref, gpost_ref, m5_ref, o_ref, acc_ref, u_ref):
    f = pl.program_id(1)
    last = pl.num_programs(1) - 1

    @pl.when(f == 0)
    def _():
        acc_ref[...] = jnp.zeros_like(acc_ref)
        u_ref[0] = _swiglu_up(h_ref[...], w1_ref[...], w3_ref[...])

    @pl.when((f > 0) & (f < last))
    def _():
        acc_ref[...] += _dot(u_ref[(f + 1) % 2], w2_ref[...])
        u_ref[f % 2] = _swiglu_up(h_ref[...], w1_ref[...], w3_ref[...])

    @pl.when(f == last)
    def _():
        y = acc_ref[...] + _dot(u_ref[(f + 1) % 2], w2_ref[...])
        o_ref[...] = x_ref[...] + m5_ref[...] * _rms(y, gpost_ref[...])


def _dispatch_kernel(pos_ref, h_ref, xs_in, xs_out, sem):
    del xs_in
    t = pl.num_programs(0) * TILE
    base = pl.program_id(0) * TILE

    def copies(r):
        src = h_ref.at[pl.ds(r, 1)]
        return (pltpu.make_async_copy(src, xs_out.at[pl.ds(pos_ref[base + r], 1)], sem),
                pltpu.make_async_copy(src, xs_out.at[pl.ds(pos_ref[t + base + r], 1)], sem))

    def start(r, carry):
        for queue, cp in enumerate(copies(r)):
            cp.start(priority=queue)
        return carry

    def wait(r, carry):
        for cp in copies(r):
            cp.wait()
        return carry

    for r in range(TILE):
        start(r, 0)
    lax.fori_loop(0, TILE, wait, 0, unroll=8)


def _moe_ffn_kernel(te_ref, nu_ref, x_ref, w1_ref, w3_ref, w2_ref, o_ref, xb_ref, u_ref):
    del te_ref
    i = pl.program_id(0)
    f = pl.program_id(1)
    last = pl.num_programs(1) - 1
    used = i < nu_ref[0]

    @pl.when(f == 0)
    def _():
        o_ref[...] = jnp.zeros_like(o_ref)

    @pl.when(used & (f == 0))
    def _():
        xb_ref[...] = x_ref[...].astype(BF16)
        u_ref[0] = _swiglu_up(xb_ref[...], w1_ref[0], w3_ref[0])

    @pl.when(used & (f > 0) & (f < last))
    def _():
        o_ref[...] += _dot(u_ref[(f + 1) % 2], w2_ref[0])
        u_ref[f % 2] = _swiglu_up(xb_ref[...], w1_ref[0], w3_ref[0])

    @pl.when(used & (f == last))
    def _():
        o_ref[...] += _dot(u_ref[(f + 1) % 2], w2_ref[0])


def _combine_kernel(pos_ref, ys_hbm, g_ref, x_ref, gpost_ref, m5_ref, o_ref, buf_ref, sem, *, t, tile0):
    i = pl.program_id(0)
    n = pl.num_programs(0)

    def copies(step, slot, r):
        base = (step + tile0) * TILE
        return (pltpu.make_async_copy(ys_hbm.at[pl.ds(pos_ref[base + r], 1)],
                                      buf_ref.at[slot, 0, pl.ds(r, 1)], sem.at[slot]),
                pltpu.make_async_copy(ys_hbm.at[pl.ds(pos_ref[t + base + r], 1)],
                                      buf_ref.at[slot, 1, pl.ds(r, 1)], sem.at[slot]))

    def issue(step, slot):
        for r in range(TILE):
            for queue, cp in enumerate(copies(step, slot, r)):
                cp.start(priority=queue)

    @pl.when(i == 0)
    def _():
        issue(0, 0)

    @pl.when(i + 1 < n)
    def _():
        issue(i + 1, (i + 1) % 2)

    slot = i % 2

    def wait(r, carry):
        for cp in copies(i, slot, r):
            cp.wait()
        return carry

    lax.fori_loop(0, TILE, wait, 0, unroll=8)
    g = g_ref[...]
    y = g[:, 10:11] * buf_ref[slot, 0] + g[:, 11:12] * buf_ref[slot, 1]
    o_ref[...] = x_ref[...] + m5_ref[...] * _rms(y, gpost_ref[...])


def _ffn_tiles(t, nb, nt):
    tm = 1024 if (nb * TILE) % 1024 == 0 and (nt * TILE) % 1024 == 0 else TILE
    per = tm // TILE
    row_of_tile = lambda i: jnp.where(i * per < nb, nb, (i * per - nb) // nt)
    return tm, row_of_tile


def _ffn(h, w1, w3, w2, layer, x, gpost, mod, nb, nt):
    t, d = x.shape
    dff = w1.shape[2]
    tm, row_of_tile = _ffn_tiles(t, nb, nt)
    tf = MOE_TF if dff % MOE_TF == 0 else 512
    nf = dff // tf
    up = lambda f: jnp.minimum(f, nf - 1)
    down = lambda f: jnp.maximum(f - 1, 0)
    return pl.pallas_call(
        _ffn_kernel,
        grid=(t // tm, nf + 1),
        in_specs=[pl.BlockSpec((tm, d), lambda i, f: (i, 0)),
                  pl.BlockSpec((None, d, tf), lambda i, f: (layer, 0, up(f))),
                  pl.BlockSpec((None, d, tf), lambda i, f: (layer, 0, up(f))),
                  pl.BlockSpec((None, tf, d), lambda i, f: (layer, down(f), 0)),
                  pl.BlockSpec((tm, d), lambda i, f: (i, 0)),
                  pl.BlockSpec((1, d), lambda i, f: (0, 0)),
                  pl.BlockSpec((None, None, 1, d), lambda i, f: (row_of_tile(i), 5, 0, 0))],
        out_specs=pl.BlockSpec((tm, d), lambda i, f: (i, 0)),
        out_shape=jax.ShapeDtypeStruct((t, d), F32),
        scratch_shapes=[pltpu.VMEM((tm, d), F32), pltpu.VMEM((2, tm, tf), BF16)],
        compiler_params=_params(("arbitrary", "arbitrary")),
        name="ffn_dense",
    )(h, w1, w3, w2, x, gpost, mod)


MOE_TM = 1024
MOE_TF = 512


def _moe_routed(h, pack, counts, w1, w3, w2, layer, x, gpost, mod, nb, nt, latent_only):
    t, d = x.shape
    _, ne, _, dff = w1.shape
    tm = MOE_TM
    tf = MOE_TF if dff % MOE_TF == 0 else 512
    n_tiles = -(-(2 * t) // tm) + ne
    p_rows = n_tiles * tm
    cnt = counts[0, :ne].astype(jnp.int32)
    padded = (cnt + tm - 1) // tm * tm
    ends = jnp.cumsum(padded)
    offs = ends - padded
    e1 = pack[:, 8].astype(jnp.int32)
    e2 = pack[:, 9].astype(jnp.int32)
    pos = jnp.concatenate([offs[e1] + pack[:, 12].astype(jnp.int32), offs[e2] + pack[:, 13].astype(jnp.int32)])
    n_used = (ends[-1] // tm).astype(jnp.int32).reshape(1)
    tile_start = jnp.arange(n_tiles, dtype=jnp.int32) * tm
    tile_start = jnp.minimum(tile_start, ends[-1] - tm)
    tile_expert = jnp.minimum(jnp.sum(tile_start[:, None] >= ends[None, :], axis=1), ne - 1).astype(jnp.int32)

    xs = pl.pallas_call(
        _dispatch_kernel,
        grid_spec=pltpu.PrefetchScalarGridSpec(
            num_scalar_prefetch=1, grid=(t // TILE,),
            in_specs=[pl.BlockSpec((TILE, d), lambda i, pos: (i, 0)), pl.BlockSpec(memory_space=pl.ANY)],
            out_specs=pl.BlockSpec(memory_space=pl.ANY),
            scratch_shapes=[pltpu.SemaphoreType.DMA(())]),
        out_shape=jax.ShapeDtypeStruct((p_rows, d), F32),
        input_output_aliases={2: 0},
        compiler_params=_params(("arbitrary",)),
        name="moe_dispatch",
    )(pos, h, jnp.zeros((p_rows, d), F32))

    nf = dff // tf
    row = lambda i, nu: jnp.minimum(i, nu[0] - 1)
    up = lambda i, f, nu: jnp.where(i < nu[0], jnp.minimum(f, nf - 1), nf - 1)
    down = lambda i, f, nu: jnp.where(i < nu[0], jnp.maximum(f - 1, 0), nf - 1)
    ys = pl.pallas_call(
        _moe_ffn_kernel,
        grid_spec=pltpu.PrefetchScalarGridSpec(
            num_scalar_prefetch=2, grid=(n_tiles, nf + 1),
            in_specs=[pl.BlockSpec((tm, d), lambda i, f, te, nu: (row(i, nu), 0)),
                      pl.BlockSpec((None, 1, d, tf), lambda i, f, te, nu: (layer, te[i], 0, up(i, f, nu))),
                      pl.BlockSpec((None, 1, d, tf), lambda i, f, te, nu: (layer, te[i], 0, up(i, f, nu))),
                      pl.BlockSpec((None, 1, tf, d), lambda i, f, te, nu: (layer, te[i], down(i, f, nu), 0))],
            out_specs=pl.BlockSpec((tm, d), lambda i, f, te, nu: (i, 0)),
            scratch_shapes=[pltpu.VMEM((tm, d), BF16), pltpu.VMEM((2, tm, tf), BF16)]),
        out_shape=jax.ShapeDtypeStruct((p_rows, d), F32),
        compiler_params=_params(("arbitrary", "arbitrary")),
        name="moe_ffn",
    )(tile_expert, n_used, xs, w1, w3, w2)

    tile0 = nb if latent_only else 0
    n_out = t // TILE - tile0
    row_of_tile = lambda i: jnp.where(i + tile0 < nb, nb, (i + tile0 - nb) // nt)
    return pl.pallas_call(
        functools.partial(_combine_kernel, t=t, tile0=tile0),
        grid_spec=pltpu.PrefetchScalarGridSpec(
            num_scalar_prefetch=1, grid=(n_out,),
            in_specs=[pl.BlockSpec(memory_space=pl.ANY),
                      pl.BlockSpec((TILE, LANES), lambda i, pos: (i + tile0, 0)),
                      pl.BlockSpec((TILE, d), lambda i, pos: (i + tile0, 0)),
                      pl.BlockSpec((1, d), lambda i, pos: (0, 0)),
                      pl.BlockSpec((None, None, 1, d), lambda i, pos: (row_of_tile(i), 5, 0, 0))],
            out_specs=pl.BlockSpec((TILE, d), lambda i, pos: (i, 0)),
            scratch_shapes=[pltpu.VMEM((2, 2, TILE, d), F32), pltpu.SemaphoreType.DMA((2,))]),
        out_shape=jax.ShapeDtypeStruct((n_out * TILE, d), F32),
        compiler_params=_params(("arbitrary",)),
        name="moe_combine",
    )(pos, ys, pack, x, gpost, mod)


def _reorder_w_in(w):
    a = GLA_QK * 2 + GLA_V * 2
    c0 = a + 2 * GLA_RANK
    g0 = c0 + 3 * NAT_W
    wb = w.astype(BF16)
    pad = jnp.zeros(w.shape[:2] + (LANES - 2 * GLA_RANK,), BF16)
    return jnp.concatenate([wb[..., :a], wb[..., g0:], wb[..., c0:g0], wb[..., a:c0], pad], axis=-1)


def _rope_tables(nt):
    t = np.arange(nt * TILE)
    pos = np.stack([t // GRID_W, t % GRID_W], axis=1).astype(np.float32)
    nf = GLA_DK // 4
    inv = (ROPE_BASE ** (-jnp.arange(nf, dtype=F32) / nf))
    ang = jnp.asarray(pos)[:, :, None] * inv[None, None, :]
    cos = jnp.concatenate([jnp.cos(ang), jnp.cos(ang)], axis=-1).reshape(nt * TILE, GLA_DK)
    sin = jnp.concatenate([-jnp.sin(ang), jnp.sin(ang)], axis=-1).reshape(nt * TILE, GLA_DK)
    ident = jnp.ones((TILE, GLA_DK), F32)
    return (jnp.concatenate([ident, cos], axis=0), jnp.concatenate([jnp.zeros_like(ident), sin], axis=0))


def _chunk_matrices():
    i = np.arange(TILE)
    same = (i[:, None] // GLA_CHUNK) == (i[None, :] // GLA_CHUNK)
    lower = same & (i[None, :] <= i[:, None])
    upper = same & (i[None, :] >= i[:, None])
    cum = jnp.asarray(np.stack([lower, upper]).astype(np.float32), BF16)
    return cum, jnp.asarray(same.astype(np.float32), BF16)


def _hi_lo(w):
    hi = w.astype(BF16)
    return hi, (w - hi.astype(F32)).astype(BF16)


def kernel(x, c, ctx, c_ctx, w_mod, b_mod, norm_mix_pre, norm_mix_post, w_in, gla_gate_w2, gla_gate_b, gla_norm,
           nat_rpb, w_branch_gla, w_branch_nat, w_out, norm_ffn_pre, norm_ffn_post, ffn_w1, ffn_w3, ffn_w2,
           moe_router, moe_w1, moe_w3, moe_w2):
    nb, seq, d = x.shape
    depth = w_mod.shape[0]
    assert ctx.shape[1] == TILE and seq % TILE == 0 and d % LANES == 0
    nt = seq // TILE
    mod_rows = -(-(nb + 1) // 8) * 8
    c_rows = jnp.concatenate([c, c_ctx[None], jnp.zeros((mod_rows - nb - 1, d), c.dtype)], axis=0)
    mod_all = _modulation(c_rows, w_mod, b_mod).reshape(depth, mod_rows, 6, 1, d)
    xs = jnp.concatenate([ctx.reshape(nb * TILE, d), x.reshape(nb * seq, d)], axis=0)
    rope_cos, rope_sin = _rope_tables(nt)
    cum_m, tot_m = _chunk_matrices()
    row = lambda v: v.reshape(1, -1)
    w_in_b = _reorder_w_in(w_in)
    wbg_b, wbn_b, wo_b = w_branch_gla.astype(BF16), w_branch_nat.astype(BF16), w_out.astype(BF16)
    ffn_b = [w.astype(BF16) for w in (ffn_w1, ffn_w3, ffn_w2)]
    moe_b = [w.astype(BF16) for w in (moe_w1, moe_w3, moe_w2)]
    nat_bias = _nat_bias_base(nat_rpb)
    nat_masks = _nat_row_masks(nt)
    w2p = jnp.zeros((depth, 2, LANES, GLA_QK), F32)
    w2p = w2p.at[:, 0, :GLA_RANK].set(gla_gate_w2[:, 0]).at[:, 1, GLA_RANK:2 * GLA_RANK].set(gla_gate_w2[:, 1])
    w2hi, w2lo = _hi_lo(w2p)
    wr = jnp.zeros((moe_router.shape[0], d, LANES), F32).at[:, :, :N_EXPERTS].set(moe_router)
    wr_hi, wr_lo = _hi_lo(wr)
    for i in range(depth):
        mod = mod_all[i]
        last = i == depth - 1
        y = _in_projection(xs, row(norm_mix_pre[i]), mod, rope_cos, rope_sin, w_in_b, i, nb, nt)
        of, ob = _gla(y, w2hi[i], w2lo[i], gla_gate_b[i].reshape(2, 1, GLA_QK), cum_m, tot_m, nb, nt)
        on = _nat(y, nat_bias, i, nat_masks, nb, nt)
        is_moe = i % 2 == 1
        j = i // 2
        router = (wr_hi[j], wr_lo[j]) if is_moe else None
        outs = _merge(y, of, ob, on, xs, row(gla_norm[i]), wbg_b, wbn_b, wo_b, i, row(norm_mix_post[i]),
                      row(norm_ffn_pre[i]), mod, router, nb, nt)
        if is_moe:
            xs, h, pack, counts = outs
            xs = _moe_routed(h, pack, counts, *moe_b, j, xs, row(norm_ffn_post[i]), mod, nb, nt, latent_only=last)
        else:
            xs, h = outs
            xs = _ffn(h, *ffn_b, j, xs, row(norm_ffn_post[i]), mod, nb, nt)
            if last:
                xs = xs[nb * TILE:]
    return xs.reshape(nb, seq, d)
```

```python
import functools

import numpy as np
import jax
import jax.numpy as jnp
from jax import lax
from jax.experimental import pallas as pl
from jax.experimental.pallas import tpu as pltpu

EPS = 1e-6
GRID_W = 64
TILE = 256
TILE_ROWS = TILE // GRID_W
GLA_HEADS, GLA_DK, GLA_DV, GLA_RANK, GLA_TAU, GLA_CHUNK = 4, 128, 256, 16, 16.0, 64
NAT_HEADS, NAT_DH = 8, 64
WIN_H, WIN_W = 8, 16
N_EXPERTS = 8
ROPE_BASE = 10000.0
GLA_QK = GLA_HEADS * GLA_DK
GLA_V = GLA_HEADS * GLA_DV
NAT_W = NAT_HEADS * NAT_DH
LANES = 128
COL_GQ, COL_GK, COL_GV, COL_GR, COL_GA, COL_GB = 0, 512, 1024, 2048, 3072, 4096
COL_NQ, COL_NK, COL_NV, COL_CODE = 5120, 5632, 6144, 6656
N_IN = COL_CODE + LANES
VMEM_LIMIT = 56 * 1024 * 1024
BF16 = jnp.bfloat16
F32 = jnp.float32


def _dot(a, b):
    return jnp.dot(a, b, preferred_element_type=F32)


def _dot_nt(a, b):
    return lax.dot_general(a, b, (((1,), (1,)), ((), ())), preferred_element_type=F32)


def _dot_tn(a, b):
    return lax.dot_general(a, b, (((0,), (0,)), ((), ())), preferred_element_type=F32)


def _rms(x, g):
    return x * lax.rsqrt(jnp.mean(x * x, axis=-1, keepdims=True) + EPS) * g


def _split_dot(a, b_hi, b_lo):
    a_hi = a.astype(BF16)
    a_lo = (a - a_hi.astype(F32)).astype(BF16)
    return _dot(a_hi, b_hi) + (_dot(a_lo, b_hi) + _dot(a_hi, b_lo))


def _params(sem):
    return pltpu.CompilerParams(dimension_semantics=sem, vmem_limit_bytes=VMEM_LIMIT)


def _mod_kernel(c_ref, w_ref, b_ref, o_ref):
    c = c_ref[...]
    s = c * jax.nn.sigmoid(c)
    o_ref[0] = _dot(s.astype(BF16), w_ref[0].astype(BF16)) + b_ref[0]


def _modulation(c_rows, w_mod, b_mod):
    depth, d, n = w_mod.shape
    rows = c_rows.shape[0]
    tn = 1536
    return pl.pallas_call(
        _mod_kernel,
        grid=(depth, n // tn),
        in_specs=[pl.BlockSpec((rows, d), lambda l, j: (0, 0)),
                  pl.BlockSpec((1, d, tn), lambda l, j: (l, 0, j)),
                  pl.BlockSpec((1, 1, tn), lambda l, j: (l, 0, j))],
        out_specs=pl.BlockSpec((1, rows, tn), lambda l, j: (l, 0, j)),
        out_shape=jax.ShapeDtypeStruct((depth, rows, n), F32),
        compiler_params=_params(("arbitrary", "arbitrary")),
        name="modulation",
    )(c_rows, w_mod, b_mod.reshape(depth, 1, n))


def _inproj_kernel(x_ref, g_ref, sh_ref, sc_ref, cos_ref, sin_ref, w_ref, o_ref):
    h = _rms(x_ref[...], g_ref[...]) * (1.0 + sc_ref[...]) + sh_ref[...]
    hb = h.astype(BF16)
    n = o_ref.shape[1]
    qk = _dot(hb, w_ref[:, :2 * GLA_QK])
    lane = lax.broadcasted_iota(jnp.int32, qk.shape, 1)
    reps = 2 * GLA_HEADS
    cos = jnp.concatenate([cos_ref[...]] * reps, axis=1)
    sin = jnp.concatenate([sin_ref[...]] * reps, axis=1)
    partner = jnp.where((lane % 64) < 32, pltpu.roll(qk, 2 * GLA_QK - 32, 1), pltpu.roll(qk, 32, 1))
    qk = qk * cos + partner * sin
    qk = jnp.where(lane < GLA_QK, qk * (GLA_DK ** -0.5), qk)
    o_ref[:, :2 * GLA_QK] = qk.astype(o_ref.dtype)
    step = 1024
    for j in range(2 * GLA_QK, n, step):
        w = min(step, n - j)
        o_ref[:, j:j + w] = _dot(hb, w_ref[:, j:j + w]).astype(o_ref.dtype)


def _in_projection(x, g, mod, rope_cos, rope_sin, w, layer, nb, nt):
    t, d = x.shape
    n = w.shape[2]
    row_of_tile = lambda i: jnp.where(i < nb, nb, (i - nb) // nt)
    rope_tile = lambda i: jnp.where(i < nb, 0, 1 + (i - nb) % nt)
    dm = mod.shape[-1]
    mspec = lambda comp: pl.BlockSpec((None, None, 1, dm), lambda i: (row_of_tile(i), comp, 0, 0))
    return pl.pallas_call(
        _inproj_kernel,
        grid=(t // TILE,),
        in_specs=[pl.BlockSpec((TILE, d), lambda i: (i, 0)),
                  pl.BlockSpec((1, d), lambda i: (0, 0)),
                  mspec(0), mspec(1),
                  pl.BlockSpec((TILE, LANES), lambda i: (rope_tile(i), 0)),
                  pl.BlockSpec((TILE, LANES), lambda i: (rope_tile(i), 0)),
                  pl.BlockSpec((None, d, n), lambda i: (layer, 0, 0))],
        out_specs=pl.BlockSpec((TILE, n), lambda i: (i, 0)),
        out_shape=jax.ShapeDtypeStruct((t, n), BF16),
        compiler_params=_params(("arbitrary",)),
        name="in_projection",
    )(x, g, mod, mod, rope_cos, rope_sin, w)


def _gla_prepare(d, q_ref, k_ref, code_ref, w2hi_ref, w2lo_ref, gb_ref, cum_ref, tot_ref, ops_ref):
    z = _dot(code_ref[...], w2hi_ref[d]) + _dot(code_ref[...], w2lo_ref[d]) + gb_ref[d]
    lg = (jnp.minimum(z, 0.0) - jnp.log(1.0 + jnp.exp(-jnp.abs(z)))) * (1.0 / GLA_TAU)
    lg_hi = lg.astype(BF16)
    lg_lo = (lg - lg_hi.astype(F32)).astype(BF16)
    cum = _dot(cum_ref[d], lg_hi) + _dot(cum_ref[d], lg_lo)
    tot = _dot(tot_ref[...], lg_hi) + _dot(tot_ref[...], lg_lo)
    k = k_ref[...].astype(F32)
    qe = q_ref[...].astype(F32) * jnp.exp(cum)
    kd = k * jnp.exp(tot - cum)
    ops_ref[d, OP_QE] = qe.astype(BF16)
    ops_ref[d, OP_KE] = (k * jnp.exp(-cum)).astype(BF16)
    n = TILE // GLA_CHUNK
    order = list(range(n)) if d == 0 else list(range(n - 1, -1, -1))
    tc = [tot[c * GLA_CHUNK:c * GLA_CHUNK + 1, :] for c in order]
    zero = jnp.zeros_like(tc[0])
    u_log = [zero, -tc[1], zero, tc[2]]
    w_log = [tc[1], zero, -tc[2], zero]
    for p, c in enumerate(order):
        rows = slice(c * GLA_CHUNK, (c + 1) * GLA_CHUNK)
        before = sum(tc[:p], zero)
        after = sum(tc[p + 1:], zero)
        ops_ref[d, OP_QU, rows] = (qe[rows] * jnp.exp(u_log[p])).astype(BF16)
        ops_ref[d, OP_KW, rows] = (kd[rows] * jnp.exp(w_log[p])).astype(BF16)
        ops_ref[d, OP_QP, rows] = (qe[rows] * jnp.exp(before)).astype(BF16)
        ops_ref[d, OP_KS, rows] = (kd[rows] * jnp.exp(after)).astype(BF16)
    return jnp.exp(sum(tc, zero))


OP_QE, OP_KE, OP_QU, OP_KW, OP_QP, OP_KS = range(6)


def _gla_kernel(qf, kf, vf, cf, qb, kb, vb, cb, w2hi, w2lo, gb, cum, tot, of, ob, st_ref, ops_ref):
    assert TILE // GLA_CHUNK == 4

    @pl.when(pl.program_id(1) == 0)
    def _():
        st_ref[...] = jnp.zeros_like(st_ref)

    decay = [_gla_prepare(0, qf, kf, cf, w2hi, w2lo, gb, cum, tot, ops_ref),
             _gla_prepare(1, qb, kb, cb, w2hi, w2lo, gb, cum, tot, ops_ref)]
    r = lax.broadcasted_iota(jnp.int32, (TILE, TILE), 0)
    s = lax.broadcasted_iota(jnp.int32, (TILE, TILE), 1)
    for d, (v_ref, o_ref) in enumerate(((vf, of), (vb, ob))):
        gap = (r // GLA_CHUNK - s // GLA_CHUNK) * (1 if d == 0 else -1)
        diag = (gap == 0) & ((s <= r) if d == 0 else (s >= r))
        for h in range(GLA_HEADS):
            kcols = slice(h * GLA_DK, (h + 1) * GLA_DK)
            vcols = slice(h * GLA_DV, (h + 1) * GLA_DV)
            op = lambda which: ops_ref[d, which, :, kcols]
            a = jnp.where(diag, _dot_nt(op(OP_QE), op(OP_KE)),
                          jnp.where(gap >= 1, _dot_nt(op(OP_QU), op(OP_KW)), 0.0))
            v = v_ref[:, vcols]
            st = st_ref[d, h]
            o = _dot(a.astype(BF16), v) + _dot_nt(op(OP_QP), st.astype(BF16))
            o_ref[:, vcols] = o.astype(o_ref.dtype)
            st_ref[d, h] = st * decay[d][:, kcols] + _dot_tn(v, op(OP_KS))


def _gla(y, w2hi, w2lo, gbias, cum_m, tot_m, nb, nt):
    t = y.shape[0]
    fwd = lambda b, j: jnp.where(j == 0, b, nb + b * nt + j - 1)
    bwd = lambda b, j: jnp.where(j == 0, b, nb + b * nt + nt - j)

    def ysl(width, col, tile):
        return pl.BlockSpec((TILE, width), lambda b, j: (tile(b, j), col // width))

    def direction(tile):
        return [ysl(GLA_QK, COL_GQ, tile), ysl(GLA_QK, COL_GK, tile), ysl(GLA_V, COL_GV, tile),
                ysl(LANES, COL_CODE, tile)]

    const = lambda shape: pl.BlockSpec(shape, lambda b, j: (0,) * len(shape))
    in_specs = (direction(fwd) + direction(bwd)
                + [const(w2hi.shape), const(w2lo.shape), const(gbias.shape), const(cum_m.shape), const(tot_m.shape)])
    args = [y] * 8 + [w2hi, w2lo, gbias, cum_m, tot_m]
    return pl.pallas_call(
        _gla_kernel,
        grid=(nb, nt + 1),
        in_specs=in_specs,
        out_specs=[pl.BlockSpec((TILE, GLA_V), lambda b, j: (fwd(b, j), 0)),
                   pl.BlockSpec((TILE, GLA_V), lambda b, j: (bwd(b, j), 0))],
        out_shape=[jax.ShapeDtypeStruct((t, GLA_V), BF16)] * 2,
        scratch_shapes=[pltpu.VMEM((2, GLA_HEADS, GLA_DV, GLA_DK), F32),
                        pltpu.VMEM((2, 6, TILE, GLA_QK), BF16)],
        compiler_params=_params(("arbitrary", "arbitrary")),
        name="gla_scan",
    )(*args)


def _nat_kernel(q_ref, k0, k1, k2, kc, v0, v1, v2, vc, bias_ref, mask_ref, o_ref):
    lane = lax.broadcasted_iota(jnp.int32, (TILE, LANES), 1)
    low = lane < NAT_DH
    row_mask = mask_ref[...]
    for p in range(NAT_HEADS // 2):
        cols = slice(p * LANES, (p + 1) * LANES)
        qp = q_ref[:, cols] * (NAT_DH ** -0.5)
        ks = [r[:, cols] for r in (k0, k1, k2, kc)]
        vs = [r[:, cols] for r in (v0, v1, v2, vc)]
        zero = jnp.zeros_like(qp)
        qs = jnp.concatenate([jnp.where(low, qp, zero), jnp.where(low, zero, qp)], axis=0)
        s_lat2 = jnp.concatenate([_dot_nt(qs, kk) for kk in ks[:3]], axis=1)
        s_ctx2 = _dot_nt(qs, ks[3])
        e_lats, e_ctxs, dens = [], [], []
        for half in range(2):
            rows = slice(half * TILE, (half + 1) * TILE)
            s_lat = s_lat2[rows] + (bias_ref[2 * p + half] + row_mask)
            s_ctx = s_ctx2[rows]
            m = jnp.maximum(jnp.max(s_lat, axis=-1, keepdims=True), jnp.max(s_ctx, axis=-1, keepdims=True))
            e_lat = jnp.exp(s_lat - m)
            e_ctx = jnp.exp(s_ctx - m)
            dens.append(jnp.sum(e_lat, axis=-1, keepdims=True) + jnp.sum(e_ctx, axis=-1, keepdims=True))
            e_lats.append(e_lat.astype(BF16))
            e_ctxs.append(e_ctx.astype(BF16))
        e_lat2 = jnp.concatenate(e_lats, axis=0)
        acc = _dot(jnp.concatenate(e_ctxs, axis=0), vs[3])
        for i in range(3):
            acc += _dot(e_lat2[:, i * TILE:(i + 1) * TILE], vs[i])
        o_ref[:, cols] = jnp.where(low, acc[:TILE] / dens[0], acc[TILE:] / dens[1]).astype(o_ref.dtype)


def _nat(y, bias, layer, row_masks, nb, nt):
    t = y.shape[0]
    lat = lambda b, tt: nb + b * nt + jnp.clip(tt, 0, nt - 1)
    qtile = lambda b, j: jnp.where(j == 0, b, nb + b * nt + j - 1)
    pattern = lambda b, j: jnp.where(j == 0, 3, jnp.where(j == 1, 0, jnp.where(j == nt, 2, 1)))

    def ysl(col, tile):
        return pl.BlockSpec((TILE, NAT_W), lambda b, j: (tile(b, j), col // NAT_W))

    slab = [lambda b, j: lat(b, j - 2), lambda b, j: lat(b, j - 1), lambda b, j: lat(b, j), lambda b, j: b]
    in_specs = ([ysl(COL_NQ, qtile)] + [ysl(COL_NK, s) for s in slab] + [ysl(COL_NV, s) for s in slab]
                + [pl.BlockSpec((None, NAT_HEADS, TILE, 3 * TILE), lambda b, j: (layer, 0, 0, 0)),
                   pl.BlockSpec((None, TILE, 3 * TILE), lambda b, j: (pattern(b, j), 0, 0))])
    return pl.pallas_call(
        _nat_kernel,
        grid=(nb, nt + 1),
        in_specs=in_specs,
        out_specs=pl.BlockSpec((TILE, NAT_W), lambda b, j: (qtile(b, j), 0)),
        out_shape=jax.ShapeDtypeStruct((t, NAT_W), BF16),
        compiler_params=_params(("arbitrary", "arbitrary")),
        name="nat_attention",
    )(*([y] * 9), bias, row_masks)


def _nat_bias_base(rpb):
    nr, nc = 2 * WIN_H - 1, 2 * WIN_W - 1
    qi = np.arange(TILE_ROWS)
    kj = np.arange(3 * TILE_ROWS)
    qc = np.arange(GRID_W)
    kc = np.arange(GRID_W)
    col_start = np.clip(qc - WIN_W // 2, 0, GRID_W - WIN_W)
    col_ok = (kc[None, :] >= col_start[:, None]) & (kc[None, :] < col_start[:, None] + WIN_W)
    col_off = np.clip(kc[None, :] - qc[:, None], 1 - WIN_W, WIN_W - 1) + (WIN_W - 1)
    row_off = kj[None, :] - qi[:, None] + (WIN_H - 1 - TILE_ROWS)
    assert row_off.min() >= 0 and row_off.max() < nr
    col_sel = ((col_off[None] == np.arange(nc)[:, None, None]) & col_ok[None]).astype(np.float32)
    row_sel = (row_off[None] == np.arange(nr)[:, None, None]).astype(np.float32)
    b = jnp.einsum("dhab,aik,bcl->dhickl", rpb.astype(F32), row_sel, col_sel, precision=lax.Precision.HIGHEST)
    b = b + jnp.where(jnp.asarray(col_ok), 0.0, -jnp.inf)[None, None, None, :, None, :]
    return b.reshape(rpb.shape[0], NAT_HEADS, TILE, 3 * TILE)


def _nat_row_masks(nt):
    rows = nt * TILE_ROWS
    kh = min(WIN_H, rows)
    qi = np.arange(TILE_ROWS)
    kj = np.arange(3 * TILE_ROWS)
    masks = []
    for t in (0, min(1, nt - 1), nt - 1):
        r = TILE_ROWS * t + qi
        row_start = np.clip(r - kh // 2, 0, rows - kh)
        kr = TILE_ROWS * (t - 1) + kj
        valid = (kr >= 0) & (kr < rows)
        masks.append((kr[None, :] >= row_start[:, None]) & (kr[None, :] < row_start[:, None] + kh) & valid[None, :])
    masks.append(np.zeros_like(masks[0]))
    m = np.where(np.stack(masks), 0.0, -np.inf).astype(np.float32)
    m = jnp.broadcast_to(jnp.asarray(m)[:, :, None, :, None], (4, TILE_ROWS, GRID_W, 3 * TILE_ROWS, GRID_W))
    return m.reshape(4, TILE, 3 * TILE)


def _merge_kernel(of_ref, ob_ref, gr_ref, ga_ref, gb_ref, on_ref, x_ref, gain_ref, wbg_ref, wbn_ref, wo_ref,
                  gpost_ref, m2_ref, gpre_ref, m3_ref, m4_ref, *rest, with_router):
    if with_router:
        wrhi_ref, wrlo_ref, x_out, h_out, gate_out, count_out, count_ref = rest
    else:
        x_out, h_out = rest
    o = of_ref[...].astype(F32) + ob_ref[...].astype(F32)
    parts = []
    for h in range(GLA_HEADS):
        oh = o[:, h * GLA_DV:(h + 1) * GLA_DV]
        parts.append(oh * lax.rsqrt(jnp.mean(oh * oh, axis=-1, keepdims=True) + EPS))
    r = gr_ref[...].astype(F32)
    yg = jnp.concatenate(parts, axis=1) * gain_ref[...] * (r * jax.nn.sigmoid(r))
    m = (jax.nn.sigmoid(ga_ref[...].astype(F32)) * _dot(yg.astype(BF16), wbg_ref[...])
         + jax.nn.sigmoid(gb_ref[...].astype(F32)) * _dot(on_ref[...], wbn_ref[...]))
    y = _dot(m.astype(BF16), wo_ref[...])
    x1 = x_ref[...] + m2_ref[...] * _rms(y, gpost_ref[...])
    x_out[...] = x1
    h = _rms(x1, gpre_ref[...]) * (1.0 + m4_ref[...]) + m3_ref[...]
    h_out[...] = h.astype(h_out.dtype)
    if with_router:
        logits = _split_dot(h, wrhi_ref[...], wrlo_ref[...])
        lane = lax.broadcasted_iota(jnp.int32, logits.shape, 1)
        lg = jnp.where(lane < N_EXPERTS, logits, -jnp.inf)
        v1 = jnp.max(lg, axis=-1, keepdims=True)
        i1 = jnp.min(jnp.where(lg == v1, lane, LANES), axis=-1, keepdims=True)
        lg2 = jnp.where(lane == i1, -jnp.inf, lg)
        v2 = jnp.max(lg2, axis=-1, keepdims=True)
        i2 = jnp.min(jnp.where(lg2 == v2, lane, LANES), axis=-1, keepdims=True)
        e2 = jnp.exp(v2 - v1)
        w1 = 1.0 / (1.0 + e2)
        w2 = e2 / (1.0 + e2)

        @pl.when(pl.program_id(0) == 0)
        def _():
            count_ref[...] = jnp.zeros_like(count_ref)

        oh1 = (lane == i1).astype(F32)
        oh2 = (lane == i2).astype(F32)
        rr = lax.broadcasted_iota(jnp.int32, (TILE, TILE), 0)
        ss = lax.broadcasted_iota(jnp.int32, (TILE, TILE), 1)
        before = (ss < rr).astype(BF16)
        carry = count_ref[...]
        cnt1 = jnp.sum(oh1, axis=0, keepdims=True)
        rank1 = jnp.sum(oh1 * (_dot(before, oh1.astype(BF16)) + carry), axis=-1, keepdims=True)
        rank2 = jnp.sum(oh2 * (_dot(before, oh2.astype(BF16)) + (carry + cnt1)), axis=-1, keepdims=True)
        total = carry + cnt1 + jnp.sum(oh2, axis=0, keepdims=True)
        count_ref[...] = total
        count_out[...] = jnp.broadcast_to(total, count_out.shape)
        packed = jnp.zeros_like(logits)
        for ln, val in ((8, i1.astype(F32)), (9, i2.astype(F32)), (10, w1), (11, w2), (12, rank1), (13, rank2)):
            packed = jnp.where(lane == ln, val, packed)
        gate_out[...] = packed


def _merge(y, of, ob, on, x, gain, wbg, wbn, wo, layer, gpost, gpre, mod, router, nb, nt, skip_context):
    d = x.shape[1]
    tile0 = nb if skip_context else 0
    t = x.shape[0] - tile0 * TILE
    row_of_tile = lambda i: jnp.where(i + tile0 < nb, nb, (i + tile0 - nb) // nt)
    mspec = lambda comp: pl.BlockSpec((None, None, 1, d), lambda i: (row_of_tile(i), comp, 0, 0))
    rowblk = lambda w: pl.BlockSpec((TILE, w), lambda i: (i + tile0, 0))
    outblk = lambda w: pl.BlockSpec((TILE, w), lambda i: (i, 0))
    ycol = lambda w, col: pl.BlockSpec((TILE, w), lambda i: (i + tile0, col // w))
    const = lambda a: pl.BlockSpec(a.shape, lambda i: (0,) * a.ndim)
    stacked = lambda a: pl.BlockSpec((None,) + a.shape[1:], lambda i: (layer,) + (0,) * (a.ndim - 1))
    in_specs = [rowblk(GLA_V), rowblk(GLA_V), ycol(GLA_V, COL_GR), ycol(d, COL_GA), ycol(d, COL_GB), rowblk(NAT_W),
                rowblk(d), const(gain), stacked(wbg), stacked(wbn), stacked(wo), const(gpost), mspec(2), const(gpre),
                mspec(3), mspec(4)]
    args = [of, ob, y, y, y, on, x, gain, wbg, wbn, wo, gpost, mod, gpre, mod, mod]
    out_specs = [outblk(d), outblk(d)]
    out_shape = [jax.ShapeDtypeStruct((t, d), F32), jax.ShapeDtypeStruct((t, d), BF16)]
    scratch = []
    if router is not None:
        in_specs += [const(router[0]), const(router[1])]
        args += list(router)
        out_specs += [outblk(LANES), pl.BlockSpec((8, LANES), lambda i: (0, 0))]
        out_shape += [jax.ShapeDtypeStruct((t, LANES), F32), jax.ShapeDtypeStruct((8, LANES), F32)]
        out_shape[1] = jax.ShapeDtypeStruct((t, d), F32)
        scratch = [pltpu.VMEM((1, LANES), F32)]
    return pl.pallas_call(
        functools.partial(_merge_kernel, with_router=router is not None),
        grid=(t // TILE,),
        in_specs=in_specs,
        out_specs=out_specs,
        out_shape=out_shape,
        scratch_shapes=scratch,
        compiler_params=_params(("arbitrary",)),
        name="merge_router" if router is not None else "merge",
    )(*args)


def _ffn_kernel(h_ref, w1_ref, w3_ref, w2_ref, x_ref, gpost_ref, m5_ref, o_ref, acc_ref):
    f = pl.program_id(1)

    @pl.when(f == 0)
    def _():
        acc_ref[...] = jnp.zeros_like(acc_ref)

    h = h_ref[...]
    a = _dot(h, w1_ref[...].astype(BF16))
    u = (a * jax.nn.sigmoid(a)) * _dot(h, w3_ref[...].astype(BF16))
    acc_ref[...] += _dot(u.astype(BF16), w2_ref[...].astype(BF16))

    @pl.when(f == pl.num_programs(1) - 1)
    def _():
        o_ref[...] = x_ref[...] + m5_ref[...] * _rms(acc_ref[...], gpost_ref[...])


def _dispatch_kernel(pos_ref, h_ref, xs_in, xs_out, sem):
    del xs_in
    t = pl.num_programs(0) * TILE
    base = pl.program_id(0) * TILE

    def copies(r):
        src = h_ref.at[pl.ds(r, 1)]
        return (pltpu.make_async_copy(src, xs_out.at[pl.ds(pos_ref[base + r], 1)], sem),
                pltpu.make_async_copy(src, xs_out.at[pl.ds(pos_ref[t + base + r], 1)], sem))

    def start(r, carry):
        for queue, cp in enumerate(copies(r)):
            cp.start(priority=queue)
        return carry

    def wait(r, carry):
        for cp in copies(r):
            cp.wait()
        return carry

    for r in range(TILE):
        start(r, 0)
    lax.fori_loop(0, TILE, wait, 0, unroll=8)


def _moe_ffn_kernel(te_ref, nu_ref, x_ref, w1_ref, w3_ref, w2_ref, o_ref, xb_ref):
    del te_ref
    i = pl.program_id(0)
    f = pl.program_id(1)

    @pl.when(i < nu_ref[0])
    def _():
        @pl.when(f == 0)
        def _():
            xb_ref[...] = x_ref[...].astype(BF16)

        h = xb_ref[...]
        a = _dot(h, w1_ref[0].astype(BF16))
        u = (a * jax.nn.sigmoid(a)) * _dot(h, w3_ref[0].astype(BF16))
        y = _dot(u.astype(BF16), w2_ref[0].astype(BF16))

        @pl.when(f == 0)
        def _():
            o_ref[...] = y

        @pl.when(f > 0)
        def _():
            o_ref[...] += y

    @pl.when((i >= nu_ref[0]) & (f == 0))
    def _():
        o_ref[...] = jnp.zeros_like(o_ref)


def _combine_kernel(pos_ref, ys_hbm, g_ref, x_ref, gpost_ref, m5_ref, o_ref, buf_ref, sem, *, t, tile0):
    i = pl.program_id(0)
    n = pl.num_programs(0)

    def copies(step, slot, r):
        base = (step + tile0) * TILE
        return (pltpu.make_async_copy(ys_hbm.at[pl.ds(pos_ref[base + r], 1)],
                                      buf_ref.at[slot, 0, pl.ds(r, 1)], sem.at[slot]),
                pltpu.make_async_copy(ys_hbm.at[pl.ds(pos_ref[t + base + r], 1)],
                                      buf_ref.at[slot, 1, pl.ds(r, 1)], sem.at[slot]))

    def issue(step, slot):
        for r in range(TILE):
            for queue, cp in enumerate(copies(step, slot, r)):
                cp.start(priority=queue)

    @pl.when(i == 0)
    def _():
        issue(0, 0)

    @pl.when(i + 1 < n)
    def _():
        issue(i + 1, (i + 1) % 2)

    slot = i % 2

    def wait(r, carry):
        for cp in copies(i, slot, r):
            cp.wait()
        return carry

    lax.fori_loop(0, TILE, wait, 0, unroll=8)
    g = g_ref[...]
    y = g[:, 10:11] * buf_ref[slot, 0] + g[:, 11:12] * buf_ref[slot, 1]
    o_ref[...] = x_ref[...] + m5_ref[...] * _rms(y, gpost_ref[...])


def _ffn_tiles(t, nb, nt):
    tm = 1024 if (nb * TILE) % 1024 == 0 and (nt * TILE) % 1024 == 0 else TILE
    per = tm // TILE
    row_of_tile = lambda i: jnp.where(i * per < nb, nb, (i * per - nb) // nt)
    return tm, row_of_tile


def _ffn(h, w1, w3, w2, layer, x, gpost, mod, nb, nt):
    t, d = x.shape
    dff = w1.shape[2]
    tm, row_of_tile = _ffn_tiles(t, nb, nt)
    tf = 512
    return pl.pallas_call(
        _ffn_kernel,
        grid=(t // tm, dff // tf),
        in_specs=[pl.BlockSpec((tm, d), lambda i, f: (i, 0)),
                  pl.BlockSpec((None, d, tf), lambda i, f: (layer, 0, f)),
                  pl.BlockSpec((None, d, tf), lambda i, f: (layer, 0, f)),
                  pl.BlockSpec((None, tf, d), lambda i, f: (layer, f, 0)),
                  pl.BlockSpec((tm, d), lambda i, f: (i, 0)),
                  pl.BlockSpec((1, d), lambda i, f: (0, 0)),
                  pl.BlockSpec((None, None, 1, d), lambda i, f: (row_of_tile(i), 5, 0, 0))],
        out_specs=pl.BlockSpec((tm, d), lambda i, f: (i, 0)),
        out_shape=jax.ShapeDtypeStruct((t, d), F32),
        scratch_shapes=[pltpu.VMEM((tm, d), F32)],
        compiler_params=_params(("arbitrary", "arbitrary")),
        name="ffn_dense",
    )(h, w1, w3, w2, x, gpost, mod)


MOE_TM = 1024
MOE_TF = 512


def _moe_routed(h, pack, counts, w1, w3, w2, layer, x, gpost, mod, nb, nt, nctx):
    t, d = x.shape
    _, ne, _, dff = w1.shape
    tm = MOE_TM
    tf = MOE_TF if dff % MOE_TF == 0 else 512
    n_tiles = -(-(2 * t) // tm) + ne
    p_rows = n_tiles * tm
    cnt = counts[0, :ne].astype(jnp.int32)
    padded = (cnt + tm - 1) // tm * tm
    ends = jnp.cumsum(padded)
    offs = ends - padded
    e1 = pack[:, 8].astype(jnp.int32)
    e2 = pack[:, 9].astype(jnp.int32)
    pos = jnp.concatenate([offs[e1] + pack[:, 12].astype(jnp.int32), offs[e2] + pack[:, 13].astype(jnp.int32)])
    n_used = (ends[-1] // tm).astype(jnp.int32).reshape(1)
    tile_start = jnp.arange(n_tiles, dtype=jnp.int32) * tm
    tile_start = jnp.minimum(tile_start, ends[-1] - tm)
    tile_expert = jnp.minimum(jnp.sum(tile_start[:, None] >= ends[None, :], axis=1), ne - 1).astype(jnp.int32)

    xs = pl.pallas_call(
        _dispatch_kernel,
        grid_spec=pltpu.PrefetchScalarGridSpec(
            num_scalar_prefetch=1, grid=(t // TILE,),
            in_specs=[pl.BlockSpec((TILE, d), lambda i, pos: (i, 0)), pl.BlockSpec(memory_space=pl.ANY)],
            out_specs=pl.BlockSpec(memory_space=pl.ANY),
            scratch_shapes=[pltpu.SemaphoreType.DMA(())]),
        out_shape=jax.ShapeDtypeStruct((p_rows, d), F32),
        input_output_aliases={2: 0},
        compiler_params=_params(("arbitrary",)),
        name="moe_dispatch",
    )(pos, h, jnp.zeros((p_rows, d), F32))

    nf = dff // tf
    row = lambda i, nu: jnp.minimum(i, nu[0] - 1)
    fcol = lambda i, f, nu: jnp.where(i < nu[0], f, nf - 1)
    ys = pl.pallas_call(
        _moe_ffn_kernel,
        grid_spec=pltpu.PrefetchScalarGridSpec(
            num_scalar_prefetch=2, grid=(n_tiles, nf),
            in_specs=[pl.BlockSpec((tm, d), lambda i, f, te, nu: (row(i, nu), 0)),
                      pl.BlockSpec((None, 1, d, tf), lambda i, f, te, nu: (layer, te[i], 0, fcol(i, f, nu))),
                      pl.BlockSpec((None, 1, d, tf), lambda i, f, te, nu: (layer, te[i], 0, fcol(i, f, nu))),
                      pl.BlockSpec((None, 1, tf, d), lambda i, f, te, nu: (layer, te[i], fcol(i, f, nu), 0))],
            out_specs=pl.BlockSpec((tm, d), lambda i, f, te, nu: (i, 0)),
            scratch_shapes=[pltpu.VMEM((tm, d), BF16)]),
        out_shape=jax.ShapeDtypeStruct((p_rows, d), F32),
        compiler_params=_params(("arbitrary", "arbitrary")),
        name="moe_ffn",
    )(tile_expert, n_used, xs, w1, w3, w2)

    tile0 = 0
    n_out = t // TILE
    row_of_tile = lambda i: jnp.where(i < nctx, nb, (i - nctx) // nt)
    return pl.pallas_call(
        functools.partial(_combine_kernel, t=t, tile0=tile0),
        grid_spec=pltpu.PrefetchScalarGridSpec(
            num_scalar_prefetch=1, grid=(n_out,),
            in_specs=[pl.BlockSpec(memory_space=pl.ANY),
                      pl.BlockSpec((TILE, LANES), lambda i, pos: (i + tile0, 0)),
                      pl.BlockSpec((TILE, d), lambda i, pos: (i + tile0, 0)),
                      pl.BlockSpec((1, d), lambda i, pos: (0, 0)),
                      pl.BlockSpec((None, None, 1, d), lambda i, pos: (row_of_tile(i), 5, 0, 0))],
            out_specs=pl.BlockSpec((TILE, d), lambda i, pos: (i, 0)),
            scratch_shapes=[pltpu.VMEM((2, 2, TILE, d), F32), pltpu.SemaphoreType.DMA((2,))]),
        out_shape=jax.ShapeDtypeStruct((n_out * TILE, d), F32),
        compiler_params=_params(("arbitrary",)),
        name="moe_combine",
    )(pos, ys, pack, x, gpost, mod)


def _reorder_w_in(w):
    a = GLA_QK * 2 + GLA_V * 2
    c0 = a + 2 * GLA_RANK
    g0 = c0 + 3 * NAT_W
    wb = w.astype(BF16)
    pad = jnp.zeros(w.shape[:2] + (LANES - 2 * GLA_RANK,), BF16)
    return jnp.concatenate([wb[..., :a], wb[..., g0:], wb[..., c0:g0], wb[..., a:c0], pad], axis=-1)


def _rope_tables(nt):
    t = np.arange(nt * TILE)
    pos = np.stack([t // GRID_W, t % GRID_W], axis=1).astype(np.float32)
    nf = GLA_DK // 4
    inv = (ROPE_BASE ** (-jnp.arange(nf, dtype=F32) / nf))
    ang = jnp.asarray(pos)[:, :, None] * inv[None, None, :]
    cos = jnp.concatenate([jnp.cos(ang), jnp.cos(ang)], axis=-1).reshape(nt * TILE, GLA_DK)
    sin = jnp.concatenate([-jnp.sin(ang), jnp.sin(ang)], axis=-1).reshape(nt * TILE, GLA_DK)
    ident = jnp.ones((TILE, GLA_DK), F32)
    return (jnp.concatenate([ident, cos], axis=0), jnp.concatenate([jnp.zeros_like(ident), sin], axis=0))


def _chunk_matrices():
    i = np.arange(TILE)
    same = (i[:, None] // GLA_CHUNK) == (i[None, :] // GLA_CHUNK)
    lower = same & (i[None, :] <= i[:, None])
    upper = same & (i[None, :] >= i[:, None])
    cum = jnp.asarray(np.stack([lower, upper]).astype(np.float32), BF16)
    return cum, jnp.asarray(same.astype(np.float32), BF16)


def _hi_lo(w):
    hi = w.astype(BF16)
    return hi, (w - hi.astype(F32)).astype(BF16)


def kernel(x, c, ctx, c_ctx, w_mod, b_mod, norm_mix_pre, norm_mix_post, w_in, gla_gate_w2, gla_gate_b, gla_norm,
           nat_rpb, w_branch_gla, w_branch_nat, w_out, norm_ffn_pre, norm_ffn_post, ffn_w1, ffn_w3, ffn_w2,
           moe_router, moe_w1, moe_w3, moe_w2):
    nb, seq, d = x.shape
    depth = w_mod.shape[0]
    assert ctx.shape[1] == TILE and seq % TILE == 0 and d % LANES == 0
    nt = seq // TILE
    mod_rows = -(-(nb + 1) // 8) * 8
    c_rows = jnp.concatenate([c, c_ctx[None], jnp.zeros((mod_rows - nb - 1, d), c.dtype)], axis=0)
    mod_all = _modulation(c_rows, w_mod, b_mod).reshape(depth, mod_rows, 6, 1, d)
    xs = jnp.concatenate([ctx.reshape(nb * TILE, d), x.reshape(nb * seq, d)], axis=0)
    rope_cos, rope_sin = _rope_tables(nt)
    cum_m, tot_m = _chunk_matrices()
    row = lambda v: v.reshape(1, -1)
    w_in_b = _reorder_w_in(w_in)
    wbg_b, wbn_b, wo_b = w_branch_gla.astype(BF16), w_branch_nat.astype(BF16), w_out.astype(BF16)
    ffn_b = (ffn_w1, ffn_w3, ffn_w2)
    moe_b = (moe_w1, moe_w3, moe_w2)
    nat_bias = _nat_bias_base(nat_rpb)
    nat_masks = _nat_row_masks(nt)
    w2p = jnp.zeros((depth, 2, LANES, GLA_QK), F32)
    w2p = w2p.at[:, 0, :GLA_RANK].set(gla_gate_w2[:, 0]).at[:, 1, GLA_RANK:2 * GLA_RANK].set(gla_gate_w2[:, 1])
    w2hi, w2lo = _hi_lo(w2p)
    wr = jnp.zeros((moe_router.shape[0], d, LANES), F32).at[:, :, :N_EXPERTS].set(moe_router)
    wr_hi, wr_lo = _hi_lo(wr)
    for i in range(depth):
        mod = mod_all[i]
        last = i == depth - 1
        y = _in_projection(xs, row(norm_mix_pre[i]), mod, rope_cos, rope_sin, w_in_b, i, nb, nt)
        of, ob = _gla(y, w2hi[i], w2lo[i], gla_gate_b[i].reshape(2, 1, GLA_QK), cum_m, tot_m, nb, nt)
        on = _nat(y, nat_bias, i, nat_masks, nb, nt)
        is_moe = i % 2 == 1
        j = i // 2
        router = (wr_hi[j], wr_lo[j]) if is_moe else None
        skip = last and is_moe
        outs = _merge(y, of, ob, on, xs, row(gla_norm[i]), wbg_b, wbn_b, wo_b, i, row(norm_mix_post[i]),
                      row(norm_ffn_pre[i]), mod, router, nb, nt, skip_context=skip)
        if is_moe:
            xs, h, pack, counts = outs
            xs = _moe_routed(h, pack, counts, *moe_b, j, xs, row(norm_ffn_post[i]), mod, nb, nt,
                             nctx=0 if skip else nb)
        else:
            xs, h = outs
            xs = _ffn(h, *ffn_b, j, xs, row(norm_ffn_post[i]), mod, nb, nt)
            if last:
                xs = xs[nb * TILE:]
    return xs.reshape(nb, seq, d)
```

```python
import functools

import numpy as np
import jax
import jax.numpy as jnp
from jax import lax
from jax.experimental import pallas as pl
from jax.experimental.pallas import tpu as pltpu

EPS = 1e-6
GRID_W = 64
TILE = 256
TILE_ROWS = TILE // GRID_W
GLA_HEADS, GLA_DK, GLA_DV, GLA_RANK, GLA_TAU, GLA_CHUNK = 4, 128, 256, 16, 16.0, 64
NAT_HEADS, NAT_DH = 8, 64
WIN_H, WIN_W = 8, 16
N_EXPERTS = 8
ROPE_BASE = 10000.0
GLA_QK = GLA_HEADS * GLA_DK
GLA_V = GLA_HEADS * GLA_DV
NAT_W = NAT_HEADS * NAT_DH
LANES = 128
COL_GQ, COL_GK, COL_GV, COL_GR, COL_GA, COL_GB = 0, 512, 1024, 2048, 3072, 4096
COL_NQ, COL_NK, COL_NV, COL_CODE = 5120, 5632, 6144, 6656
N_IN = COL_CODE + LANES
VMEM_LIMIT = 56 * 1024 * 1024
BF16 = jnp.bfloat16
F32 = jnp.float32


def _dot(a, b):
    return jnp.dot(a, b, preferred_element_type=F32)


def _dot_nt(a, b):
    return lax.dot_general(a, b, (((1,), (1,)), ((), ())), preferred_element_type=F32)


def _dot_tn(a, b):
    return lax.dot_general(a, b, (((0,), (0,)), ((), ())), preferred_element_type=F32)


def _rms(x, g):
    return x * lax.rsqrt(jnp.mean(x * x, axis=-1, keepdims=True) + EPS) * g


def _split_dot(a, b_hi, b_lo):
    a_hi = a.astype(BF16)
    a_lo = (a - a_hi.astype(F32)).astype(BF16)
    return _dot(a_hi, b_hi) + (_dot(a_lo, b_hi) + _dot(a_hi, b_lo))


def _params(sem):
    return pltpu.CompilerParams(dimension_semantics=sem, vmem_limit_bytes=VMEM_LIMIT)


def _mod_kernel(c_ref, w_ref, b_ref, o_ref):
    c = c_ref[...]
    s = c * jax.nn.sigmoid(c)
    o_ref[0] = _dot(s.astype(BF16), w_ref[0].astype(BF16)) + b_ref[0]


def _modulation(c_rows, w_mod, b_mod):
    depth, d, n = w_mod.shape
    rows = c_rows.shape[0]
    tn = 1536
    return pl.pallas_call(
        _mod_kernel,
        grid=(depth, n // tn),
        in_specs=[pl.BlockSpec((rows, d), lambda l, j: (0, 0)),
                  pl.BlockSpec((1, d, tn), lambda l, j: (l, 0, j)),
                  pl.BlockSpec((1, 1, tn), lambda l, j: (l, 0, j))],
        out_specs=pl.BlockSpec((1, rows, tn), lambda l, j: (l, 0, j)),
        out_shape=jax.ShapeDtypeStruct((depth, rows, n), F32),
        compiler_params=_params(("arbitrary", "arbitrary")),
        name="modulation",
    )(c_rows, w_mod, b_mod.reshape(depth, 1, n))


def _inproj_kernel(x_ref, g_ref, sh_ref, sc_ref, cos_ref, sin_ref, w_ref, o_ref):
    h = _rms(x_ref[...], g_ref[...]) * (1.0 + sc_ref[...]) + sh_ref[...]
    hb = h.astype(BF16)
    n = o_ref.shape[1]
    qk = _dot(hb, w_ref[:, :2 * GLA_QK])
    lane = lax.broadcasted_iota(jnp.int32, qk.shape, 1)
    reps = 2 * GLA_HEADS
    cos = jnp.concatenate([cos_ref[...]] * reps, axis=1)
    sin = jnp.concatenate([sin_ref[...]] * reps, axis=1)
    partner = jnp.where((lane % 64) < 32, pltpu.roll(qk, 2 * GLA_QK - 32, 1), pltpu.roll(qk, 32, 1))
    qk = qk * cos + partner * sin
    qk = jnp.where(lane < GLA_QK, qk * (GLA_DK ** -0.5), qk)
    o_ref[:, :2 * GLA_QK] = qk.astype(o_ref.dtype)
    step = 1024
    for j in range(2 * GLA_QK, n, step):
        w = min(step, n - j)
        o_ref[:, j:j + w] = _dot(hb, w_ref[:, j:j + w]).astype(o_ref.dtype)


def _in_projection(x, g, mod, rope_cos, rope_sin, w, layer, nb, nt):
    t, d = x.shape
    n = w.shape[2]
    row_of_tile = lambda i: jnp.where(i < nb, nb, (i - nb) // nt)
    rope_tile = lambda i: jnp.where(i < nb, 0, 1 + (i - nb) % nt)
    dm = mod.shape[-1]
    mspec = lambda comp: pl.BlockSpec((None, None, 1, dm), lambda i: (row_of_tile(i), comp, 0, 0))
    return pl.pallas_call(
        _inproj_kernel,
        grid=(t // TILE,),
        in_specs=[pl.BlockSpec((TILE, d), lambda i: (i, 0)),
                  pl.BlockSpec((1, d), lambda i: (0, 0)),
                  mspec(0), mspec(1),
                  pl.BlockSpec((TILE, LANES), lambda i: (rope_tile(i), 0)),
                  pl.BlockSpec((TILE, LANES), lambda i: (rope_tile(i), 0)),
                  pl.BlockSpec((None, d, n), lambda i: (layer, 0, 0))],
        out_specs=pl.BlockSpec((TILE, n), lambda i: (i, 0)),
        out_shape=jax.ShapeDtypeStruct((t, n), BF16),
        compiler_params=_params(("arbitrary",)),
        name="in_projection",
    )(x, g, mod, mod, rope_cos, rope_sin, w)


def _gla_prepare(d, q_ref, k_ref, code_ref, w2hi_ref, w2lo_ref, gb_ref, cum_ref, tot_ref, ops_ref):
    z = _dot(code_ref[...], w2hi_ref[d]) + _dot(code_ref[...], w2lo_ref[d]) + gb_ref[d]
    lg = (jnp.minimum(z, 0.0) - jnp.log(1.0 + jnp.exp(-jnp.abs(z)))) * (1.0 / GLA_TAU)
    lg_hi = lg.astype(BF16)
    lg_lo = (lg - lg_hi.astype(F32)).astype(BF16)
    cum = _dot(cum_ref[d], lg_hi) + _dot(cum_ref[d], lg_lo)
    tot = _dot(tot_ref[...], lg_hi) + _dot(tot_ref[...], lg_lo)
    k = k_ref[...].astype(F32)
    qe = q_ref[...].astype(F32) * jnp.exp(cum)
    kd = k * jnp.exp(tot - cum)
    ops_ref[d, OP_QE] = qe.astype(BF16)
    ops_ref[d, OP_KE] = (k * jnp.exp(-cum)).astype(BF16)
    n = TILE // GLA_CHUNK
    order = list(range(n)) if d == 0 else list(range(n - 1, -1, -1))
    tc = [tot[c * GLA_CHUNK:c * GLA_CHUNK + 1, :] for c in order]
    zero = jnp.zeros_like(tc[0])
    u_log = [zero, -tc[1], zero, tc[2]]
    w_log = [tc[1], zero, -tc[2], zero]
    for p, c in enumerate(order):
        rows = slice(c * GLA_CHUNK, (c + 1) * GLA_CHUNK)
        before = sum(tc[:p], zero)
        after = sum(tc[p + 1:], zero)
        ops_ref[d, OP_QU, rows] = (qe[rows] * jnp.exp(u_log[p])).astype(BF16)
        ops_ref[d, OP_KW, rows] = (kd[rows] * jnp.exp(w_log[p])).astype(BF16)
        ops_ref[d, OP_QP, rows] = (qe[rows] * jnp.exp(before)).astype(BF16)
        ops_ref[d, OP_KS, rows] = (kd[rows] * jnp.exp(after)).astype(BF16)
    return jnp.exp(sum(tc, zero))


OP_QE, OP_KE, OP_QU, OP_KW, OP_QP, OP_KS = range(6)


def _gla_kernel(qf, kf, vf, cf, qb, kb, vb, cb, w2hi, w2lo, gb, cum, tot, of, ob, st_ref, ops_ref):
    assert TILE // GLA_CHUNK == 4

    @pl.when(pl.program_id(1) == 0)
    def _():
        st_ref[...] = jnp.zeros_like(st_ref)

    decay = [_gla_prepare(0, qf, kf, cf, w2hi, w2lo, gb, cum, tot, ops_ref),
             _gla_prepare(1, qb, kb, cb, w2hi, w2lo, gb, cum, tot, ops_ref)]
    r = lax.broadcasted_iota(jnp.int32, (TILE, TILE), 0)
    s = lax.broadcasted_iota(jnp.int32, (TILE, TILE), 1)
    for d, (v_ref, o_ref) in enumerate(((vf, of), (vb, ob))):
        gap = (r // GLA_CHUNK - s // GLA_CHUNK) * (1 if d == 0 else -1)
        diag = (gap == 0) & ((s <= r) if d == 0 else (s >= r))
        for h in range(GLA_HEADS):
            kcols = slice(h * GLA_DK, (h + 1) * GLA_DK)
            vcols = slice(h * GLA_DV, (h + 1) * GLA_DV)
            op = lambda which: ops_ref[d, which, :, kcols]
            a = jnp.where(diag, _dot_nt(op(OP_QE), op(OP_KE)),
                          jnp.where(gap >= 1, _dot_nt(op(OP_QU), op(OP_KW)), 0.0))
            v = v_ref[:, vcols]
            st = st_ref[d, h]
            o = _dot(a.astype(BF16), v) + _dot_nt(op(OP_QP), st.astype(BF16))
            o_ref[:, vcols] = o.astype(o_ref.dtype)
            st_ref[d, h] = st * decay[d][:, kcols] + _dot_tn(v, op(OP_KS))


def _gla(y, w2hi, w2lo, gbias, cum_m, tot_m, nb, nt):
    t = y.shape[0]
    fwd = lambda b, j: jnp.where(j == 0, b, nb + b * nt + j - 1)
    bwd = lambda b, j: jnp.where(j == 0, b, nb + b * nt + nt - j)

    def ysl(width, col, tile):
        return pl.BlockSpec((TILE, width), lambda b, j: (tile(b, j), col // width))

    def direction(tile):
        return [ysl(GLA_QK, COL_GQ, tile), ysl(GLA_QK, COL_GK, tile), ysl(GLA_V, COL_GV, tile),
                ysl(LANES, COL_CODE, tile)]

    const = lambda shape: pl.BlockSpec(shape, lambda b, j: (0,) * len(shape))
    in_specs = (direction(fwd) + direction(bwd)
                + [const(w2hi.shape), const(w2lo.shape), const(gbias.shape), const(cum_m.shape), const(tot_m.shape)])
    args = [y] * 8 + [w2hi, w2lo, gbias, cum_m, tot_m]
    return pl.pallas_call(
        _gla_kernel,
        grid=(nb, nt + 1),
        in_specs=in_specs,
        out_specs=[pl.BlockSpec((TILE, GLA_V), lambda b, j: (fwd(b, j), 0)),
                   pl.BlockSpec((TILE, GLA_V), lambda b, j: (bwd(b, j), 0))],
        out_shape=[jax.ShapeDtypeStruct((t, GLA_V), BF16)] * 2,
        scratch_shapes=[pltpu.VMEM((2, GLA_HEADS, GLA_DV, GLA_DK), F32),
                        pltpu.VMEM((2, 6, TILE, GLA_QK), BF16)],
        compiler_params=_params(("arbitrary", "arbitrary")),
        name="gla_scan",
    )(*args)


def _nat_kernel(q_ref, k0, k1, k2, kc, v0, v1, v2, vc, bias_ref, mask_ref, o_ref):
    lane = lax.broadcasted_iota(jnp.int32, (TILE, LANES), 1)
    low = lane < NAT_DH
    row_mask = mask_ref[...]
    for p in range(NAT_HEADS // 2):
        cols = slice(p * LANES, (p + 1) * LANES)
        qp = q_ref[:, cols] * (NAT_DH ** -0.5)
        ks = [r[:, cols] for r in (k0, k1, k2, kc)]
        vs = [r[:, cols] for r in (v0, v1, v2, vc)]
        zero = jnp.zeros_like(qp)
        qs = jnp.concatenate([jnp.where(low, qp, zero), jnp.where(low, zero, qp)], axis=0)
        s_lat2 = jnp.concatenate([_dot_nt(qs, kk) for kk in ks[:3]], axis=1)
        s_ctx2 = _dot_nt(qs, ks[3])
        e_lats, e_ctxs, dens = [], [], []
        for half in range(2):
            rows = slice(half * TILE, (half + 1) * TILE)
            s_lat = s_lat2[rows] + (bias_ref[2 * p + half] + row_mask)
            s_ctx = s_ctx2[rows]
            m = jnp.maximum(jnp.max(s_lat, axis=-1, keepdims=True), jnp.max(s_ctx, axis=-1, keepdims=True))
            e_lat = jnp.exp(s_lat - m)
            e_ctx = jnp.exp(s_ctx - m)
            dens.append(jnp.sum(e_lat, axis=-1, keepdims=True) + jnp.sum(e_ctx, axis=-1, keepdims=True))
            e_lats.append(e_lat.astype(BF16))
            e_ctxs.append(e_ctx.astype(BF16))
        e_lat2 = jnp.concatenate(e_lats, axis=0)
        acc = _dot(jnp.concatenate(e_ctxs, axis=0), vs[3])
        for i in range(3):
            acc += _dot(e_lat2[:, i * TILE:(i + 1) * TILE], vs[i])
        o_ref[:, cols] = jnp.where(low, acc[:TILE] / dens[0], acc[TILE:] / dens[1]).astype(o_ref.dtype)


def _nat(y, bias, layer, row_masks, nb, nt):
    t = y.shape[0]
    lat = lambda b, tt: nb + b * nt + jnp.clip(tt, 0, nt - 1)
    qtile = lambda b, j: jnp.where(j == 0, b, nb + b * nt + j - 1)
    pattern = lambda b, j: jnp.where(j == 0, 3, jnp.where(j == 1, 0, jnp.where(j == nt, 2, 1)))

    def ysl(col, tile):
        return pl.BlockSpec((TILE, NAT_W), lambda b, j: (tile(b, j), col // NAT_W))

    slab = [lambda b, j: lat(b, j - 2), lambda b, j: lat(b, j - 1), lambda b, j: lat(b, j), lambda b, j: b]
    in_specs = ([ysl(COL_NQ, qtile)] + [ysl(COL_NK, s) for s in slab] + [ysl(COL_NV, s) for s in slab]
                + [pl.BlockSpec((None, NAT_HEADS, TILE, 3 * TILE), lambda b, j: (layer, 0, 0, 0)),
                   pl.BlockSpec((None, TILE, 3 * TILE), lambda b, j: (pattern(b, j), 0, 0))])
    return pl.pallas_call(
        _nat_kernel,
        grid=(nb, nt + 1),
        in_specs=in_specs,
        out_specs=pl.BlockSpec((TILE, NAT_W), lambda b, j: (qtile(b, j), 0)),
        out_shape=jax.ShapeDtypeStruct((t, NAT_W), BF16),
        compiler_params=_params(("arbitrary", "arbitrary")),
        name="nat_attention",
    )(*([y] * 9), bias, row_masks)


def _nat_bias_base(rpb):
    nr, nc = 2 * WIN_H - 1, 2 * WIN_W - 1
    qi = np.arange(TILE_ROWS)
    kj = np.arange(3 * TILE_ROWS)
    qc = np.arange(GRID_W)
    kc = np.arange(GRID_W)
    col_start = np.clip(qc - WIN_W // 2, 0, GRID_W - WIN_W)
    col_ok = (kc[None, :] >= col_start[:, None]) & (kc[None, :] < col_start[:, None] + WIN_W)
    col_off = np.clip(kc[None, :] - qc[:, None], 1 - WIN_W, WIN_W - 1) + (WIN_W - 1)
    row_off = kj[None, :] - qi[:, None] + (WIN_H - 1 - TILE_ROWS)
    assert row_off.min() >= 0 and row_off.max() < nr
    col_sel = ((col_off[None] == np.arange(nc)[:, None, None]) & col_ok[None]).astype(np.float32)
    row_sel = (row_off[None] == np.arange(nr)[:, None, None]).astype(np.float32)
    b = jnp.einsum("dhab,aik,bcl->dhickl", rpb.astype(F32), row_sel, col_sel, precision=lax.Precision.HIGHEST)
    b = b + jnp.where(jnp.asarray(col_ok), 0.0, -jnp.inf)[None, None, None, :, None, :]
    return b.reshape(rpb.shape[0], NAT_HEADS, TILE, 3 * TILE)


def _nat_row_masks(nt):
    rows = nt * TILE_ROWS
    kh = min(WIN_H, rows)
    qi = np.arange(TILE_ROWS)
    kj = np.arange(3 * TILE_ROWS)
    masks = []
    for t in (0, min(1, nt - 1), nt - 1):
        r = TILE_ROWS * t + qi
        row_start = np.clip(r - kh // 2, 0, rows - kh)
        kr = TILE_ROWS * (t - 1) + kj
        valid = (kr >= 0) & (kr < rows)
        masks.append((kr[None, :] >= row_start[:, None]) & (kr[None, :] < row_start[:, None] + kh) & valid[None, :])
    masks.append(np.zeros_like(masks[0]))
    m = np.where(np.stack(masks), 0.0, -np.inf).astype(np.float32)
    m = jnp.broadcast_to(jnp.asarray(m)[:, :, None, :, None], (4, TILE_ROWS, GRID_W, 3 * TILE_ROWS, GRID_W))
    return m.reshape(4, TILE, 3 * TILE)


def _merge_kernel(of_ref, ob_ref, gr_ref, ga_ref, gb_ref, on_ref, x_ref, gain_ref, wbg_ref, wbn_ref, wo_ref,
                  gpost_ref, m2_ref, gpre_ref, m3_ref, m4_ref, *rest, with_router):
    if with_router:
        wrhi_ref, wrlo_ref, x_out, h_out, gate_out, count_out, count_ref = rest
    else:
        x_out, h_out = rest
    o = of_ref[...].astype(F32) + ob_ref[...].astype(F32)
    parts = []
    for h in range(GLA_HEADS):
        oh = o[:, h * GLA_DV:(h + 1) * GLA_DV]
        parts.append(oh * lax.rsqrt(jnp.mean(oh * oh, axis=-1, keepdims=True) + EPS))
    r = gr_ref[...].astype(F32)
    yg = jnp.concatenate(parts, axis=1) * gain_ref[...] * (r * jax.nn.sigmoid(r))
    m = (jax.nn.sigmoid(ga_ref[...].astype(F32)) * _dot(yg.astype(BF16), wbg_ref[...])
         + jax.nn.sigmoid(gb_ref[...].astype(F32)) * _dot(on_ref[...], wbn_ref[...]))
    y = _dot(m.astype(BF16), wo_ref[...])
    x1 = x_ref[...] + m2_ref[...] * _rms(y, gpost_ref[...])
    x_out[...] = x1
    h = _rms(x1, gpre_ref[...]) * (1.0 + m4_ref[...]) + m3_ref[...]
    h_out[...] = h.astype(h_out.dtype)
    if with_router:
        logits = _split_dot(h, wrhi_ref[...], wrlo_ref[...])
        lane = lax.broadcasted_iota(jnp.int32, logits.shape, 1)
        lg = jnp.where(lane < N_EXPERTS, logits, -jnp.inf)
        v1 = jnp.max(lg, axis=-1, keepdims=True)
        i1 = jnp.min(jnp.where(lg == v1, lane, LANES), axis=-1, keepdims=True)
        lg2 = jnp.where(lane == i1, -jnp.inf, lg)
        v2 = jnp.max(lg2, axis=-1, keepdims=True)
        i2 = jnp.min(jnp.where(lg2 == v2, lane, LANES), axis=-1, keepdims=True)
        e2 = jnp.exp(v2 - v1)
        w1 = 1.0 / (1.0 + e2)
        w2 = e2 / (1.0 + e2)

        @pl.when(pl.program_id(0) == 0)
        def _():
            count_ref[...] = jnp.zeros_like(count_ref)

        oh1 = (lane == i1).astype(F32)
        oh2 = (lane == i2).astype(F32)
        rr = lax.broadcasted_iota(jnp.int32, (TILE, TILE), 0)
        ss = lax.broadcasted_iota(jnp.int32, (TILE, TILE), 1)
        before = (ss < rr).astype(BF16)
        carry = count_ref[...]
        cnt1 = jnp.sum(oh1, axis=0, keepdims=True)
        rank1 = jnp.sum(oh1 * (_dot(before, oh1.astype(BF16)) + carry), axis=-1, keepdims=True)
        rank2 = jnp.sum(oh2 * (_dot(before, oh2.astype(BF16)) + (carry + cnt1)), axis=-1, keepdims=True)
        total = carry + cnt1 + jnp.sum(oh2, axis=0, keepdims=True)
        count_ref[...] = total
        count_out[...] = jnp.broadcast_to(total, count_out.shape)
        packed = jnp.zeros_like(logits)
        for ln, val in ((8, i1.astype(F32)), (9, i2.astype(F32)), (10, w1), (11, w2), (12, rank1), (13, rank2)):
            packed = jnp.where(lane == ln, val, packed)
        gate_out[...] = packed


def _merge(y, of, ob, on, x, gain, wbg, wbn, wo, layer, gpost, gpre, mod, router, nb, nt, skip_context):
    d = x.shape[1]
    tile0 = nb if skip_context else 0
    t = x.shape[0] - tile0 * TILE
    row_of_tile = lambda i: jnp.where(i + tile0 < nb, nb, (i + tile0 - nb) // nt)
    mspec = lambda comp: pl.BlockSpec((None, None, 1, d), lambda i: (row_of_tile(i), comp, 0, 0))
    rowblk = lambda w: pl.BlockSpec((TILE, w), lambda i: (i + tile0, 0))
    outblk = lambda w: pl.BlockSpec((TILE, w), lambda i: (i, 0))
    ycol = lambda w, col: pl.BlockSpec((TILE, w), lambda i: (i + tile0, col // w))
    const = lambda a: pl.BlockSpec(a.shape, lambda i: (0,) * a.ndim)
    stacked = lambda a: pl.BlockSpec((None,) + a.shape[1:], lambda i: (layer,) + (0,) * (a.ndim - 1))
    in_specs = [rowblk(GLA_V), rowblk(GLA_V), ycol(GLA_V, COL_GR), ycol(d, COL_GA), ycol(d, COL_GB), rowblk(NAT_W),
                rowblk(d), const(gain), stacked(wbg), stacked(wbn), stacked(wo), const(gpost), mspec(2), const(gpre),
                mspec(3), mspec(4)]
    args = [of, ob, y, y, y, on, x, gain, wbg, wbn, wo, gpost, mod, gpre, mod, mod]
    out_specs = [outblk(d), outblk(d)]
    out_shape = [jax.ShapeDtypeStruct((t, d), F32), jax.ShapeDtypeStruct((t, d), BF16)]
    scratch = []
    if router is not None:
        in_specs += [const(router[0]), const(router[1])]
        args += list(router)
        out_specs += [outblk(LANES), pl.BlockSpec((8, LANES), lambda i: (0, 0))]
        out_shape += [jax.ShapeDtypeStruct((t, LANES), F32), jax.ShapeDtypeStruct((8, LANES), F32)]
        out_shape[1] = jax.ShapeDtypeStruct((t, d), F32)
        scratch = [pltpu.VMEM((1, LANES), F32)]
    return pl.pallas_call(
        functools.partial(_merge_kernel, with_router=router is not None),
        grid=(t // TILE,),
        in_specs=in_specs,
        out_specs=out_specs,
        out_shape=out_shape,
        scratch_shapes=scratch,
        compiler_params=_params(("arbitrary",)),
        name="merge_router" if router is not None else "merge",
    )(*args)


def _ffn_kernel(h_ref, w1_ref, w3_ref, w2_ref, x_ref, gpost_ref, m5_ref, o_ref, acc_ref):
    f = pl.program_id(1)

    @pl.when(f == 0)
    def _():
        acc_ref[...] = jnp.zeros_like(acc_ref)

    h = h_ref[...]
    a = _dot(h, w1_ref[...].astype(BF16))
    u = (a * jax.nn.sigmoid(a)) * _dot(h, w3_ref[...].astype(BF16))
    acc_ref[...] += _dot(u.astype(BF16), w2_ref[...].astype(BF16))

    @pl.when(f == pl.num_programs(1) - 1)
    def _():
        o_ref[...] = x_ref[...] + m5_ref[...] * _rms(acc_ref[...], gpost_ref[...])


def _dispatch_kernel(pos_ref, h_ref, xs_in, xs_out, sem):
    del xs_in
    t = pl.num_programs(0) * TILE
    base = pl.program_id(0) * TILE

    def copies(r):
        src = h_ref.at[pl.ds(r, 1)]
        return (pltpu.make_async_copy(src, xs_out.at[pl.ds(pos_ref[base + r], 1)], sem),
                pltpu.make_async_copy(src, xs_out.at[pl.ds(pos_ref[t + base + r], 1)], sem))

    def start(r, carry):
        for queue, cp in enumerate(copies(r)):
            cp.start(priority=queue)
        return carry

    def wait(r, carry):
        for cp in copies(r):
            cp.wait()
        return carry

    for r in range(TILE):
        start(r, 0)
    lax.fori_loop(0, TILE, wait, 0, unroll=8)


def _moe_ffn_kernel(te_ref, nu_ref, x_ref, w1_ref, w3_ref, w2_ref, o_ref, xb_ref):
    del te_ref
    i = pl.program_id(0)
    f = pl.program_id(1)

    @pl.when(i < nu_ref[0])
    def _():
        @pl.when(f == 0)
        def _():
            xb_ref[...] = x_ref[...].astype(BF16)

        h = xb_ref[...]
        a = _dot(h, w1_ref[0].astype(BF16))
        u = (a * jax.nn.sigmoid(a)) * _dot(h, w3_ref[0].astype(BF16))
        y = _dot(u.astype(BF16), w2_ref[0].astype(BF16))

        @pl.when(f == 0)
        def _():
            o_ref[...] = y

        @pl.when(f > 0)
        def _():
            o_ref[...] += y

    @pl.when((i >= nu_ref[0]) & (f == 0))
    def _():
        o_ref[...] = jnp.zeros_like(o_ref)


def _combine_kernel(pos_ref, ys_hbm, g_ref, x_ref, gpost_ref, m5_ref, o_ref, buf_ref, sem, *, t, tile0):
    i = pl.program_id(0)
    n = pl.num_programs(0)

    def copies(step, slot, r):
        base = (step + tile0) * TILE
        return (pltpu.make_async_copy(ys_hbm.at[pl.ds(pos_ref[base + r], 1)],
                                      buf_ref.at[slot, 0, pl.ds(r, 1)], sem.at[slot]),
                pltpu.make_async_copy(ys_hbm.at[pl.ds(pos_ref[t + base + r], 1)],
                                      buf_ref.at[slot, 1, pl.ds(r, 1)], sem.at[slot]))

    def issue(step, slot):
        for r in range(TILE):
            for queue, cp in enumerate(copies(step, slot, r)):
                cp.start(priority=queue)

    @pl.when(i == 0)
    def _():
        issue(0, 0)

    @pl.when(i + 1 < n)
    def _():
        issue(i + 1, (i + 1) % 2)

    slot = i % 2

    def wait(r, carry):
        for cp in copies(i, slot, r):
            cp.wait()
        return carry

    lax.fori_loop(0, TILE, wait, 0, unroll=8)
    g = g_ref[...]
    y = g[:, 10:11] * buf_ref[slot, 0] + g[:, 11:12] * buf_ref[slot, 1]
    o_ref[...] = x_ref[...] + m5_ref[...] * _rms(y, gpost_ref[...])


def _ffn_tiles(t, nb, nt):
    tm = 1024 if (nb * TILE) % 1024 == 0 and (nt * TILE) % 1024 == 0 else TILE
    per = tm // TILE
    row_of_tile = lambda i: jnp.where(i * per < nb, nb, (i * per - nb) // nt)
    return tm, row_of_tile


def _ffn(h, w1, w3, w2, layer, x, gpost, mod, nb, nt):
    t, d = x.shape
    dff = w1.shape[2]
    tm, row_of_tile = _ffn_tiles(t, nb, nt)
    tf = 512
    return pl.pallas_call(
        _ffn_kernel,
        grid=(t // tm, dff // tf),
        in_specs=[pl.BlockSpec((tm, d), lambda i, f: (i, 0)),
                  pl.BlockSpec((None, d, tf), lambda i, f: (layer, 0, f)),
                  pl.BlockSpec((None, d, tf), lambda i, f: (layer, 0, f)),
                  pl.BlockSpec((None, tf, d), lambda i, f: (layer, f, 0)),
                  pl.BlockSpec((tm, d), lambda i, f: (i, 0)),
                  pl.BlockSpec((1, d), lambda i, f: (0, 0)),
                  pl.BlockSpec((None, None, 1, d), lambda i, f: (row_of_tile(i), 5, 0, 0))],
        out_specs=pl.BlockSpec((tm, d), lambda i, f: (i, 0)),
        out_shape=jax.ShapeDtypeStruct((t, d), F32),
        scratch_shapes=[pltpu.VMEM((tm, d), F32)],
        compiler_params=_params(("arbitrary", "arbitrary")),
        name="ffn_dense",
    )(h, w1, w3, w2, x, gpost, mod)


MOE_TM = 1024
MOE_TF = 512


def _moe_routed(h, pack, counts, w1, w3, w2, layer, x, gpost, mod, nb, nt, nctx, n_tiles, sorted_buf):
    t, d = x.shape
    _, ne, _, dff = w1.shape
    tm = MOE_TM
    tf = MOE_TF if dff % MOE_TF == 0 else 512
    assert n_tiles >= -(-(2 * t) // tm) + ne
    p_rows = n_tiles * tm
    cnt = counts[0, :ne].astype(jnp.int32)
    padded = (cnt + tm - 1) // tm * tm
    ends = jnp.cumsum(padded)
    offs = ends - padded
    e1 = pack[:, 8].astype(jnp.int32)
    e2 = pack[:, 9].astype(jnp.int32)
    pos = jnp.concatenate([offs[e1] + pack[:, 12].astype(jnp.int32), offs[e2] + pack[:, 13].astype(jnp.int32)])
    n_used = (ends[-1] // tm).astype(jnp.int32).reshape(1)
    tile_start = jnp.arange(n_tiles, dtype=jnp.int32) * tm
    tile_start = jnp.minimum(tile_start, ends[-1] - tm)
    tile_expert = jnp.minimum(jnp.sum(tile_start[:, None] >= ends[None, :], axis=1), ne - 1).astype(jnp.int32)

    xs = pl.pallas_call(
        _dispatch_kernel,
        grid_spec=pltpu.PrefetchScalarGridSpec(
            num_scalar_prefetch=1, grid=(t // TILE,),
            in_specs=[pl.BlockSpec((TILE, d), lambda i, pos: (i, 0)), pl.BlockSpec(memory_space=pl.ANY)],
            out_specs=pl.BlockSpec(memory_space=pl.ANY),
            scratch_shapes=[pltpu.SemaphoreType.DMA(())]),
        out_shape=jax.ShapeDtypeStruct((p_rows, d), F32),
        input_output_aliases={2: 0},
        compiler_params=_params(("arbitrary",)),
        name="moe_dispatch",
    )(pos, h, sorted_buf)

    nf = dff // tf
    row = lambda i, nu: jnp.minimum(i, nu[0] - 1)
    fcol = lambda i, f, nu: jnp.where(i < nu[0], f, nf - 1)
    ys = pl.pallas_call(
        _moe_ffn_kernel,
        grid_spec=pltpu.PrefetchScalarGridSpec(
            num_scalar_prefetch=2, grid=(n_tiles, nf),
            in_specs=[pl.BlockSpec((tm, d), lambda i, f, te, nu: (row(i, nu), 0)),
                      pl.BlockSpec((None, 1, d, tf), lambda i, f, te, nu: (layer, te[i], 0, fcol(i, f, nu))),
                      pl.BlockSpec((None, 1, d, tf), lambda i, f, te, nu: (layer, te[i], 0, fcol(i, f, nu))),
                      pl.BlockSpec((None, 1, tf, d), lambda i, f, te, nu: (layer, te[i], fcol(i, f, nu), 0))],
            out_specs=pl.BlockSpec((tm, d), lambda i, f, te, nu: (i, 0)),
            scratch_shapes=[pltpu.VMEM((tm, d), BF16)]),
        out_shape=jax.ShapeDtypeStruct((p_rows, d), F32),
        compiler_params=_params(("arbitrary", "arbitrary")),
        name="moe_ffn",
    )(tile_expert, n_used, xs, w1, w3, w2)

    tile0 = 0
    n_out = t // TILE
    row_of_tile = lambda i: jnp.where(i < nctx, nb, (i - nctx) // nt)
    out = pl.pallas_call(
        functools.partial(_combine_kernel, t=t, tile0=tile0),
        grid_spec=pltpu.PrefetchScalarGridSpec(
            num_scalar_prefetch=1, grid=(n_out,),
            in_specs=[pl.BlockSpec(memory_space=pl.ANY),
                      pl.BlockSpec((TILE, LANES), lambda i, pos: (i + tile0, 0)),
                      pl.BlockSpec((TILE, d), lambda i, pos: (i + tile0, 0)),
                      pl.BlockSpec((1, d), lambda i, pos: (0, 0)),
                      pl.BlockSpec((None, None, 1, d), lambda i, pos: (row_of_tile(i), 5, 0, 0))],
            out_specs=pl.BlockSpec((TILE, d), lambda i, pos: (i, 0)),
            scratch_shapes=[pltpu.VMEM((2, 2, TILE, d), F32), pltpu.SemaphoreType.DMA((2,))]),
        out_shape=jax.ShapeDtypeStruct((n_out * TILE, d), F32),
        compiler_params=_params(("arbitrary",)),
        name="moe_combine",
    )(pos, ys, pack, x, gpost, mod)
    return out, xs


def _reorder_w_in(w):
    a = GLA_QK * 2 + GLA_V * 2
    c0 = a + 2 * GLA_RANK
    g0 = c0 + 3 * NAT_W
    wb = w.astype(BF16)
    pad = jnp.zeros(w.shape[:2] + (LANES - 2 * GLA_RANK,), BF16)
    return jnp.concatenate([wb[..., :a], wb[..., g0:], wb[..., c0:g0], wb[..., a:c0], pad], axis=-1)


def _rope_tables(nt):
    t = np.arange(nt * TILE)
    pos = np.stack([t // GRID_W, t % GRID_W], axis=1).astype(np.float32)
    nf = GLA_DK // 4
    inv = (ROPE_BASE ** (-jnp.arange(nf, dtype=F32) / nf))
    ang = jnp.asarray(pos)[:, :, None] * inv[None, None, :]
    cos = jnp.concatenate([jnp.cos(ang), jnp.cos(ang)], axis=-1).reshape(nt * TILE, GLA_DK)
    sin = jnp.concatenate([-jnp.sin(ang), jnp.sin(ang)], axis=-1).reshape(nt * TILE, GLA_DK)
    ident = jnp.ones((TILE, GLA_DK), F32)
    return (jnp.concatenate([ident, cos], axis=0), jnp.concatenate([jnp.zeros_like(ident), sin], axis=0))


def _chunk_matrices():
    i = np.arange(TILE)
    same = (i[:, None] // GLA_CHUNK) == (i[None, :] // GLA_CHUNK)
    lower = same & (i[None, :] <= i[:, None])
    upper = same & (i[None, :] >= i[:, None])
    cum = jnp.asarray(np.stack([lower, upper]).astype(np.float32), BF16)
    return cum, jnp.asarray(same.astype(np.float32), BF16)


def _hi_lo(w):
    hi = w.astype(BF16)
    return hi, (w - hi.astype(F32)).astype(BF16)


def kernel(x, c, ctx, c_ctx, w_mod, b_mod, norm_mix_pre, norm_mix_post, w_in, gla_gate_w2, gla_gate_b, gla_norm,
           nat_rpb, w_branch_gla, w_branch_nat, w_out, norm_ffn_pre, norm_ffn_post, ffn_w1, ffn_w3, ffn_w2,
           moe_router, moe_w1, moe_w3, moe_w2):
    nb, seq, d = x.shape
    depth = w_mod.shape[0]
    assert ctx.shape[1] == TILE and seq % TILE == 0 and d % LANES == 0
    nt = seq // TILE
    mod_rows = -(-(nb + 1) // 8) * 8
    c_rows = jnp.concatenate([c, c_ctx[None], jnp.zeros((mod_rows - nb - 1, d), c.dtype)], axis=0)
    mod_all = _modulation(c_rows, w_mod, b_mod).reshape(depth, mod_rows, 6, 1, d)
    xs = jnp.concatenate([ctx.reshape(nb * TILE, d), x.reshape(nb * seq, d)], axis=0)
    rope_cos, rope_sin = _rope_tables(nt)
    cum_m, tot_m = _chunk_matrices()
    row = lambda v: v.reshape(1, -1)
    w_in_b = _reorder_w_in(w_in)
    wbg_b, wbn_b, wo_b = w_branch_gla.astype(BF16), w_branch_nat.astype(BF16), w_out.astype(BF16)
    ffn_b = (ffn_w1, ffn_w3, ffn_w2)
    moe_b = (moe_w1, moe_w3, moe_w2)
    nat_bias = _nat_bias_base(nat_rpb)
    nat_masks = _nat_row_masks(nt)
    w2p = jnp.zeros((depth, 2, LANES, GLA_QK), F32)
    w2p = w2p.at[:, 0, :GLA_RANK].set(gla_gate_w2[:, 0]).at[:, 1, GLA_RANK:2 * GLA_RANK].set(gla_gate_w2[:, 1])
    w2hi, w2lo = _hi_lo(w2p)
    wr = jnp.zeros((moe_router.shape[0], d, LANES), F32).at[:, :, :N_EXPERTS].set(moe_router)
    wr_hi, wr_lo = _hi_lo(wr)
    moe_tiles = -(-(2 * xs.shape[0]) // MOE_TM) + N_EXPERTS
    sorted_buf = None
    for i in range(depth):
        mod = mod_all[i]
        last = i == depth - 1
        y = _in_projection(xs, row(norm_mix_pre[i]), mod, rope_cos, rope_sin, w_in_b, i, nb, nt)
        of, ob = _gla(y, w2hi[i], w2lo[i], gla_gate_b[i].reshape(2, 1, GLA_QK), cum_m, tot_m, nb, nt)
        on = _nat(y, nat_bias, i, nat_masks, nb, nt)
        is_moe = i % 2 == 1
        j = i // 2
        router = (wr_hi[j], wr_lo[j]) if is_moe else None
        skip = last and is_moe
        outs = _merge(y, of, ob, on, xs, row(gla_norm[i]), wbg_b, wbn_b, wo_b, i, row(norm_mix_post[i]),
                      row(norm_ffn_pre[i]), mod, router, nb, nt, skip_context=skip)
        if is_moe:
            xs, h, pack, counts = outs
            if sorted_buf is None:
                sorted_buf = jnp.zeros((moe_tiles * MOE_TM, d), F32)
            xs, sorted_buf = _moe_routed(h, pack, counts, *moe_b, j, xs, row(norm_ffn_post[i]), mod, nb, nt,
                                         0 if skip else nb, moe_tiles, sorted_buf)
        else:
            xs, h = outs
            xs = _ffn(h, *ffn_b, j, xs, row(norm_ffn_post[i]), mod, nb, nt)
            if last:
                xs = xs[nb * TILE:]
    return xs.reshape(nb, seq, d)
```
